```python
import jax, jax.numpy as jnp
from jax import lax
import numpy as np

D_MODEL = 4096
BATCH = 2
SEQ = 4096
DEPTH = 1

D_MIX = D_MODEL
D_GLA = D_MIX // 2
GLA_HEADS = 4
GLA_DV = D_GLA // GLA_HEADS
GLA_DK = GLA_DV // 2
D_GLA_K = GLA_HEADS * GLA_DK
GLA_LOWRANK = 16
GLA_TAU = 16.0
GLA_CHUNK = 64
D_CONV = D_MIX - D_GLA
CONV_WIDTH = 31
N_EXPERTS = 32
TOP_K = 4
D_EXPERT = D_MODEL // 4
SWIGLU_ALPHA = 1.702
SWIGLU_LIMIT = 7.0
MOE_BLOCK = 128
EPS = 1e-5
SPLITS = (D_GLA_K, 2 * D_GLA_K, 2 * D_GLA_K + D_GLA, 2 * D_GLA_K + 2 * D_GLA,
          2 * D_GLA_K + 2 * D_GLA + GLA_LOWRANK)
D_IN = 2 * D_GLA_K + 2 * D_GLA + GLA_LOWRANK + 2 * D_CONV

kernel_name = "hybrid_gla_conformer_moe"


def rmsnorm(x, g):
    xf = x.astype(jnp.float32)
    y = xf * lax.rsqrt(jnp.mean(xf * xf, axis=-1, keepdims=True) + EPS)
    return (y * g.astype(jnp.float32)).astype(x.dtype)


def gla_chunked(q, k, v, log_a):
    b_, s_ = q.shape[:2]
    n = s_ // GLA_CHUNK

    def to_chunks(t):
        return t.reshape(b_, n, GLA_CHUNK, GLA_HEADS, -1).transpose(1, 0, 3, 2, 4).astype(jnp.float32)

    q, k, v, la = to_chunks(q), to_chunks(k), to_chunks(v), to_chunks(log_a)
    bcum = jnp.cumsum(la, axis=-2)
    b_last = bcum[..., -1:, :]
    q_e = q * jnp.exp(bcum) * (GLA_DK ** -0.5)
    k_e = k * jnp.exp(-bcum)
    k_d = k * jnp.exp(b_last - bcum)
    decay = jnp.exp(b_last[..., 0, :])
    mask = jnp.tril(jnp.ones((GLA_CHUNK, GLA_CHUNK), dtype=bool))
    scores = jnp.einsum('nbhtd,nbhsd->nbhts', q_e, k_e)
    o_intra = jnp.einsum('nbhts,nbhsv->nbhtv', jnp.where(mask, scores, 0.0), v)

    def step(state, xs):
        q_n, k_n, v_n, d_n = xs
        o_n = jnp.einsum('bhtd,bhdv->bhtv', q_n, state)
        state = state * d_n[..., None] + jnp.einsum('bhsd,bhsv->bhdv', k_n, v_n)
        return state, o_n

    s0 = jnp.zeros((b_, GLA_HEADS, GLA_DK, GLA_DV), jnp.float32)
    _, o_inter = lax.scan(step, s0, (q_e, k_d, v, decay))
    o = o_intra + o_inter
    return o.transpose(1, 0, 3, 2, 4).reshape(b_, s_, GLA_HEADS, GLA_DV)


def hybrid_mixer(xn, w_in, w_a2, b_a, gla_norm_g, b_glu, w_dw, b_dw, ln_g, ln_b, w_out):
    b_, s_ = xn.shape[:2]
    proj = xn @ w_in
    q, k, v, r, a_low, glu = jnp.split(proj, SPLITS, axis=-1)
    log_a = jax.nn.log_sigmoid((a_low @ w_a2 + b_a).astype(jnp.float32)) / GLA_TAU
    heads = lambda t: t.reshape(b_, s_, GLA_HEADS, -1)
    o = gla_chunked(heads(q), heads(k), heads(v), heads(log_a))
    o = o * lax.rsqrt(jnp.mean(o * o, axis=-1, keepdims=True) + EPS) * gla_norm_g.astype(jnp.float32)
    gla_out = (o.reshape(b_, s_, D_GLA) * jax.nn.silu(r.astype(jnp.float32))).astype(xn.dtype)
    glu = glu + b_glu
    u = glu[..., :D_CONV] * jax.nn.sigmoid(glu[..., D_CONV:])
    u = lax.conv_general_dilated(u, w_dw[:, None, :], window_strides=(1,),
                                 padding=[(CONV_WIDTH - 1, 0)],
                                 dimension_numbers=('NWC', 'WIO', 'NWC'),
                                 feature_group_count=D_CONV) + b_dw
    uf = u.astype(jnp.float32)
    mu = jnp.mean(uf, axis=-1, keepdims=True)
    var = jnp.mean(jnp.square(uf - mu), axis=-1, keepdims=True)
    un = (uf - mu) * lax.rsqrt(var + EPS) * ln_g.astype(jnp.float32) + ln_b.astype(jnp.float32)
    conv_out = jax.nn.silu(un).astype(xn.dtype)
    return jnp.concatenate([gla_out, conv_out], axis=-1) @ w_out


def moe_ffn(h, w_router, b_router, w1, b1, w2, b2):
    b_, s_, d_ = h.shape
    t_ = b_ * s_
    tk = t_ * TOP_K
    hf = h.reshape(t_, d_)
    logits = (hf @ w_router + b_router).astype(jnp.float32)
    top_vals, top_idx = lax.top_k(logits, TOP_K)
    gates = jax.nn.softmax(top_vals, axis=-1)
    e_flat = top_idx.reshape(tk).astype(jnp.int32)
    tok_flat = jnp.repeat(jnp.arange(t_, dtype=jnp.int32), TOP_K)
    g_flat = gates.reshape(tk)
    order = jnp.argsort(e_flat, stable=True)
    e_s, tok_s, g_s = e_flat[order], tok_flat[order], g_flat[order]
    counts = jnp.bincount(e_flat, length=N_EXPERTS).astype(jnp.int32)
    starts = jnp.cumsum(counts) - counts
    padded = (counts + MOE_BLOCK - 1) // MOE_BLOCK * MOE_BLOCK
    pad_ends = jnp.cumsum(padded)
    pad_starts = pad_ends - padded
    slot = pad_starts[e_s] + (jnp.arange(tk, dtype=jnp.int32) - starts[e_s])
    n_blocks = -(-tk // MOE_BLOCK) + N_EXPERTS
    n_slots = n_blocks * MOE_BLOCK
    slot_tok = jnp.zeros((n_slots,), jnp.int32).at[slot].set(tok_s)
    slot_gate = jnp.zeros((n_slots,), jnp.float32).at[slot].set(g_s)
    block_rows = jnp.arange(n_blocks, dtype=jnp.int32) * MOE_BLOCK
    block_expert = jnp.clip(jnp.searchsorted(pad_ends, block_rows, side='right'), 0, N_EXPERTS - 1)

    def expert_block(args):
        tok_b, gate_b, e = args
        xb = hf[tok_b]
        hid = xb @ w1[e] + b1[e]
        x_glu, x_lin = hid[:, :D_EXPERT], hid[:, D_EXPERT:]
        x_glu = jnp.minimum(x_glu, SWIGLU_LIMIT)
        x_lin = jnp.clip(x_lin, -SWIGLU_LIMIT, SWIGLU_LIMIT)
        act = x_glu * jax.nn.sigmoid(SWIGLU_ALPHA * x_glu) * (x_lin + 1.0)
        y = act @ w2[e] + b2[e]
        return y * gate_b[:, None].astype(y.dtype)

    ys = lax.map(expert_block, (slot_tok.reshape(n_blocks, MOE_BLOCK),
                                slot_gate.reshape(n_blocks, MOE_BLOCK), block_expert))
    out = jnp.zeros((t_, d_), ys.dtype).at[slot_tok].add(ys.reshape(n_slots, d_))
    return out.reshape(b_, s_, d_).astype(h.dtype)


def setup_inputs(seed: int = 0) -> dict:
    key = jax.random.key(seed)
    ks = jax.random.split(key, 20)
    f32 = jnp.float32
    nrm = lambda k, shape, scale: jax.random.normal(k, shape, f32) * scale
    L = DEPTH
    return {
        "x": nrm(ks[0], (BATCH, SEQ, D_MODEL), 1.0),
        "attn_norm_g": 1.0 + nrm(ks[1], (L, D_MODEL), 0.02),
        "w_in": nrm(ks[2], (L, D_MODEL, D_IN), D_MODEL ** -0.5),
        "w_a2": nrm(ks[3], (L, GLA_LOWRANK, D_GLA_K), GLA_LOWRANK ** -0.5),
        "b_a": nrm(ks[4], (L, D_GLA_K), 0.1) + 1.0,
        "gla_norm_g": 1.0 + nrm(ks[5], (L, GLA_DV), 0.02),
        "b_glu": nrm(ks[6], (L, 2 * D_CONV), 0.02),
        "w_dw": nrm(ks[7], (L, CONV_WIDTH, D_CONV), CONV_WIDTH ** -0.5),
        "b_dw": nrm(ks[8], (L, D_CONV), 0.02),
        "conv_ln_g": 1.0 + nrm(ks[9], (L, D_CONV), 0.02),
        "conv_ln_b": nrm(ks[10], (L, D_CONV), 0.02),
        "w_out": nrm(ks[11], (L, D_MIX, D_MODEL), D_MIX ** -0.5),
        "ffn_norm_g": 1.0 + nrm(ks[12], (L, D_MODEL), 0.02),
        "w_router": nrm(ks[13], (L, D_MODEL, N_EXPERTS), D_MODEL ** -0.5),
        "b_router": nrm(ks[14], (L, N_EXPERTS), 0.01),
        "w1": nrm(ks[15], (L, N_EXPERTS, D_MODEL, 2 * D_EXPERT), D_MODEL ** -0.5),
        "b1": nrm(ks[16], (L, N_EXPERTS, 2 * D_EXPERT), 0.02),
        "w2": nrm(ks[17], (L, N_EXPERTS, D_EXPERT, D_MODEL), D_EXPERT ** -0.5),
        "b2": nrm(ks[18], (L, N_EXPERTS, D_MODEL), 0.02),
        "final_norm_g": 1.0 + nrm(ks[19], (D_MODEL,), 0.02),
    }


def reference(x, attn_norm_g, w_in, w_a2, b_a, gla_norm_g, b_glu, w_dw, b_dw, conv_ln_g,
              conv_ln_b, w_out, ffn_norm_g, w_router, b_router, w1, b1, w2, b2, final_norm_g):
    h = x
    for l in range(DEPTH):
        n = rmsnorm(h, attn_norm_g[l])
        h = h + hybrid_mixer(n, w_in[l], w_a2[l], b_a[l], gla_norm_g[l], b_glu[l], w_dw[l],
                             b_dw[l], conv_ln_g[l], conv_ln_b[l], w_out[l])
        n = rmsnorm(h, ffn_norm_g[l])
        h = h + moe_ffn(n, w_router[l], b_router[l], w1[l], b1[l], w2[l], b2[l])
    return rmsnorm(h, final_norm_g)
```

```python
import functools

import jax
import jax.numpy as jnp
from jax import lax
from jax.experimental import pallas as pl
from jax.experimental.pallas import tpu as pltpu

GLA_HEADS = 4
GLA_TAU = 16.0
GLA_CHUNK = 64
TOP_K = 4
SWIGLU_ALPHA = 1.702
SWIGLU_LIMIT = 7.0
EPS = 1e-5

LANES = 128
MOE_ROWS = 256
CONV_HALO = 32
MIB = 1024 * 1024

F32 = jnp.float32
BF16 = jnp.bfloat16


def _params(semantics, vmem_mib):
    return pltpu.CompilerParams(dimension_semantics=semantics, vmem_limit_bytes=vmem_mib * MIB)


def _rmsnorm_kernel(x_ref, g_ref, o_ref):
    x = x_ref[...]
    y = x * lax.rsqrt(jnp.mean(x * x, axis=-1, keepdims=True) + EPS)
    o_ref[...] = (y * g_ref[...]).astype(o_ref.dtype)


def _rmsnorm(x, g, out_dtype, rows=256):
    t, d = x.shape
    rows = min(rows, t)
    return pl.pallas_call(
        _rmsnorm_kernel,
        grid=(t // rows,),
        in_specs=[pl.BlockSpec((rows, d), lambda i: (i, 0)),
                  pl.BlockSpec((1, d), lambda i: (0, 0))],
        out_specs=pl.BlockSpec((rows, d), lambda i: (i, 0)),
        out_shape=jax.ShapeDtypeStruct((t, d), out_dtype),
        compiler_params=_params(("parallel",), 32),
        name="rmsnorm",
    )(x, g.reshape(1, d))


def _mm_kernel(a_ref, w_ref, o_ref):
    o_ref[...] = jnp.dot(a_ref[...], w_ref[...], preferred_element_type=F32).astype(o_ref.dtype)


def _matmul(a, w, tm, tn, name):
    m, k = a.shape
    n = w.shape[1]
    tm, tn = min(tm, m), min(tn, n)
    return pl.pallas_call(
        _mm_kernel,
        grid=(m // tm, n // tn),
        in_specs=[pl.BlockSpec((tm, k), lambda i, j: (i, 0)),
                  pl.BlockSpec((k, tn), lambda i, j: (0, j))],
        out_specs=pl.BlockSpec((tm, tn), lambda i, j: (i, j)),
        out_shape=jax.ShapeDtypeStruct((m, n), F32),
        compiler_params=_params(("parallel", "arbitrary"), 48),
        name=name,
    )(a, w)


def _outproj_kernel(a1_ref, a2_ref, w1_ref, w2_ref, x_ref, o_ref):
    acc = jnp.dot(a1_ref[...], w1_ref[...], preferred_element_type=F32)
    acc = acc + jnp.dot(a2_ref[...], w2_ref[...], preferred_element_type=F32)
    o_ref[...] = x_ref[...] + acc


def _outproj(a1, a2, w, x, tm=1024, tn=512):
    m, k1 = a1.shape
    k2 = a2.shape[1]
    n = w.shape[1]
    assert k1 == k2
    tm, tn = min(tm, m), min(tn, n)
    return pl.pallas_call(
        _outproj_kernel,
        grid=(m // tm, n // tn),
        in_specs=[pl.BlockSpec((tm, k1), lambda i, j: (i, 0)),
                  pl.BlockSpec((tm, k2), lambda i, j: (i, 0)),
                  pl.BlockSpec((k1, tn), lambda i, j: (0, j)),
                  pl.BlockSpec((k2, tn), lambda i, j: (1, j)),
                  pl.BlockSpec((tm, tn), lambda i, j: (i, j))],
        out_specs=pl.BlockSpec((tm, tn), lambda i, j: (i, j)),
        out_shape=jax.ShapeDtypeStruct((m, n), F32),
        compiler_params=_params(("parallel", "arbitrary"), 48),
        name="outproj",
    )(a1, a2, w, w, x)


def _gla_kernel(q_ref, k_ref, v_ref, r_ref, al_ref, wa_ref, ba_ref, g_ref, o_ref, st_ref, *, chunk, scale):
    @pl.when(pl.program_id(2) == 0)
    def _():
        st_ref[...] = jnp.zeros_like(st_ref)

    ts = q_ref.shape[1]
    rows = lax.broadcasted_iota(jnp.int32, (chunk, chunk), 0)
    cols = lax.broadcasted_iota(jnp.int32, (chunk, chunk), 1)
    causal = rows >= cols
    tril = jnp.where(causal, 1.0, 0.0).astype(BF16)
    nt = (((1,), (1,)), ((), ()))
    tn = (((0,), (0,)), ((), ()))
    for c in range(ts // chunk):
        sl = pl.ds(c * chunk, chunk)
        q, k, v = q_ref[0, sl, :], k_ref[0, sl, :], v_ref[0, sl, :]
        z = jnp.dot(al_ref[0, sl, :].astype(BF16), wa_ref[...], preferred_element_type=F32) + ba_ref[...]
        la = (jnp.minimum(z, 0.0) - jnp.log1p(jnp.exp(-jnp.abs(z)))) * (1.0 / GLA_TAU)
        la_hi = la.astype(BF16)
        la_lo = (la - la_hi.astype(F32)).astype(BF16)
        bcum = (jnp.dot(tril, la_hi, preferred_element_type=F32)
                + jnp.dot(tril, la_lo, preferred_element_type=F32))
        b_last = bcum[chunk - 1:chunk, :]
        q_e = (q * jnp.exp(bcum) * scale).astype(BF16)
        k_e = (k * jnp.exp(-bcum)).astype(BF16)
        k_d = (k * jnp.exp(b_last - bcum)).astype(BF16)
        decay = jnp.exp(b_last)
        vb = v.astype(BF16)
        s = lax.dot_general(q_e, k_e, nt, preferred_element_type=F32)
        s = jnp.where(causal, s, 0.0).astype(BF16)
        st = st_ref[...]
        o = jnp.dot(s, vb, preferred_element_type=F32)
        o = o + lax.dot_general(q_e, st.astype(BF16), nt, preferred_element_type=F32)
        st_ref[...] = st * decay + lax.dot_general(vb, k_d, tn, preferred_element_type=F32)
        o = o * lax.rsqrt(jnp.mean(o * o, axis=-1, keepdims=True) + EPS) * g_ref[...]
        r = r_ref[0, sl, :]
        o_ref[0, sl, :] = (o * (r * jax.nn.sigmoid(r))).astype(o_ref.dtype)


def _gla(proj, a_low, wa, ba, g, dk, dv, ts=256):
    b, s, _ = proj.shape
    h = GLA_HEADS
    ts = min(ts, s)
    kq, kv = (h * dk) // dk, (2 * h * dk) // dv
    return pl.pallas_call(
        functools.partial(_gla_kernel, chunk=GLA_CHUNK, scale=dk ** -0.5),
        grid=(b, h, s // ts),
        in_specs=[pl.BlockSpec((1, ts, dk), lambda bi, hi, si: (bi, si, hi)),
                  pl.BlockSpec((1, ts, dk), lambda bi, hi, si: (bi, si, kq + hi)),
                  pl.BlockSpec((1, ts, dv), lambda bi, hi, si: (bi, si, kv + hi)),
                  pl.BlockSpec((1, ts, dv), lambda bi, hi, si: (bi, si, kv + h + hi)),
                  pl.BlockSpec((1, ts, LANES), lambda bi, hi, si: (bi, si, 0)),
                  pl.BlockSpec((LANES, dk), lambda bi, hi, si: (0, hi)),
                  pl.BlockSpec((1, dk), lambda bi, hi, si: (0, hi)),
                  pl.BlockSpec((1, dv), lambda bi, hi, si: (0, 0))],
        out_specs=pl.BlockSpec((1, ts, dv), lambda bi, hi, si: (bi, si, hi)),
        out_shape=jax.ShapeDtypeStruct((b, s, h * dv), BF16),
        scratch_shapes=[pltpu.VMEM((dv, dk), F32)],
        compiler_params=_params(("parallel", "parallel", "arbitrary"), 32),
        name="gla",
    )(proj, proj, proj, proj, a_low, wa, ba, g)


def _conv_kernel(a_ref, b_ref, bga_ref, bgb_ref, w_ref, bdw_ref, lg_ref, lb_ref, o_ref, ubuf, cbuf, *,
                 width, cw, rc):
    ts, dc = a_ref.shape[1], a_ref.shape[2]
    si = pl.program_id(1)

    @pl.when(si == 0)
    def _():
        ubuf[0:CONV_HALO, :] = jnp.zeros((CONV_HALO, dc), F32)

    @pl.when(si > 0)
    def _():
        ubuf[0:CONV_HALO, :] = ubuf[ts:ts + CONV_HALO, :]

    ubuf[CONV_HALO:CONV_HALO + ts, :] = (a_ref[0] + bga_ref[...]) * jax.nn.sigmoid(b_ref[0] + bgb_ref[...])

    def col_body(cb, carry):
        cs = pl.ds(pl.multiple_of(cb * cw, cw), cw)
        for rb in range(ts // rc):
            acc = jnp.broadcast_to(bdw_ref[:, cs], (rc, cw))
            for j in range(width):
                off = CONV_HALO - (width - 1) + j + rb * rc
                acc = acc + w_ref[j:j + 1, cs] * ubuf[off:off + rc, cs]
            cbuf[rb * rc:(rb + 1) * rc, cs] = acc
        return carry

    lax.fori_loop(0, dc // cw, col_body, 0)
    c = cbuf[...]
    mu = jnp.mean(c, axis=-1, keepdims=True)
    cen = c - mu
    var = jnp.mean(cen * cen, axis=-1, keepdims=True)
    un = cen * lax.rsqrt(var + EPS) * lg_ref[...] + lb_ref[...]
    o_ref[0] = (un * jax.nn.sigmoid(un)).astype(o_ref.dtype)


def _conv_module(proj, col0, dc, b_glu, w_dw, b_dw, ln_g, ln_b, ts=256):
    b, s, _ = proj.shape
    ts = min(ts, s)
    width = w_dw.shape[0]
    assert width - 1 <= CONV_HALO <= ts
    cblk = col0 // dc
    row = lambda v: v.reshape(1, dc)
    vec = pl.BlockSpec((1, dc), lambda bi, si: (0, 0))
    return pl.pallas_call(
        functools.partial(_conv_kernel, width=width, cw=min(256, dc), rc=min(64, ts)),
        grid=(b, s // ts),
        in_specs=[pl.BlockSpec((1, ts, dc), lambda bi, si: (bi, si, cblk)),
                  pl.BlockSpec((1, ts, dc), lambda bi, si: (bi, si, cblk + 1)),
                  vec, vec,
                  pl.BlockSpec((width, dc), lambda bi, si: (0, 0)),
                  vec, vec, vec],
        out_specs=pl.BlockSpec((1, ts, dc), lambda bi, si: (bi, si, 0)),
        out_shape=jax.ShapeDtypeStruct((b, s, dc), BF16),
        scratch_shapes=[pltpu.VMEM((CONV_HALO + ts, dc), F32), pltpu.VMEM((ts, dc), F32)],
        compiler_params=_params(("parallel", "arbitrary"), 32),
        name="conv_module",
    )(proj, proj, row(b_glu[:dc]), row(b_glu[dc:]), w_dw, row(b_dw), row(ln_g), row(ln_b))


def _router_kernel(h_ref, g_ref, wr_ref, br_ref, n_ref, idx_ref, gate_ref, *, n_exp):
    h = h_ref[...]
    n = h * lax.rsqrt(jnp.mean(h * h, axis=-1, keepdims=True) + EPS) * g_ref[...]
    n_ref[...] = n.astype(n_ref.dtype)
    logits = jnp.dot(n.astype(BF16), wr_ref[...], preferred_element_type=F32) + br_ref[...]
    lane = lax.broadcasted_iota(jnp.int32, logits.shape, 1)
    lane_f = lane.astype(F32)
    neg = jnp.float32(-jnp.inf)
    l = jnp.where(lane < n_exp, logits, neg)
    vals, idxs = [], []
    for _ in range(TOP_K):
        m = jnp.max(l, axis=-1, keepdims=True)
        i = jnp.min(jnp.where(l == m, lane_f, float(LANES)), axis=-1, keepdims=True)
        vals.append(m)
        idxs.append(i)
        l = jnp.where(lane_f == i, neg, l)
    es = [jnp.exp(v - vals[0]) for v in vals]
    tot = es[0]
    for e in es[1:]:
        tot = tot + e
    idx_out = jnp.zeros(logits.shape, F32)
    gate_out = jnp.zeros(logits.shape, F32)
    for kk in range(TOP_K):
        idx_out = jnp.where(lane == kk, idxs[kk], idx_out)
        gate_out = jnp.where(lane == kk, es[kk] / tot, gate_out)
    idx_ref[...] = idx_out.astype(jnp.int32)
    gate_ref[...] = gate_out


def _router(h, g, wr, br, n_exp, rows=256):
    t, d = h.shape
    rows = min(rows, t)
    blk = lambda w: pl.BlockSpec((rows, w), lambda i: (i, 0))
    return pl.pallas_call(
        functools.partial(_router_kernel, n_exp=n_exp),
        grid=(t // rows,),
        in_specs=[blk(d),
                  pl.BlockSpec((1, d), lambda i: (0, 0)),
                  pl.BlockSpec((d, LANES), lambda i: (0, 0)),
                  pl.BlockSpec((1, LANES), lambda i: (0, 0))],
        out_specs=[blk(d), blk(LANES), blk(LANES)],
        out_shape=[jax.ShapeDtypeStruct((t, d), F32),
                   jax.ShapeDtypeStruct((t, LANES), jnp.int32),
                   jax.ShapeDtypeStruct((t, LANES), F32)],
        compiler_params=_params(("parallel",), 32),
        name="ffn_norm_router",
    )(h, g.reshape(1, d), wr, br)


def _row_copy(src_hbm, dst_hbm, sem, src_row, dst_row):
    return pltpu.make_async_copy(src_hbm.at[pl.ds(src_row, 1)], dst_hbm.at[pl.ds(dst_row, 1)], sem)


def _dispatch_kernel(tok_ref, src_hbm, dst_hbm, sem, *, rows):
    base = pl.program_id(0) * rows

    def start(r, carry):
        _row_copy(src_hbm, dst_hbm, sem, tok_ref[0, 0, r], base + r).start()
        return carry

    def wait(r, carry):
        _row_copy(src_hbm, dst_hbm, sem, 0, base + r).wait()
        return carry

    lax.fori_loop(0, rows, start, 0)
    lax.fori_loop(0, rows, wait, 0)


def _dispatch(n, slot_tok, rows=512):
    n_slots = slot_tok.shape[0]
    d = n.shape[1]
    rows = min(rows, n_slots)
    return pl.pallas_call(
        functools.partial(_dispatch_kernel, rows=rows),
        grid=(n_slots // rows,),
        in_specs=[pl.BlockSpec((1, 1, rows), lambda i: (i, 0, 0), memory_space=pltpu.SMEM),
                  pl.BlockSpec(memory_space=pl.ANY)],
        out_specs=pl.BlockSpec(memory_space=pl.ANY),
        out_shape=jax.ShapeDtypeStruct((n_slots, d), n.dtype),
        scratch_shapes=[pltpu.SemaphoreType.DMA(())],
        compiler_params=_params(("arbitrary",), 32),
        name="moe_dispatch",
    )(slot_tok.reshape(n_slots // rows, 1, rows), n)


def _moe_up_kernel(be_ref, x_ref, wg_ref, wl_ref, bg_ref, bl_ref, o_ref):
    x = x_ref[...].astype(BF16)
    xg = jnp.dot(x, wg_ref[0].astype(BF16), preferred_element_type=F32) + bg_ref[0]
    xl = jnp.dot(x, wl_ref[0].astype(BF16), preferred_element_type=F32) + bl_ref[0]
    xg = jnp.minimum(xg, SWIGLU_LIMIT)
    xl = jnp.clip(xl, -SWIGLU_LIMIT, SWIGLU_LIMIT)
    o_ref[...] = (xg * jax.nn.sigmoid(SWIGLU_ALPHA * xg) * (xl + 1.0)).astype(o_ref.dtype)


def _moe_up(xs, block_expert, w1, b1, tf=512):
    n_slots, d = xs.shape
    e, _, f2 = w1.shape
    f = f2 // 2
    tf = min(tf, f)
    nf = f // tf
    nb = n_slots // MOE_ROWS
    grid_spec = pltpu.PrefetchScalarGridSpec(
        num_scalar_prefetch=1,
        grid=(nf, nb),
        in_specs=[pl.BlockSpec((MOE_ROWS, d), lambda fi, bi, be: (bi, 0)),
                  pl.BlockSpec((1, d, tf), lambda fi, bi, be: (be[bi], 0, fi)),
                  pl.BlockSpec((1, d, tf), lambda fi, bi, be: (be[bi], 0, nf + fi)),
                  pl.BlockSpec((1, 1, tf), lambda fi, bi, be: (be[bi], 0, fi)),
                  pl.BlockSpec((1, 1, tf), lambda fi, bi, be: (be[bi], 0, nf + fi))],
        out_specs=pl.BlockSpec((MOE_ROWS, tf), lambda fi, bi, be: (bi, fi)),
    )
    return pl.pallas_call(
        _moe_up_kernel,
        grid_spec=grid_spec,
        out_shape=jax.ShapeDtypeStruct((n_slots, f), BF16),
        compiler_params=_params(("arbitrary", "arbitrary"), 56),
        name="moe_up",
    )(block_expert, xs, w1, w1, b1.reshape(e, 1, f2), b1.reshape(e, 1, f2))


def _moe_down_kernel(be_ref, h_ref, w_ref, b_ref, o_ref):
    y = jnp.dot(h_ref[...], w_ref[0].astype(BF16), preferred_element_type=F32) + b_ref[0]
    o_ref[...] = y.astype(o_ref.dtype)


def _moe_down(hid, block_expert, w2, b2, tn=2048):
    n_slots, f = hid.shape
    e, _, d = w2.shape
    tn = min(tn, d)
    nb = n_slots // MOE_ROWS
    grid_spec = pltpu.PrefetchScalarGridSpec(
        num_scalar_prefetch=1,
        grid=(d // tn, nb),
        in_specs=[pl.BlockSpec((MOE_ROWS, f), lambda ni, bi, be: (bi, 0)),
                  pl.BlockSpec((1, f, tn), lambda ni, bi, be: (be[bi], 0, ni)),
                  pl.BlockSpec((1, 1, tn), lambda ni, bi, be: (be[bi], 0, ni))],
        out_specs=pl.BlockSpec((MOE_ROWS, tn), lambda ni, bi, be: (bi, ni)),
    )
    return pl.pallas_call(
        _moe_down_kernel,
        grid_spec=grid_spec,
        out_shape=jax.ShapeDtypeStruct((n_slots, d), F32),
        compiler_params=_params(("arbitrary", "arbitrary"), 48),
        name="moe_down",
    )(block_expert, hid, w2, b2.reshape(e, 1, d))


def _combine_kernel(slot_ref, gate_ref, h_ref, g_ref, y_hbm, o_ref, ybuf, sem, *, rows):
    def copy(r, kk, slot):
        return pltpu.make_async_copy(y_hbm.at[pl.ds(slot, 1)], ybuf.at[kk, pl.ds(r, 1)], sem)

    def start(r, carry):
        for kk in range(TOP_K):
            copy(r, kk, slot_ref[0, 0, r * TOP_K + kk]).start()
        return carry

    def wait(r, carry):
        for kk in range(TOP_K):
            copy(r, kk, 0).wait()
        return carry

    lax.fori_loop(0, rows, start, 0)
    lax.fori_loop(0, rows, wait, 0)
    gates = gate_ref[...]
    acc = h_ref[...]
    for kk in range(TOP_K):
        acc = acc + ybuf[kk] * gates[:, kk:kk + 1]
    o_ref[...] = acc * lax.rsqrt(jnp.mean(acc * acc, axis=-1, keepdims=True) + EPS) * g_ref[...]


def _combine(y, slot, gates, h, g, rows=64):
    t, d = h.shape
    rows = min(rows, t)
    return pl.pallas_call(
        functools.partial(_combine_kernel, rows=rows),
        grid=(t // rows,),
        in_specs=[pl.BlockSpec((1, 1, rows * TOP_K), lambda i: (i, 0, 0), memory_space=pltpu.SMEM),
                  pl.BlockSpec((rows, LANES), lambda i: (i, 0)),
                  pl.BlockSpec((rows, d), lambda i: (i, 0)),
                  pl.BlockSpec((1, d), lambda i: (0, 0)),
                  pl.BlockSpec(memory_space=pl.ANY)],
        out_specs=pl.BlockSpec((rows, d), lambda i: (i, 0)),
        out_shape=jax.ShapeDtypeStruct((t, d), F32),
        scratch_shapes=[pltpu.VMEM((TOP_K, rows, d), F32), pltpu.SemaphoreType.DMA(())],
        compiler_params=_params(("arbitrary",), 32),
        name="moe_combine",
    )(slot.reshape(t // rows, 1, rows * TOP_K), gates, h, g.reshape(1, d), y)


def _routing(top_idx, n_exp):
    t = top_idx.shape[0]
    tk = t * TOP_K
    onehot = (top_idx[:, :, None] == jnp.arange(n_exp, dtype=jnp.int32)[None, None, :]).any(axis=1).astype(jnp.int32)
    incl = jnp.cumsum(onehot, axis=0)
    counts = incl[-1]
    rank = jnp.take_along_axis(incl - onehot, top_idx, axis=1)
    padded = (counts + MOE_ROWS - 1) // MOE_ROWS * MOE_ROWS
    pad_ends = jnp.cumsum(padded)
    pad_starts = pad_ends - padded
    slot = pad_starts[top_idx] + rank
    n_blocks = -(-tk // MOE_ROWS) + n_exp
    n_slots = n_blocks * MOE_ROWS
    tok = jnp.broadcast_to(jnp.arange(t, dtype=jnp.int32)[:, None], (t, TOP_K))
    slot_tok = jnp.zeros((n_slots,), jnp.int32).at[slot.reshape(tk)].set(tok.reshape(tk))
    block_rows = jnp.arange(n_blocks, dtype=jnp.int32) * MOE_ROWS
    block_expert = jnp.clip(jnp.searchsorted(pad_ends, block_rows, side='right'), 0, n_exp - 1).astype(jnp.int32)
    return slot.astype(jnp.int32), slot_tok, block_expert


def _layer(x2, b, s, attn_norm_g, w_in, w_a2, b_a, gla_norm_g, b_glu, w_dw, b_dw, conv_ln_g, conv_ln_b, w_out,
           ffn_norm_g, w_router, b_router, w1, b1, w2, b2):
    t, d = x2.shape
    dv = gla_norm_g.shape[0]
    dk = w_a2.shape[1] // GLA_HEADS
    lowrank = w_a2.shape[0]
    dc = w_dw.shape[1]
    n_exp = w_router.shape[1]
    c_qkvr = 2 * GLA_HEADS * (dk + dv)
    assert c_qkvr % dc == 0 and lowrank <= LANES and n_exp <= LANES

    n1 = _rmsnorm(x2, attn_norm_g, BF16)
    w_main = jnp.concatenate([w_in[:, :c_qkvr], w_in[:, c_qkvr + lowrank:]], axis=1).astype(BF16)
    w_low = jnp.pad(w_in[:, c_qkvr:c_qkvr + lowrank], ((0, 0), (0, LANES - lowrank))).astype(BF16)
    proj = _matmul(n1, w_main, 1024, 1024, "inproj").reshape(b, s, -1)
    a_low = _matmul(n1, w_low, 1024, LANES, "inproj_lowrank").reshape(b, s, LANES)

    wa = jnp.pad(w_a2, ((0, LANES - lowrank), (0, 0))).astype(BF16)
    gla_out = _gla(proj, a_low, wa, b_a.reshape(1, -1), gla_norm_g.reshape(1, dv), dk, dv)
    conv_out = _conv_module(proj, c_qkvr, dc, b_glu, w_dw, b_dw, conv_ln_g, conv_ln_b)
    h = _outproj(gla_out.reshape(t, -1), conv_out.reshape(t, dc), w_out.astype(BF16), x2)

    wr = jnp.pad(w_router, ((0, 0), (0, LANES - n_exp))).astype(BF16)
    br = jnp.pad(b_router, (0, LANES - n_exp)).reshape(1, LANES)
    n2, idx_pad, gate_pad = _router(h, ffn_norm_g, wr, br, n_exp)
    slot, slot_tok, block_expert = _routing(idx_pad[:, :TOP_K], n_exp)
    xs = _dispatch(n2, slot_tok)
    hid = _moe_up(xs, block_expert, w1, b1)
    y = _moe_down(hid, block_expert, w2, b2)
    return h, y, slot, gate_pad


def kernel(x, attn_norm_g, w_in, w_a2, b_a, gla_norm_g, b_glu, w_dw, b_dw, conv_ln_g, conv_ln_b, w_out,
           ffn_norm_g, w_router, b_router, w1, b1, w2, b2, final_norm_g):
    b, s, d = x.shape
    depth = w_in.shape[0]
    assert depth == 1
    h, y, slot, gate_pad = _layer(
        x.reshape(b * s, d), b, s, attn_norm_g[0], w_in[0], w_a2[0], b_a[0], gla_norm_g[0], b_glu[0], w_dw[0],
        b_dw[0], conv_ln_g[0], conv_ln_b[0], w_out[0], ffn_norm_g[0], w_router[0], b_router[0], w1[0], b1[0],
        w2[0], b2[0])
    out = _combine(y, slot, gate_pad, h, final_norm_g)
    return out.reshape(b, s, d)
```

```python
import functools

import jax
import jax.numpy as jnp
from jax import lax
from jax.experimental import pallas as pl
from jax.experimental.pallas import tpu as pltpu

GLA_HEADS = 4
GLA_TAU = 16.0
GLA_CHUNK = 64
TOP_K = 4
SWIGLU_ALPHA = 1.702
SWIGLU_LIMIT = 7.0
EPS = 1e-5

LANES = 128
MOE_ROWS = 256
CONV_HALO = 32
MIB = 1024 * 1024

F32 = jnp.float32
BF16 = jnp.bfloat16
U32 = jnp.uint32


def _params(semantics, vmem_mib):
    return pltpu.CompilerParams(dimension_semantics=semantics, vmem_limit_bytes=vmem_mib * MIB)


def _pack_halves(lo, hi):
    lo_u = lax.bitcast_convert_type(lo.astype(BF16).astype(F32), U32)
    hi_u = lax.bitcast_convert_type(hi.astype(BF16).astype(F32), U32)
    return (lo_u >> 16) | (hi_u & jnp.uint32(0xFFFF0000))


def _unpack_halves(p):
    lo = lax.bitcast_convert_type(p << 16, F32)
    hi = lax.bitcast_convert_type(p & jnp.uint32(0xFFFF0000), F32)
    return lo, hi


def _rmsnorm_kernel(x_ref, g_ref, o_ref):
    x = x_ref[...]
    y = x * lax.rsqrt(jnp.mean(x * x, axis=-1, keepdims=True) + EPS)
    o_ref[...] = (y * g_ref[...]).astype(o_ref.dtype)


def _rmsnorm(x, g, out_dtype, rows=256):
    t, d = x.shape
    rows = min(rows, t)
    return pl.pallas_call(
        _rmsnorm_kernel,
        grid=(t // rows,),
        in_specs=[pl.BlockSpec((rows, d), lambda i: (i, 0)),
                  pl.BlockSpec((1, d), lambda i: (0, 0))],
        out_specs=pl.BlockSpec((rows, d), lambda i: (i, 0)),
        out_shape=jax.ShapeDtypeStruct((t, d), out_dtype),
        compiler_params=_params(("parallel",), 32),
        name="rmsnorm",
    )(x, g.reshape(1, d))


def _mm_kernel(*refs, n_a, has_res):
    a_refs, w_ref = refs[:n_a], refs[n_a]
    res_ref = refs[n_a + 1] if has_res else None
    o_ref, wb = refs[-2], refs[-1]

    @pl.when(pl.program_id(1) == 0)
    def _():
        wb[...] = w_ref[...].astype(BF16)

    acc, k0 = None, 0
    for a_ref in a_refs:
        kk = a_ref.shape[1]
        part = jnp.dot(a_ref[...], wb[k0:k0 + kk, :], preferred_element_type=F32)
        acc = part if acc is None else acc + part
        k0 += kk
    if has_res:
        acc = acc + res_ref[...]
    o_ref[...] = acc.astype(o_ref.dtype)


def _matmul(a_list, w, n_out, name, res=None, tm=1024, tn=512):
    m = a_list[0].shape[0]
    k = w.shape[0]
    assert sum(a.shape[1] for a in a_list) == k
    tm, tn = min(tm, m), min(tn, n_out)
    in_specs = [pl.BlockSpec((tm, a.shape[1]), lambda j, i: (i, 0)) for a in a_list]
    in_specs.append(pl.BlockSpec((k, tn), lambda j, i: (0, j)))
    args = list(a_list) + [w]
    if res is not None:
        in_specs.append(pl.BlockSpec((tm, tn), lambda j, i: (i, j)))
        args.append(res)
    return pl.pallas_call(
        functools.partial(_mm_kernel, n_a=len(a_list), has_res=res is not None),
        grid=(n_out // tn, m // tm),
        in_specs=in_specs,
        out_specs=pl.BlockSpec((tm, tn), lambda j, i: (i, j)),
        out_shape=jax.ShapeDtypeStruct((m, n_out), F32),
        scratch_shapes=[pltpu.VMEM((k, tn), BF16)],
        compiler_params=_params(("parallel", "arbitrary"), 52),
        name=name,
    )(*args)


def _gla_kernel(q_ref, k_ref, v_ref, r_ref, al_ref, wa_ref, ba_ref, g_ref, o_ref, st_ref, *, chunk, scale):
    @pl.when(pl.program_id(2) == 0)
    def _():
        st_ref[...] = jnp.zeros_like(st_ref)

    ts = q_ref.shape[1]
    rows = lax.broadcasted_iota(jnp.int32, (chunk, chunk), 0)
    cols = lax.broadcasted_iota(jnp.int32, (chunk, chunk), 1)
    causal = rows >= cols
    tril = jnp.where(causal, 1.0, 0.0).astype(BF16)
    nt = (((1,), (1,)), ((), ()))
    tn = (((0,), (0,)), ((), ()))
    for c in range(ts // chunk):
        sl = pl.ds(c * chunk, chunk)
        q, k, v = q_ref[0, sl, :], k_ref[0, sl, :], v_ref[0, sl, :]
        z = jnp.dot(al_ref[0, sl, :].astype(BF16), wa_ref[...], preferred_element_type=F32) + ba_ref[...]
        la = (jnp.minimum(z, 0.0) - jnp.log1p(jnp.exp(-jnp.abs(z)))) * (1.0 / GLA_TAU)
        la_hi = la.astype(BF16)
        la_lo = (la - la_hi.astype(F32)).astype(BF16)
        bcum = (jnp.dot(tril, la_hi, preferred_element_type=F32)
                + jnp.dot(tril, la_lo, preferred_element_type=F32))
        b_last = bcum[chunk - 1:chunk, :]
        q_e = (q * jnp.exp(bcum) * scale).astype(BF16)
        k_e = (k * jnp.exp(-bcum)).astype(BF16)
        k_d = (k * jnp.exp(b_last - bcum)).astype(BF16)
        decay = jnp.exp(b_last)
        vb = v.astype(BF16)
        s = lax.dot_general(q_e, k_e, nt, preferred_element_type=F32)
        s = jnp.where(causal, s, 0.0).astype(BF16)
        st = st_ref[...]
        o = jnp.dot(s, vb, preferred_element_type=F32)
        o = o + lax.dot_general(q_e, st.astype(BF16), nt, preferred_element_type=F32)
        st_ref[...] = st * decay + lax.dot_general(vb, k_d, tn, preferred_element_type=F32)
        o = o * lax.rsqrt(jnp.mean(o * o, axis=-1, keepdims=True) + EPS) * g_ref[...]
        r = r_ref[0, sl, :]
        o_ref[0, sl, :] = (o * (r * jax.nn.sigmoid(r))).astype(o_ref.dtype)


def _gla(proj, a_low, wa, ba, g, dk, dv, ts=256):
    b, s, _ = proj.shape
    h = GLA_HEADS
    ts = min(ts, s)
    kq, kv = (h * dk) // dk, (2 * h * dk) // dv
    return pl.pallas_call(
        functools.partial(_gla_kernel, chunk=GLA_CHUNK, scale=dk ** -0.5),
        grid=(b, h, s // ts),
        in_specs=[pl.BlockSpec((1, ts, dk), lambda bi, hi, si: (bi, si, hi)),
                  pl.BlockSpec((1, ts, dk), lambda bi, hi, si: (bi, si, kq + hi)),
                  pl.BlockSpec((1, ts, dv), lambda bi, hi, si: (bi, si, kv + hi)),
                  pl.BlockSpec((1, ts, dv), lambda bi, hi, si: (bi, si, kv + h + hi)),
                  pl.BlockSpec((1, ts, LANES), lambda bi, hi, si: (bi, si, 0)),
                  pl.BlockSpec((LANES, dk), lambda bi, hi, si: (0, hi)),
                  pl.BlockSpec((1, dk), lambda bi, hi, si: (0, hi)),
                  pl.BlockSpec((1, dv), lambda bi, hi, si: (0, 0))],
        out_specs=pl.BlockSpec((1, ts, dv), lambda bi, hi, si: (bi, si, hi)),
        out_shape=jax.ShapeDtypeStruct((b, s, h * dv), BF16),
        scratch_shapes=[pltpu.VMEM((dv, dk), F32)],
        compiler_params=_params(("parallel", "parallel", "arbitrary"), 32),
        name="gla",
    )(proj, proj, proj, proj, a_low, wa, ba, g)


def _conv_kernel(a_ref, b_ref, bga_ref, bgb_ref, w_ref, bdw_ref, lg_ref, lb_ref, o_ref, ubuf, cbuf, *,
                 width, cw, rc):
    ts, dc = a_ref.shape[1], a_ref.shape[2]
    si = pl.program_id(1)

    @pl.when(si == 0)
    def _():
        ubuf[0:CONV_HALO, :] = jnp.zeros((CONV_HALO, dc), F32)

    @pl.when(si > 0)
    def _():
        ubuf[0:CONV_HALO, :] = ubuf[ts:ts + CONV_HALO, :]

    ubuf[CONV_HALO:CONV_HALO + ts, :] = (a_ref[0] + bga_ref[...]) * jax.nn.sigmoid(b_ref[0] + bgb_ref[...])

    def col_body(cb, carry):
        cs = pl.ds(pl.multiple_of(cb * cw, cw), cw)
        for rb in range(ts // rc):
            acc = jnp.broadcast_to(bdw_ref[:, cs], (rc, cw))
            for j in range(width):
                off = CONV_HALO - (width - 1) + j + rb * rc
                acc = acc + w_ref[j:j + 1, cs] * ubuf[off:off + rc, cs]
            cbuf[rb * rc:(rb + 1) * rc, cs] = acc
        return carry

    lax.fori_loop(0, dc // cw, col_body, 0)
    c = cbuf[...]
    mu = jnp.mean(c, axis=-1, keepdims=True)
    cen = c - mu
    var = jnp.mean(cen * cen, axis=-1, keepdims=True)
    un = cen * lax.rsqrt(var + EPS) * lg_ref[...] + lb_ref[...]
    o_ref[0] = (un * jax.nn.sigmoid(un)).astype(o_ref.dtype)


def _conv_module(glu, b_glu, w_dw, b_dw, ln_g, ln_b, ts=256):
    b, s, dc2 = glu.shape
    dc = dc2 // 2
    ts = min(ts, s)
    width = w_dw.shape[0]
    assert width - 1 <= CONV_HALO <= ts
    row = lambda v: v.reshape(1, dc)
    vec = pl.BlockSpec((1, dc), lambda bi, si: (0, 0))
    return pl.pallas_call(
        functools.partial(_conv_kernel, width=width, cw=min(256, dc), rc=min(64, ts)),
        grid=(b, s // ts),
        in_specs=[pl.BlockSpec((1, ts, dc), lambda bi, si: (bi, si, 0)),
                  pl.BlockSpec((1, ts, dc), lambda bi, si: (bi, si, 1)),
                  vec, vec,
                  pl.BlockSpec((width, dc), lambda bi, si: (0, 0)),
                  vec, vec, vec],
        out_specs=pl.BlockSpec((1, ts, dc), lambda bi, si: (bi, si, 0)),
        out_shape=jax.ShapeDtypeStruct((b, s, dc), BF16),
        scratch_shapes=[pltpu.VMEM((CONV_HALO + ts, dc), F32), pltpu.VMEM((ts, dc), F32)],
        compiler_params=_params(("parallel", "arbitrary"), 32),
        name="conv_module",
    )(glu, glu, row(b_glu[:dc]), row(b_glu[dc:]), w_dw, row(b_dw), row(ln_g), row(ln_b))


def _router_kernel(h_ref, g_ref, wr_ref, br_ref, n_ref, idx_ref, gate_ref, *, n_exp):
    h = h_ref[...]
    d2 = h.shape[1] // 2
    n = h * lax.rsqrt(jnp.mean(h * h, axis=-1, keepdims=True) + EPS) * g_ref[...]
    n_ref[...] = _pack_halves(n[:, :d2], n[:, d2:])
    logits = jnp.dot(n.astype(BF16), wr_ref[...], preferred_element_type=F32) + br_ref[...]
    lane = lax.broadcasted_iota(jnp.int32, logits.shape, 1)
    lane_f = lane.astype(F32)
    neg = jnp.float32(-jnp.inf)
    l = jnp.where(lane < n_exp, logits, neg)
    vals, idxs = [], []
    for _ in range(TOP_K):
        m = jnp.max(l, axis=-1, keepdims=True)
        i = jnp.min(jnp.where(l == m, lane_f, float(LANES)), axis=-1, keepdims=True)
        vals.append(m)
        idxs.append(i)
        l = jnp.where(lane_f == i, neg, l)
    es = [jnp.exp(v - vals[0]) for v in vals]
    tot = es[0]
    for e in es[1:]:
        tot = tot + e
    idx_out = jnp.zeros(logits.shape, F32)
    gate_out = jnp.zeros(logits.shape, F32)
    for kk in range(TOP_K):
        idx_out = jnp.where(lane == kk, idxs[kk], idx_out)
        gate_out = jnp.where(lane == kk, es[kk] / tot, gate_out)
    idx_ref[...] = idx_out.astype(jnp.int32)
    gate_ref[...] = gate_out


def _router(h, g, wr, br, n_exp, rows=256):
    t, d = h.shape
    rows = min(rows, t)
    blk = lambda w: pl.BlockSpec((rows, w), lambda i: (i, 0))
    return pl.pallas_call(
        functools.partial(_router_kernel, n_exp=n_exp),
        grid=(t // rows,),
        in_specs=[blk(d),
                  pl.BlockSpec((1, d), lambda i: (0, 0)),
                  pl.BlockSpec((d, LANES), lambda i: (0, 0)),
                  pl.BlockSpec((1, LANES), lambda i: (0, 0))],
        out_specs=[blk(d // 2), blk(LANES), blk(LANES)],
        out_shape=[jax.ShapeDtypeStruct((t, d // 2), U32),
                   jax.ShapeDtypeStruct((t, LANES), jnp.int32),
                   jax.ShapeDtypeStruct((t, LANES), F32)],
        compiler_params=_params(("parallel",), 32),
        name="ffn_norm_router",
    )(h, g.reshape(1, d), wr, br)


def _dispatch_kernel(zs_ref, nu_ref, slot_ref, n_ref, xs_hbm, zbuf, zsem, sem, *, rows, n_exp):
    def zero_copy(row):
        return pltpu.make_async_copy(zbuf, xs_hbm.at[pl.ds(pl.multiple_of(row, MOE_ROWS), MOE_ROWS)], zsem)

    @pl.when(pl.program_id(0) == 0)
    def _():
        zbuf[...] = jnp.zeros_like(zbuf)
        n_blocks = xs_hbm.shape[0] // MOE_ROWS
        for e in range(n_exp):
            @pl.when(zs_ref[e] >= 0)
            def _():
                zero_copy(zs_ref[e]).start()

        def tail_start(bi, carry):
            zero_copy(bi * MOE_ROWS).start()
            return carry

        def tail_wait(bi, carry):
            zero_copy(bi * MOE_ROWS).wait()
            return carry

        lax.fori_loop(nu_ref[0], n_blocks, tail_start, 0)
        for e in range(n_exp):
            @pl.when(zs_ref[e] >= 0)
            def _():
                zero_copy(zs_ref[e]).wait()
        lax.fori_loop(nu_ref[0], n_blocks, tail_wait, 0)

    def row_copy(r, slot):
        return pltpu.make_async_copy(n_ref.at[pl.ds(r, 1)], xs_hbm.at[pl.ds(slot, 1)], sem)

    def start(r, carry):
        for kk in range(TOP_K):
            row_copy(r, slot_ref[0, 0, r * TOP_K + kk]).start()
        return carry

    def wait(r, carry):
        for kk in range(TOP_K):
            row_copy(r, 0).wait()
        return carry

    lax.fori_loop(0, rows, start, 0)
    lax.fori_loop(0, rows, wait, 0)


def _dispatch(n_packed, slot, zero_start, n_used, n_slots, rows=256):
    t, d2 = n_packed.shape
    rows = min(rows, t)
    n_exp = zero_start.shape[0]
    grid_spec = pltpu.PrefetchScalarGridSpec(
        num_scalar_prefetch=2,
        grid=(t // rows,),
        in_specs=[pl.BlockSpec((1, 1, rows * TOP_K), lambda i, zs, nu: (i, 0, 0), memory_space=pltpu.SMEM),
                  pl.BlockSpec((rows, d2), lambda i, zs, nu: (i, 0))],
        out_specs=pl.BlockSpec(memory_space=pl.ANY),
        scratch_shapes=[pltpu.VMEM((MOE_ROWS, d2), U32), pltpu.SemaphoreType.DMA(()), pltpu.SemaphoreType.DMA(())],
    )
    return pl.pallas_call(
        functools.partial(_dispatch_kernel, rows=rows, n_exp=n_exp),
        grid_spec=grid_spec,
        out_shape=jax.ShapeDtypeStruct((n_slots, d2), U32),
        compiler_params=_params(("arbitrary",), 32),
        name="moe_dispatch",
    )(zero_start, n_used, slot.reshape(t // rows, 1, rows * TOP_K), n_packed)


def _block_state(be_ref, nu_ref, bi):
    used = bi < nu_ref[0]
    fresh = jnp.logical_or(bi == 0, be_ref[bi] != be_ref[jnp.maximum(bi - 1, 0)])
    return used, jnp.logical_and(used, fresh)


def _moe_up_kernel(be_ref, nu_ref, x_ref, wg_ref, wl_ref, bg_ref, bl_ref, o_ref, wgb, wlb):
    used, fresh = _block_state(be_ref, nu_ref, pl.program_id(1))

    @pl.when(fresh)
    def _():
        wgb[...] = wg_ref[0].astype(BF16)
        wlb[...] = wl_ref[0].astype(BF16)

    @pl.when(used)
    def _():
        d2 = x_ref.shape[1]
        lo, hi = _unpack_halves(x_ref[...])
        lo, hi = lo.astype(BF16), hi.astype(BF16)

        def proj(wb, b_ref):
            return (jnp.dot(lo, wb[0:d2, :], preferred_element_type=F32)
                    + jnp.dot(hi, wb[d2:2 * d2, :], preferred_element_type=F32) + b_ref[0])

        xg = jnp.minimum(proj(wgb, bg_ref), SWIGLU_LIMIT)
        xl = jnp.clip(proj(wlb, bl_ref), -SWIGLU_LIMIT, SWIGLU_LIMIT)
        o_ref[...] = (xg * jax.nn.sigmoid(SWIGLU_ALPHA * xg) * (xl + 1.0)).astype(o_ref.dtype)

    @pl.when(jnp.logical_not(used))
    def _():
        o_ref[...] = jnp.zeros_like(o_ref)


def _moe_up(xs, block_expert, n_used, w1, b1, tf=512):
    n_slots, d2 = xs.shape
    e, d, f2 = w1.shape
    f = f2 // 2
    tf = min(tf, f)
    nf = f // tf
    nb = n_slots // MOE_ROWS
    blk = lambda bi, nu: jnp.minimum(bi, nu[0] - 1)
    grid_spec = pltpu.PrefetchScalarGridSpec(
        num_scalar_prefetch=2,
        grid=(nf, nb),
        in_specs=[pl.BlockSpec((MOE_ROWS, d2), lambda fi, bi, be, nu: (blk(bi, nu), 0)),
                  pl.BlockSpec((1, d, tf), lambda fi, bi, be, nu: (be[blk(bi, nu)], 0, fi)),
                  pl.BlockSpec((1, d, tf), lambda fi, bi, be, nu: (be[blk(bi, nu)], 0, nf + fi)),
                  pl.BlockSpec((1, 1, tf), lambda fi, bi, be, nu: (be[blk(bi, nu)], 0, fi)),
                  pl.BlockSpec((1, 1, tf), lambda fi, bi, be, nu: (be[blk(bi, nu)], 0, nf + fi))],
        out_specs=pl.BlockSpec((MOE_ROWS, tf), lambda fi, bi, be, nu: (bi, fi)),
        scratch_shapes=[pltpu.VMEM((d, tf), BF16), pltpu.VMEM((d, tf), BF16)],
    )
    return pl.pallas_call(
        _moe_up_kernel,
        grid_spec=grid_spec,
        out_shape=jax.ShapeDtypeStruct((n_slots, f), BF16),
        compiler_params=_params(("arbitrary", "arbitrary"), 56),
        name="moe_up",
    )(block_expert, n_used, xs, w1, w1, b1.reshape(e, 1, f2), b1.reshape(e, 1, f2))


def _moe_down_kernel(be_ref, nu_ref, h_ref, w_ref, b_ref, o_ref, wb):
    used, fresh = _block_state(be_ref, nu_ref, pl.program_id(0))

    @pl.when(fresh)
    def _():
        wb[...] = w_ref[0].astype(BF16)

    @pl.when(used)
    def _():
        d2 = o_ref.shape[1]
        h = h_ref[...]
        y_lo = jnp.dot(h, wb[:, 0:d2], preferred_element_type=F32) + b_ref[0, :, 0:d2]
        y_hi = jnp.dot(h, wb[:, d2:2 * d2], preferred_element_type=F32) + b_ref[0, :, d2:2 * d2]
        o_ref[...] = _pack_halves(y_lo, y_hi)

    @pl.when(jnp.logical_not(used))
    def _():
        o_ref[...] = jnp.zeros_like(o_ref)


def _moe_down(hid, block_expert, n_used, w2, b2):
    n_slots, f = hid.shape
    e, _, d = w2.shape
    nb = n_slots // MOE_ROWS
    blk = lambda bi, nu: jnp.minimum(bi, nu[0] - 1)
    grid_spec = pltpu.PrefetchScalarGridSpec(
        num_scalar_prefetch=2,
        grid=(nb,),
        in_specs=[pl.BlockSpec((MOE_ROWS, f), lambda bi, be, nu: (blk(bi, nu), 0)),
                  pl.BlockSpec((1, f, d), lambda bi, be, nu: (be[blk(bi, nu)], 0, 0)),
                  pl.BlockSpec((1, 1, d), lambda bi, be, nu: (be[blk(bi, nu)], 0, 0))],
        out_specs=pl.BlockSpec((MOE_ROWS, d // 2), lambda bi, be, nu: (bi, 0)),
        scratch_shapes=[pltpu.VMEM((f, d), BF16)],
    )
    return pl.pallas_call(
        _moe_down_kernel,
        grid_spec=grid_spec,
        out_shape=jax.ShapeDtypeStruct((n_slots, d // 2), U32),
        compiler_params=_params(("arbitrary",), 56),
        name="moe_down",
    )(block_expert, n_used, hid, w2, b2.reshape(e, 1, d))


def _combine_kernel(slot_ref, nslot_ref, gate_ref, h_ref, g_ref, y_hbm, o_ref, ybuf, sem, *, rows):
    i = pl.program_id(0)
    cur = lax.rem(i, 2)

    def issue(s_ref, buf):
        def body(r, carry):
            for kk in range(TOP_K):
                pltpu.make_async_copy(y_hbm.at[pl.ds(s_ref[0, 0, r * TOP_K + kk], 1)],
                                      ybuf.at[buf, kk, pl.ds(r, 1)], sem.at[buf]).start()
            return carry
        lax.fori_loop(0, rows, body, 0)

    @pl.when(i == 0)
    def _():
        issue(slot_ref, 0)

    @pl.when(i + 1 < pl.num_programs(0))
    def _():
        issue(nslot_ref, 1 - cur)

    def wait(r, carry):
        for kk in range(TOP_K):
            pltpu.make_async_copy(y_hbm.at[pl.ds(0, 1)], ybuf.at[cur, kk, pl.ds(r, 1)], sem.at[cur]).wait()
        return carry

    lax.fori_loop(0, rows, wait, 0)

    d2 = ybuf.shape[3]
    gates = gate_ref[...]
    h = h_ref[...]
    acc_lo, acc_hi = h[:, :d2], h[:, d2:]
    for kk in range(TOP_K):
        lo, hi = _unpack_halves(ybuf[cur, kk])
        gk = gates[:, kk:kk + 1]
        acc_lo = acc_lo + gk * lo
        acc_hi = acc_hi + gk * hi
    ms = (jnp.sum(acc_lo * acc_lo, axis=-1, keepdims=True)
          + jnp.sum(acc_hi * acc_hi, axis=-1, keepdims=True)) * (1.0 / (2 * d2))
    inv = lax.rsqrt(ms + EPS)
    o_ref[:, :d2] = acc_lo * inv * g_ref[:, :d2]
    o_ref[:, d2:] = acc_hi * inv * g_ref[:, d2:]


def _combine(y_packed, slot, gates, h, g, rows=128):
    t, d = h.shape
    rows = min(rows, t)
    nsteps = t // rows
    slot3 = slot.reshape(nsteps, 1, rows * TOP_K)
    return pl.pallas_call(
        functools.partial(_combine_kernel, rows=rows),
        grid=(nsteps,),
        in_specs=[pl.BlockSpec((1, 1, rows * TOP_K), lambda i: (i, 0, 0), memory_space=pltpu.SMEM),
                  pl.BlockSpec((1, 1, rows * TOP_K), lambda i: (jnp.minimum(i + 1, nsteps - 1), 0, 0),
                               memory_space=pltpu.SMEM),
                  pl.BlockSpec((rows, LANES), lambda i: (i, 0)),
                  pl.BlockSpec((rows, d), lambda i: (i, 0)),
                  pl.BlockSpec((1, d), lambda i: (0, 0)),
                  pl.BlockSpec(memory_space=pl.ANY)],
        out_specs=pl.BlockSpec((rows, d), lambda i: (i, 0)),
        out_shape=jax.ShapeDtypeStruct((t, d), F32),
        scratch_shapes=[pltpu.VMEM((2, TOP_K, rows, d // 2), U32), pltpu.SemaphoreType.DMA((2,))],
        compiler_params=_params(("arbitrary",), 32),
        name="moe_combine",
    )(slot3, slot3, gates, h, g.reshape(1, d), y_packed)


def _routing(top_idx, n_exp):
    t = top_idx.shape[0]
    experts = jnp.arange(n_exp, dtype=jnp.int32)
    onehot = (top_idx[:, :, None] == experts[None, None, :]).any(axis=1).astype(jnp.int32)
    incl = jnp.cumsum(onehot, axis=0)
    counts = incl[-1]
    rank = jnp.take_along_axis(incl - onehot, top_idx, axis=1)
    padded = (counts + MOE_ROWS - 1) // MOE_ROWS * MOE_ROWS
    pad_ends = jnp.cumsum(padded)
    slot = (pad_ends - padded)[top_idx] + rank
    n_blocks = -(-(t * TOP_K) // MOE_ROWS) + n_exp
    block_rows = jnp.arange(n_blocks, dtype=jnp.int32) * MOE_ROWS
    block_expert = jnp.minimum(jnp.sum(pad_ends[None, :] <= block_rows[:, None], axis=1), n_exp - 1)
    n_used = (pad_ends[-1] // MOE_ROWS).reshape(1)
    zero_start = jnp.where(padded > 0, pad_ends - MOE_ROWS, -1)
    return (slot.astype(jnp.int32), block_expert.astype(jnp.int32), n_used.astype(jnp.int32),
            zero_start.astype(jnp.int32), n_blocks * MOE_ROWS)


def kernel(x, attn_norm_g, w_in, w_a2, b_a, gla_norm_g, b_glu, w_dw, b_dw, conv_ln_g, conv_ln_b, w_out,
           ffn_norm_g, w_router, b_router, w1, b1, w2, b2, final_norm_g):
    b, s, d = x.shape
    assert w_in.shape[0] == 1, "single-layer stack"
    t = b * s
    dv = gla_norm_g.shape[1]
    dk = w_a2.shape[2] // GLA_HEADS
    lowrank = w_a2.shape[1]
    dc = w_dw.shape[2]
    n_exp = w_router.shape[2]
    c_qkvr = 2 * GLA_HEADS * (dk + dv)
    assert lowrank <= LANES and n_exp <= LANES
    x2 = x.reshape(t, d)

    n1 = _rmsnorm(x2, attn_norm_g[0], BF16)
    w_glu = w_in[0, :, c_qkvr + lowrank:]
    w_low = jnp.pad(w_in[0, :, c_qkvr:c_qkvr + lowrank], ((0, 0), (0, LANES - lowrank)))
    proj = _matmul([n1], w_in[0], c_qkvr, "inproj").reshape(b, s, c_qkvr)
    glu = _matmul([n1], w_glu, 2 * dc, "inproj_glu").reshape(b, s, 2 * dc)
    a_low = _matmul([n1], w_low, LANES, "inproj_lowrank").reshape(b, s, LANES)

    wa = jnp.pad(w_a2[0], ((0, LANES - lowrank), (0, 0))).astype(BF16)
    gla_out = _gla(proj, a_low, wa, b_a[0].reshape(1, -1), gla_norm_g[0].reshape(1, dv), dk, dv)
    conv_out = _conv_module(glu, b_glu[0], w_dw[0], b_dw[0], conv_ln_g[0], conv_ln_b[0])
    h = _matmul([gla_out.reshape(t, -1), conv_out.reshape(t, dc)], w_out[0], d, "outproj", res=x2)

    wr = jnp.pad(w_router[0], ((0, 0), (0, LANES - n_exp))).astype(BF16)
    br = jnp.pad(b_router[0], (0, LANES - n_exp)).reshape(1, LANES)
    n2, idx_pad, gate_pad = _router(h, ffn_norm_g[0], wr, br, n_exp)
    slot, block_expert, n_used, zero_start, n_slots = _routing(idx_pad[:, :TOP_K], n_exp)
    xs = _dispatch(n2, slot, zero_start, n_used, n_slots)
    hid = _moe_up(xs, block_expert, n_used, w1[0], b1[0])
    y = _moe_down(hid, block_expert, n_used, w2[0], b2[0])
    out = _combine(y, slot, gate_pad, h, final_norm_g)
    return out.reshape(b, s, d)
```

```python
import functools

import jax
import jax.numpy as jnp
from jax import lax
from jax.experimental import pallas as pl
from jax.experimental.pallas import tpu as pltpu

GLA_HEADS = 4
GLA_TAU = 16.0
GLA_CHUNK = 64
TOP_K = 4
SWIGLU_ALPHA = 1.702
SWIGLU_LIMIT = 7.0
EPS = 1e-5

LANES = 128
SUBLANES = 8
MOE_ROWS = 256
CONV_HALO = 32
MIB = 1024 * 1024

F32 = jnp.float32
BF16 = jnp.bfloat16
U32 = jnp.uint32


def _params(semantics, vmem_mib):
    return pltpu.CompilerParams(dimension_semantics=semantics, vmem_limit_bytes=vmem_mib * MIB)


def _pack_halves(lo, hi):
    lo_u = lax.bitcast_convert_type(lo.astype(BF16).astype(F32), U32)
    hi_u = lax.bitcast_convert_type(hi.astype(BF16).astype(F32), U32)
    return (lo_u >> 16) | (hi_u & jnp.uint32(0xFFFF0000))


def _unpack_halves(p):
    lo = lax.bitcast_convert_type(p << 16, F32)
    hi = lax.bitcast_convert_type(p & jnp.uint32(0xFFFF0000), F32)
    return lo, hi


def _rmsnorm_kernel(x_ref, g_ref, o_ref):
    x = x_ref[...]
    y = x * lax.rsqrt(jnp.mean(x * x, axis=-1, keepdims=True) + EPS)
    o_ref[...] = (y * g_ref[...]).astype(o_ref.dtype)


def _rmsnorm(x, g, out_dtype, rows=256):
    t, d = x.shape
    rows = min(rows, t)
    return pl.pallas_call(
        _rmsnorm_kernel,
        grid=(t // rows,),
        in_specs=[pl.BlockSpec((rows, d), lambda i: (i, 0)),
                  pl.BlockSpec((1, d), lambda i: (0, 0))],
        out_specs=pl.BlockSpec((rows, d), lambda i: (i, 0)),
        out_shape=jax.ShapeDtypeStruct((t, d), out_dtype),
        compiler_params=_params(("parallel",), 32),
        name="rmsnorm",
    )(x, g.reshape(1, d))


def _wprep_kernel(a_ref, b_ref, o_ref, low_ref, *, nb_main, lowrank):
    i = pl.program_id(0)
    rows = o_ref.shape[0]

    @pl.when(i < nb_main)
    def _():
        o_ref[...] = a_ref[...].astype(BF16)

    @pl.when(i >= nb_main)
    def _():
        o_ref[0:rows - lowrank, :] = a_ref[lowrank:rows, :].astype(BF16)
        o_ref[rows - lowrank:rows, :] = b_ref[0:lowrank, :].astype(BF16)

    @pl.when(i == nb_main)
    def _():
        low_ref[...] = jnp.zeros_like(low_ref)
        low_ref[0:lowrank, :] = a_ref[0:lowrank, :].astype(BF16)


def _wprep(wt, c_main, lowrank):
    n_in, k = wt.shape
    rows = LANES
    n_out = n_in - lowrank
    assert c_main % rows == 0 and n_out % rows == 0 and lowrank % 16 == 0
    last = pl.cdiv(n_in, rows) - 1
    return pl.pallas_call(
        functools.partial(_wprep_kernel, nb_main=c_main // rows, lowrank=lowrank),
        grid=(n_out // rows,),
        in_specs=[pl.BlockSpec((rows, k), lambda i: (i, 0)),
                  pl.BlockSpec((rows, k), lambda i: (jnp.minimum(i + 1, last), 0))],
        out_specs=[pl.BlockSpec((rows, k), lambda i: (i, 0)),
                   pl.BlockSpec((rows, k), lambda i: (0, 0))],
        out_shape=[jax.ShapeDtypeStruct((n_out, k), BF16), jax.ShapeDtypeStruct((rows, k), BF16)],
        compiler_params=_params(("arbitrary",), 32),
        name="inproj_weight_prep",
    )(wt, wt)


def _mm_nt_kernel(a_ref, wt_ref, o_ref):
    o_ref[...] = lax.dot_general(a_ref[...], wt_ref[...], (((1,), (1,)), ((), ())), preferred_element_type=F32)


def _matmul_nt(a, wt, name, tm=1024, tn=1024):
    m, k = a.shape
    n = wt.shape[0]
    tm, tn = min(tm, m), min(tn, n)
    return pl.pallas_call(
        _mm_nt_kernel,
        grid=(m // tm, n // tn),
        in_specs=[pl.BlockSpec((tm, k), lambda i, j: (i, 0)),
                  pl.BlockSpec((tn, k), lambda i, j: (j, 0))],
        out_specs=pl.BlockSpec((tm, tn), lambda i, j: (i, j)),
        out_shape=jax.ShapeDtypeStruct((m, n), F32),
        compiler_params=_params(("parallel", "arbitrary"), 48),
        name=name,
    )(a, wt)


def _mm_kernel(*refs, n_a, has_res):
    a_refs, w_ref = refs[:n_a], refs[n_a]
    res_ref = refs[n_a + 1] if has_res else None
    o_ref, wb = refs[-2], refs[-1]

    @pl.when(pl.program_id(1) == 0)
    def _():
        wb[...] = w_ref[...].astype(BF16)

    acc, k0 = None, 0
    for a_ref in a_refs:
        kk = a_ref.shape[1]
        part = jnp.dot(a_ref[...], wb[k0:k0 + kk, :], preferred_element_type=F32)
        acc = part if acc is None else acc + part
        k0 += kk
    if has_res:
        acc = acc + res_ref[...]
    o_ref[...] = acc.astype(o_ref.dtype)


def _matmul(a_list, w, n_out, name, res=None, tm=1024, tn=512):
    m = a_list[0].shape[0]
    k = w.shape[0]
    assert sum(a.shape[1] for a in a_list) == k
    tm, tn = min(tm, m), min(tn, n_out)
    in_specs = [pl.BlockSpec((tm, a.shape[1]), lambda j, i: (i, 0)) for a in a_list]
    in_specs.append(pl.BlockSpec((k, tn), lambda j, i: (0, j)))
    args = list(a_list) + [w]
    if res is not None:
        in_specs.append(pl.BlockSpec((tm, tn), lambda j, i: (i, j)))
        args.append(res)
    return pl.pallas_call(
        functools.partial(_mm_kernel, n_a=len(a_list), has_res=res is not None),
        grid=(n_out // tn, m // tm),
        in_specs=in_specs,
        out_specs=pl.BlockSpec((tm, tn), lambda j, i: (i, j)),
        out_shape=jax.ShapeDtypeStruct((m, n_out), F32),
        scratch_shapes=[pltpu.VMEM((k, tn), BF16)],
        compiler_params=_params(("parallel", "arbitrary"), 52),
        name=name,
    )(*args)


def _gla_kernel(q_ref, k_ref, v_ref, r_ref, al_ref, wa_ref, ba_ref, g_ref, o_ref, st_ref, *, chunk, scale):
    @pl.when(pl.program_id(2) == 0)
    def _():
        st_ref[...] = jnp.zeros_like(st_ref)

    ts = q_ref.shape[1]
    rows = lax.broadcasted_iota(jnp.int32, (ts, ts), 0)
    cols = lax.broadcasted_iota(jnp.int32, (ts, ts), 1)
    assert chunk & (chunk - 1) == 0
    same_chunk = (rows & -chunk) == (cols & -chunk)
    causal = jnp.logical_and(same_chunk, rows >= cols)
    tril = jnp.where(causal, 1.0, 0.0).astype(BF16)
    ones = jnp.where(same_chunk, 1.0, 0.0).astype(BF16)
    nt = (((1,), (1,)), ((), ()))
    tn = (((0,), (0,)), ((), ()))

    q, k, v = q_ref[0], k_ref[0], v_ref[0]
    z = jnp.dot(al_ref[0].astype(BF16), wa_ref[...], preferred_element_type=F32) + ba_ref[...]
    la = (jnp.minimum(z, 0.0) - jnp.log1p(jnp.exp(-jnp.abs(z)))) * (1.0 / GLA_TAU)
    la_hi = la.astype(BF16)
    la_lo = (la - la_hi.astype(F32)).astype(BF16)
    bcum = jnp.dot(tril, la_hi, preferred_element_type=F32) + jnp.dot(tril, la_lo, preferred_element_type=F32)
    btot = jnp.dot(ones, la_hi, preferred_element_type=F32) + jnp.dot(ones, la_lo, preferred_element_type=F32)
    q_e = (q * jnp.exp(bcum) * scale).astype(BF16)
    k_e = (k * jnp.exp(-bcum)).astype(BF16)
    k_d = (k * jnp.exp(btot - bcum)).astype(BF16)
    decay = jnp.exp(btot)
    vb = v.astype(BF16)
    s = lax.dot_general(q_e, k_e, nt, preferred_element_type=F32)
    s = jnp.where(causal, s, 0.0).astype(BF16)
    o_intra = jnp.dot(s, vb, preferred_element_type=F32)

    st = st_ref[...]
    outs = []
    for c in range(ts // chunk):
        lo, hi = c * chunk, (c + 1) * chunk
        outs.append(o_intra[lo:hi] + lax.dot_general(q_e[lo:hi], st.astype(BF16), nt, preferred_element_type=F32))
        st = st * decay[lo:lo + 1] + lax.dot_general(vb[lo:hi], k_d[lo:hi], tn, preferred_element_type=F32)
    st_ref[...] = st
    o = jnp.concatenate(outs, axis=0)
    o = o * lax.rsqrt(jnp.mean(o * o, axis=-1, keepdims=True) + EPS) * g_ref[...]
    r = r_ref[0]
    o_ref[0] = (o * (r * jax.nn.sigmoid(r))).astype(o_ref.dtype)


def _gla(proj, a_low, wa, ba, g, dk, dv, ts=256):
    b, s, _ = proj.shape
    h = GLA_HEADS
    ts = min(ts, s)
    kq, kv = (h * dk) // dk, (2 * h * dk) // dv
    return pl.pallas_call(
        functools.partial(_gla_kernel, chunk=GLA_CHUNK, scale=dk ** -0.5),
        grid=(b, h, s // ts),
        in_specs=[pl.BlockSpec((1, ts, dk), lambda bi, hi, si: (bi, si, hi)),
                  pl.BlockSpec((1, ts, dk), lambda bi, hi, si: (bi, si, kq + hi)),
                  pl.BlockSpec((1, ts, dv), lambda bi, hi, si: (bi, si, kv + hi)),
                  pl.BlockSpec((1, ts, dv), lambda bi, hi, si: (bi, si, kv + h + hi)),
                  pl.BlockSpec((1, ts, LANES), lambda bi, hi, si: (bi, si, 0)),
                  pl.BlockSpec((LANES, dk), lambda bi, hi, si: (0, hi)),
                  pl.BlockSpec((1, dk), lambda bi, hi, si: (0, hi)),
                  pl.BlockSpec((1, dv), lambda bi, hi, si: (0, 0))],
        out_specs=pl.BlockSpec((1, ts, dv), lambda bi, hi, si: (bi, si, hi)),
        out_shape=jax.ShapeDtypeStruct((b, s, h * dv), BF16),
        scratch_shapes=[pltpu.VMEM((dv, dk), F32)],
        compiler_params=_params(("parallel", "parallel", "arbitrary"), 32),
        name="gla",
    )(proj, proj, proj, proj, a_low, wa, ba, g)


def _conv_kernel(a_ref, b_ref, bga_ref, bgb_ref, w_ref, bdw_ref, lg_ref, lb_ref, o_ref, ubuf, sbuf, cbuf, *,
                 width, cw, rc):
    ts, dc = a_ref.shape[1], a_ref.shape[2]
    si = pl.program_id(1)

    @pl.when(si == 0)
    def _():
        ubuf[0:CONV_HALO, :] = jnp.zeros((CONV_HALO, dc), F32)

    @pl.when(si > 0)
    def _():
        ubuf[0:CONV_HALO, :] = ubuf[ts:ts + CONV_HALO, :]

    ubuf[CONV_HALO:CONV_HALO + ts, :] = (a_ref[0] + bga_ref[...]) * jax.nn.sigmoid(b_ref[0] + bgb_ref[...])

    span = ts + CONV_HALO - SUBLANES

    def col_body(cb, carry):
        cs = pl.ds(pl.multiple_of(cb * cw, cw), cw)
        for b in range(1, SUBLANES):
            sbuf[b, 0:span, :] = ubuf[b:b + span, cs]
        for rb in range(ts // rc):
            acc = jnp.broadcast_to(bdw_ref[:, cs], (rc, cw))
            for j in range(width):
                off = CONV_HALO - (width - 1) + j
                b = off % SUBLANES
                r0 = off - b + rb * rc
                src = ubuf[r0:r0 + rc, cs] if b == 0 else sbuf[b, r0:r0 + rc, :]
                acc = acc + w_ref[j:j + 1, cs] * src
            cbuf[rb * rc:(rb + 1) * rc, cs] = acc
        return carry

    lax.fori_loop(0, dc // cw, col_body, 0)
    c = cbuf[...]
    mu = jnp.mean(c, axis=-1, keepdims=True)
    cen = c - mu
    var = jnp.mean(cen * cen, axis=-1, keepdims=True)
    un = cen * lax.rsqrt(var + EPS) * lg_ref[...] + lb_ref[...]
    o_ref[0] = (un * jax.nn.sigmoid(un)).astype(o_ref.dtype)


def _conv_module(proj, col0, dc, b_glu, w_dw, b_dw, ln_g, ln_b, ts=256):
    b, s, _ = proj.shape
    ts = min(ts, s)
    width = w_dw.shape[0]
    cw = min(256, dc)
    assert width - 1 <= CONV_HALO <= ts and col0 % dc == 0
    cblk = col0 // dc
    row = lambda v: v.reshape(1, dc)
    vec = pl.BlockSpec((1, dc), lambda bi, si: (0, 0))
    return pl.pallas_call(
        functools.partial(_conv_kernel, width=width, cw=cw, rc=min(64, ts)),
        grid=(b, s // ts),
        in_specs=[pl.BlockSpec((1, ts, dc), lambda bi, si: (bi, si, cblk)),
                  pl.BlockSpec((1, ts, dc), lambda bi, si: (bi, si, cblk + 1)),
                  vec, vec,
                  pl.BlockSpec((width, dc), lambda bi, si: (0, 0)),
                  vec, vec, vec],
        out_specs=pl.BlockSpec((1, ts, dc), lambda bi, si: (bi, si, 0)),
        out_shape=jax.ShapeDtypeStruct((b, s, dc), BF16),
        scratch_shapes=[pltpu.VMEM((CONV_HALO + ts, dc), F32),
                        pltpu.VMEM((SUBLANES, CONV_HALO + ts, cw), F32),
                        pltpu.VMEM((ts, dc), F32)],
        compiler_params=_params(("parallel", "arbitrary"), 32),
        name="conv_module",
    )(proj, proj, row(b_glu[:dc]), row(b_glu[dc:]), w_dw, row(b_dw), row(ln_g), row(ln_b))


def _router_kernel(h_ref, g_ref, wr_ref, br_ref, n_ref, idx_ref, gate_ref, *, n_exp):
    h = h_ref[...]
    d2 = h.shape[1] // 2
    n = h * lax.rsqrt(jnp.mean(h * h, axis=-1, keepdims=True) + EPS) * g_ref[...]
    n_ref[...] = _pack_halves(n[:, :d2], n[:, d2:])
    logits = jnp.dot(n.astype(BF16), wr_ref[...], preferred_element_type=F32) + br_ref[...]
    lane = lax.broadcasted_iota(jnp.int32, logits.shape, 1)
    lane_f = lane.astype(F32)
    neg = jnp.float32(-jnp.inf)
    l = jnp.where(lane < n_exp, logits, neg)
    vals, idxs = [], []
    for _ in range(TOP_K):
        m = jnp.max(l, axis=-1, keepdims=True)
        i = jnp.min(jnp.where(l == m, lane_f, float(LANES)), axis=-1, keepdims=True)
        vals.append(m)
        idxs.append(i)
        l = jnp.where(lane_f == i, neg, l)
    es = [jnp.exp(v - vals[0]) for v in vals]
    tot = es[0]
    for e in es[1:]:
        tot = tot + e
    idx_out = jnp.zeros(logits.shape, F32)
    gate_out = jnp.zeros(logits.shape, F32)
    for kk in range(TOP_K):
        idx_out = jnp.where(lane == kk, idxs[kk], idx_out)
        gate_out = jnp.where(lane == kk, es[kk] / tot, gate_out)
    idx_ref[...] = idx_out.astype(jnp.int32)
    gate_ref[...] = gate_out


def _router(h, g, wr, br, n_exp, rows=256):
    t, d = h.shape
    rows = min(rows, t)
    blk = lambda w: pl.BlockSpec((rows, w), lambda i: (i, 0))
    return pl.pallas_call(
        functools.partial(_router_kernel, n_exp=n_exp),
        grid=(t // rows,),
        in_specs=[blk(d),
                  pl.BlockSpec((1, d), lambda i: (0, 0)),
                  pl.BlockSpec((d, LANES), lambda i: (0, 0)),
                  pl.BlockSpec((1, LANES), lambda i: (0, 0))],
        out_specs=[blk(d // 2), blk(LANES), blk(LANES)],
        out_shape=[jax.ShapeDtypeStruct((t, d // 2), U32),
                   jax.ShapeDtypeStruct((t, LANES), jnp.int32),
                   jax.ShapeDtypeStruct((t, LANES), F32)],
        compiler_params=_params(("parallel",), 32),
        name="ffn_norm_router",
    )(h, g.reshape(1, d), wr, br)


def _dispatch_kernel(zs_ref, nu_ref, slot_ref, n_ref, xs_hbm, zbuf, zsem, sem, *, rows, n_exp):
    def zero_copy(row):
        return pltpu.make_async_copy(zbuf, xs_hbm.at[pl.ds(pl.multiple_of(row, MOE_ROWS), MOE_ROWS)], zsem)

    @pl.when(pl.program_id(0) == 0)
    def _():
        zbuf[...] = jnp.zeros_like(zbuf)
        n_blocks = xs_hbm.shape[0] // MOE_ROWS
        for e in range(n_exp):
            @pl.when(zs_ref[e] >= 0)
            def _():
                zero_copy(zs_ref[e]).start()

        def tail_start(bi, carry):
            zero_copy(bi * MOE_ROWS).start()
            return carry

        def tail_wait(bi, carry):
            zero_copy(bi * MOE_ROWS).wait()
            return carry

        lax.fori_loop(nu_ref[0], n_blocks, tail_start, 0)
        for e in range(n_exp):
            @pl.when(zs_ref[e] >= 0)
            def _():
                zero_copy(zs_ref[e]).wait()
        lax.fori_loop(nu_ref[0], n_blocks, tail_wait, 0)

    def row_copy(r, slot):
        return pltpu.make_async_copy(n_ref.at[pl.ds(r, 1)], xs_hbm.at[pl.ds(slot, 1)], sem)

    def start(r, carry):
        for kk in range(TOP_K):
            row_copy(r, slot_ref[0, 0, r * TOP_K + kk]).start()
        return carry

    def wait(r, carry):
        for kk in range(TOP_K):
            row_copy(r, 0).wait()
        return carry

    lax.fori_loop(0, rows, start, 0)
    lax.fori_loop(0, rows, wait, 0)


def _dispatch(n_packed, slot, zero_start, n_used, n_slots, rows=256):
    t, d2 = n_packed.shape
    rows = min(rows, t)
    n_exp = zero_start.shape[0]
    grid_spec = pltpu.PrefetchScalarGridSpec(
        num_scalar_prefetch=2,
        grid=(t // rows,),
        in_specs=[pl.BlockSpec((1, 1, rows * TOP_K), lambda i, zs, nu: (i, 0, 0), memory_space=pltpu.SMEM),
                  pl.BlockSpec((rows, d2), lambda i, zs, nu: (i, 0))],
        out_specs=pl.BlockSpec(memory_space=pl.ANY),
        scratch_shapes=[pltpu.VMEM((MOE_ROWS, d2), U32), pltpu.SemaphoreType.DMA(()), pltpu.SemaphoreType.DMA(())],
    )
    return pl.pallas_call(
        functools.partial(_dispatch_kernel, rows=rows, n_exp=n_exp),
        grid_spec=grid_spec,
        out_shape=jax.ShapeDtypeStruct((n_slots, d2), U32),
        compiler_params=_params(("arbitrary",), 32),
        name="moe_dispatch",
    )(zero_start, n_used, slot.reshape(t // rows, 1, rows * TOP_K), n_packed)


def _block_state(be_ref, nu_ref, bi):
    used = bi < nu_ref[0]
    fresh = jnp.logical_or(bi == 0, be_ref[bi] != be_ref[jnp.maximum(bi - 1, 0)])
    return used, jnp.logical_and(used, fresh)


def _run_weights(copies, first, fresh, cast, then):
    first_cond, first_args = first

    @pl.when(first_cond)
    def _():
        for c in copies(*first_args):
            c.start()

    @pl.when(fresh)
    def _():
        for c in copies(*first_args):
            c.wait()
        cast()
        for cond, args in then:
            @pl.when(cond)
            def _():
                for c in copies(*args):
                    c.start()


def _moe_up_kernel(be_ref, nu_ref, nx_ref, x_ref, bg_ref, bl_ref, w_hbm, o_ref, sg, sl, wgb, wlb, sem, *, nf):
    fi, bi = pl.program_id(0), pl.program_id(1)
    used, fresh = _block_state(be_ref, nu_ref, bi)
    tf = sg.shape[1]
    f_all = w_hbm.shape[2] // 2

    def copies(e, f):
        c0 = pl.multiple_of(f * tf, tf)
        return (pltpu.make_async_copy(w_hbm.at[e, :, pl.ds(c0, tf)], sg, sem.at[0]),
                pltpu.make_async_copy(w_hbm.at[e, :, pl.ds(f_all + c0, tf)], sl, sem.at[1]))

    def cast():
        wgb[...] = sg[...].astype(BF16)
        wlb[...] = sl[...].astype(BF16)

    nx = nx_ref[bi]
    _run_weights(copies, (jnp.logical_and(fi == 0, bi == 0), (be_ref[0], 0)), fresh, cast,
                 [(nx >= 0, (nx, fi)),
                  (jnp.logical_and(nx < 0, fi + 1 < nf), (be_ref[0], fi + 1))])

    @pl.when(used)
    def _():
        d2 = x_ref.shape[1]
        lo, hi = _unpack_halves(x_ref[...])
        lo, hi = lo.astype(BF16), hi.astype(BF16)

        def proj(wb, b_ref):
            return (jnp.dot(lo, wb[0:d2, :], preferred_element_type=F32)
                    + jnp.dot(hi, wb[d2:2 * d2, :], preferred_element_type=F32) + b_ref[0])

        xg = jnp.minimum(proj(wgb, bg_ref), SWIGLU_LIMIT)
        xl = jnp.clip(proj(wlb, bl_ref), -SWIGLU_LIMIT, SWIGLU_LIMIT)
        o_ref[...] = (xg * jax.nn.sigmoid(SWIGLU_ALPHA * xg) * (xl + 1.0)).astype(o_ref.dtype)

    @pl.when(jnp.logical_not(used))
    def _():
        o_ref[...] = jnp.zeros_like(o_ref)


def _moe_up(xs, block_expert, n_used, next_expert, w1, b1, tf=512):
    n_slots, d2 = xs.shape
    e, d, f2 = w1.shape
    f = f2 // 2
    tf = min(tf, f)
    nf = f // tf
    nb = n_slots // MOE_ROWS
    blk = lambda bi, nu: jnp.minimum(bi, nu[0] - 1)
    grid_spec = pltpu.PrefetchScalarGridSpec(
        num_scalar_prefetch=3,
        grid=(nf, nb),
        in_specs=[pl.BlockSpec((MOE_ROWS, d2), lambda fi, bi, be, nu, nx: (blk(bi, nu), 0)),
                  pl.BlockSpec((1, 1, tf), lambda fi, bi, be, nu, nx: (be[blk(bi, nu)], 0, fi)),
                  pl.BlockSpec((1, 1, tf), lambda fi, bi, be, nu, nx: (be[blk(bi, nu)], 0, nf + fi)),
                  pl.BlockSpec(memory_space=pl.ANY)],
        out_specs=pl.BlockSpec((MOE_ROWS, tf), lambda fi, bi, be, nu, nx: (bi, fi)),
        scratch_shapes=[pltpu.VMEM((d, tf), F32), pltpu.VMEM((d, tf), F32),
                        pltpu.VMEM((d, tf), BF16), pltpu.VMEM((d, tf), BF16),
                        pltpu.SemaphoreType.DMA((2,))],
    )
    return pl.pallas_call(
        functools.partial(_moe_up_kernel, nf=nf),
        grid_spec=grid_spec,
        out_shape=jax.ShapeDtypeStruct((n_slots, f), BF16),
        compiler_params=_params(("arbitrary", "arbitrary"), 48),
        name="moe_up",
    )(block_expert, n_used, next_expert, xs, b1.reshape(e, 1, f2), b1.reshape(e, 1, f2), w1)


def _moe_down_kernel(be_ref, nu_ref, nx_ref, h_ref, b_ref, w_hbm, o_ref, sw, wb, sem):
    bi = pl.program_id(0)
    used, fresh = _block_state(be_ref, nu_ref, bi)

    def copies(e):
        return (pltpu.make_async_copy(w_hbm.at[e], sw, sem.at[0]),)

    def cast():
        wb[...] = sw[...].astype(BF16)

    nx = nx_ref[bi]
    _run_weights(copies, (bi == 0, (be_ref[0],)), fresh, cast, [(nx >= 0, (nx,))])

    @pl.when(used)
    def _():
        d2 = o_ref.shape[1]
        h = h_ref[...]
        y_lo = jnp.dot(h, wb[:, 0:d2], preferred_element_type=F32) + b_ref[0, :, 0:d2]
        y_hi = jnp.dot(h, wb[:, d2:2 * d2], preferred_element_type=F32) + b_ref[0, :, d2:2 * d2]
        o_ref[...] = _pack_halves(y_lo, y_hi)

    @pl.when(jnp.logical_not(used))
    def _():
        o_ref[...] = jnp.zeros_like(o_ref)


def _moe_down(hid, block_expert, n_used, next_expert, w2, b2):
    n_slots, f = hid.shape
    e, _, d = w2.shape
    nb = n_slots // MOE_ROWS
    blk = lambda bi, nu: jnp.minimum(bi, nu[0] - 1)
    grid_spec = pltpu.PrefetchScalarGridSpec(
        num_scalar_prefetch=3,
        grid=(nb,),
        in_specs=[pl.BlockSpec((MOE_ROWS, f), lambda bi, be, nu, nx: (blk(bi, nu), 0)),
                  pl.BlockSpec((1, 1, d), lambda bi, be, nu, nx: (be[blk(bi, nu)], 0, 0)),
                  pl.BlockSpec(memory_space=pl.ANY)],
        out_specs=pl.BlockSpec((MOE_ROWS, d // 2), lambda bi, be, nu, nx: (bi, 0)),
        scratch_shapes=[pltpu.VMEM((f, d), F32), pltpu.VMEM((f, d), BF16), pltpu.SemaphoreType.DMA((1,))],
    )
    return pl.pallas_call(
        _moe_down_kernel,
        grid_spec=grid_spec,
        out_shape=jax.ShapeDtypeStruct((n_slots, d // 2), U32),
        compiler_params=_params(("arbitrary",), 48),
        name="moe_down",
    )(block_expert, n_used, next_expert, hid, b2.reshape(e, 1, d), w2)


def _combine_kernel(slot_ref, nslot_ref, gate_ref, h_ref, g_ref, y_hbm, o_ref, ybuf, sem, *, rows):
    i = pl.program_id(0)
    cur = lax.rem(i, 2)

    def issue(s_ref, buf):
        def body(r, carry):
            for kk in range(TOP_K):
                pltpu.make_async_copy(y_hbm.at[pl.ds(s_ref[0, 0, r * TOP_K + kk], 1)],
                                      ybuf.at[buf, kk, pl.ds(r, 1)], sem.at[buf]).start()
            return carry
        lax.fori_loop(0, rows, body, 0)

    @pl.when(i == 0)
    def _():
        issue(slot_ref, 0)

    @pl.when(i + 1 < pl.num_programs(0))
    def _():
        issue(nslot_ref, 1 - cur)

    def wait(r, carry):
        for kk in range(TOP_K):
            pltpu.make_async_copy(y_hbm.at[pl.ds(0, 1)], ybuf.at[cur, kk, pl.ds(r, 1)], sem.at[cur]).wait()
        return carry

    lax.fori_loop(0, rows, wait, 0)

    d2 = ybuf.shape[3]
    gates = gate_ref[...]
    h = h_ref[...]
    acc_lo, acc_hi = h[:, :d2], h[:, d2:]
    for kk in range(TOP_K):
        lo, hi = _unpack_halves(ybuf[cur, kk])
        gk = gates[:, kk:kk + 1]
        acc_lo = acc_lo + gk * lo
        acc_hi = acc_hi + gk * hi
    ms = (jnp.sum(acc_lo * acc_lo, axis=-1, keepdims=True)
          + jnp.sum(acc_hi * acc_hi, axis=-1, keepdims=True)) * (1.0 / (2 * d2))
    inv = lax.rsqrt(ms + EPS)
    o_ref[:, :d2] = acc_lo * inv * g_ref[:, :d2]
    o_ref[:, d2:] = acc_hi * inv * g_ref[:, d2:]


def _combine(y_packed, slot, gates, h, g, rows=128):
    t, d = h.shape
    rows = min(rows, t)
    nsteps = t // rows
    slot3 = slot.reshape(nsteps, 1, rows * TOP_K)
    return pl.pallas_call(
        functools.partial(_combine_kernel, rows=rows),
        grid=(nsteps,),
        in_specs=[pl.BlockSpec((1, 1, rows * TOP_K), lambda i: (i, 0, 0), memory_space=pltpu.SMEM),
                  pl.BlockSpec((1, 1, rows * TOP_K), lambda i: (jnp.minimum(i + 1, nsteps - 1), 0, 0),
                               memory_space=pltpu.SMEM),
                  pl.BlockSpec((rows, LANES), lambda i: (i, 0)),
                  pl.BlockSpec((rows, d), lambda i: (i, 0)),
                  pl.BlockSpec((1, d), lambda i: (0, 0)),
                  pl.BlockSpec(memory_space=pl.ANY)],
        out_specs=pl.BlockSpec((rows, d), lambda i: (i, 0)),
        out_shape=jax.ShapeDtypeStruct((t, d), F32),
        scratch_shapes=[pltpu.VMEM((2, TOP_K, rows, d // 2), U32), pltpu.SemaphoreType.DMA((2,))],
        compiler_params=_params(("arbitrary",), 32),
        name="moe_combine",
    )(slot3, slot3, gates, h, g.reshape(1, d), y_packed)


def _routing(top_idx, n_exp):
    t = top_idx.shape[0]
    experts = jnp.arange(n_exp, dtype=jnp.int32)
    onehot = (top_idx[:, :, None] == experts[None, None, :]).any(axis=1).astype(jnp.int32)
    incl = jnp.cumsum(onehot, axis=0)
    counts = incl[-1]
    rank = jnp.take_along_axis(incl - onehot, top_idx, axis=1)
    padded = (counts + MOE_ROWS - 1) // MOE_ROWS * MOE_ROWS
    pad_ends = jnp.cumsum(padded)
    slot = (pad_ends - padded)[top_idx] + rank
    n_blocks = -(-(t * TOP_K) // MOE_ROWS) + n_exp
    block_rows = jnp.arange(n_blocks, dtype=jnp.int32) * MOE_ROWS
    block_expert = jnp.minimum(jnp.sum(pad_ends[None, :] <= block_rows[:, None], axis=1), n_exp - 1)
    n_used = (pad_ends[-1] // MOE_ROWS).reshape(1)
    zero_start = jnp.where(padded > 0, pad_ends - MOE_ROWS, -1)
    later = jnp.logical_and(experts[None, :] > experts[:, None], padded[None, :] > 0)
    next_of = jnp.min(jnp.where(later, experts[None, :], n_exp), axis=1)
    next_expert = jnp.where(next_of < n_exp, next_of, -1)[block_expert]
    return (slot.astype(jnp.int32), block_expert.astype(jnp.int32), n_used.astype(jnp.int32),
            next_expert.astype(jnp.int32), zero_start.astype(jnp.int32), n_blocks * MOE_ROWS)


def kernel(x, attn_norm_g, w_in, w_a2, b_a, gla_norm_g, b_glu, w_dw, b_dw, conv_ln_g, conv_ln_b, w_out,
           ffn_norm_g, w_router, b_router, w1, b1, w2, b2, final_norm_g):
    b, s, d = x.shape
    assert w_in.shape[0] == 1, "single-layer stack"
    t = b * s
    dv = gla_norm_g.shape[1]
    dk = w_a2.shape[2] // GLA_HEADS
    lowrank = w_a2.shape[1]
    dc = w_dw.shape[2]
    n_exp = w_router.shape[2]
    c_qkvr = 2 * GLA_HEADS * (dk + dv)
    assert lowrank <= LANES and n_exp <= LANES
    x2 = x.reshape(t, d)

    n1 = _rmsnorm(x2, attn_norm_g[0], BF16)
    w_main, w_low = _wprep(jnp.swapaxes(w_in[0], 0, 1), c_qkvr, lowrank)
    proj = _matmul_nt(n1, w_main, "inproj").reshape(b, s, -1)
    a_low = _matmul_nt(n1, w_low, "inproj_lowrank").reshape(b, s, LANES)

    wa = jnp.pad(w_a2[0], ((0, LANES - lowrank), (0, 0))).astype(BF16)
    gla_out = _gla(proj, a_low, wa, b_a[0].reshape(1, -1), gla_norm_g[0].reshape(1, dv), dk, dv)
    conv_out = _conv_module(proj, c_qkvr, dc, b_glu[0], w_dw[0], b_dw[0], conv_ln_g[0], conv_ln_b[0])
    h = _matmul([gla_out.reshape(t, -1), conv_out.reshape(t, dc)], w_out[0], d, "outproj", res=x2)

    wr = jnp.pad(w_router[0], ((0, 0), (0, LANES - n_exp))).astype(BF16)
    br = jnp.pad(b_router[0], (0, LANES - n_exp)).reshape(1, LANES)
    n2, idx_pad, gate_pad = _router(h, ffn_norm_g[0], wr, br, n_exp)
    slot, block_expert, n_used, next_expert, zero_start, n_slots = _routing(idx_pad[:, :TOP_K], n_exp)
    xs = _dispatch(n2, slot, zero_start, n_used, n_slots)
    hid = _moe_up(xs, block_expert, n_used, next_expert, w1[0], b1[0])
    y = _moe_down(hid, block_expert, n_used, next_expert, w2[0], b2[0])
    out = _combine(y, slot, gate_pad, h, final_norm_g)
    return out.reshape(b, s, d)
```

```python
import functools

import jax
import jax.numpy as jnp
from jax import lax
from jax.experimental import pallas as pl
from jax.experimental.pallas import tpu as pltpu

GLA_HEADS = 4
GLA_TAU = 16.0
GLA_CHUNK = 64
TOP_K = 4
SWIGLU_ALPHA = 1.702
SWIGLU_LIMIT = 7.0
EPS = 1e-5

LANES = 128
SUBLANES = 8
MOE_ROWS = 256
CONV_HALO = 32
MIB = 1024 * 1024

F32 = jnp.float32
BF16 = jnp.bfloat16
U32 = jnp.uint32


def _params(semantics, vmem_mib):
    return pltpu.CompilerParams(dimension_semantics=semantics, vmem_limit_bytes=vmem_mib * MIB)


def _pack_halves(lo, hi):
    lo_u = lax.bitcast_convert_type(lo.astype(BF16).astype(F32), U32)
    hi_u = lax.bitcast_convert_type(hi.astype(BF16).astype(F32), U32)
    return (lo_u >> 16) | (hi_u & jnp.uint32(0xFFFF0000))


def _unpack_halves(p):
    lo = lax.bitcast_convert_type(p << 16, F32)
    hi = lax.bitcast_convert_type(p & jnp.uint32(0xFFFF0000), F32)
    return lo, hi


def _rmsnorm_kernel(x_ref, g_ref, o_ref):
    x = x_ref[...]
    y = x * lax.rsqrt(jnp.mean(x * x, axis=-1, keepdims=True) + EPS)
    o_ref[...] = (y * g_ref[...]).astype(o_ref.dtype)


def _rmsnorm(x, g, out_dtype, rows=256):
    t, d = x.shape
    rows = min(rows, t)
    return pl.pallas_call(
        _rmsnorm_kernel,
        grid=(t // rows,),
        in_specs=[pl.BlockSpec((rows, d), lambda i: (i, 0)),
                  pl.BlockSpec((1, d), lambda i: (0, 0))],
        out_specs=pl.BlockSpec((rows, d), lambda i: (i, 0)),
        out_shape=jax.ShapeDtypeStruct((t, d), out_dtype),
        compiler_params=_params(("parallel",), 32),
        name="rmsnorm",
    )(x, g.reshape(1, d))


def _wprep_kernel(a_ref, b_ref, o_ref, low_ref, *, nb_main, lowrank):
    i = pl.program_id(0)
    rows = o_ref.shape[0]

    @pl.when(i < nb_main)
    def _():
        o_ref[...] = a_ref[...].astype(BF16)

    @pl.when(i >= nb_main)
    def _():
        o_ref[0:rows - lowrank, :] = a_ref[lowrank:rows, :].astype(BF16)
        o_ref[rows - lowrank:rows, :] = b_ref[0:lowrank, :].astype(BF16)

    @pl.when(i == nb_main)
    def _():
        low_ref[...] = jnp.zeros_like(low_ref)
        low_ref[0:lowrank, :] = a_ref[0:lowrank, :].astype(BF16)


def _wprep(wt, c_main, lowrank):
    n_in, k = wt.shape
    rows = 2 * LANES
    n_out = n_in - lowrank
    assert c_main % rows == 0 and n_out % rows == 0 and lowrank % 16 == 0
    last = pl.cdiv(n_in, rows) - 1
    return pl.pallas_call(
        functools.partial(_wprep_kernel, nb_main=c_main // rows, lowrank=lowrank),
        grid=(n_out // rows,),
        in_specs=[pl.BlockSpec((rows, k), lambda i: (i, 0)),
                  pl.BlockSpec((rows, k), lambda i: (jnp.minimum(i + 1, last), 0))],
        out_specs=[pl.BlockSpec((rows, k), lambda i: (i, 0)),
                   pl.BlockSpec((LANES, k), lambda i: (0, 0))],
        out_shape=[jax.ShapeDtypeStruct((n_out, k), BF16), jax.ShapeDtypeStruct((LANES, k), BF16)],
        compiler_params=_params(("arbitrary",), 32),
        name="inproj_weight_prep",
    )(wt, wt)


def _mm_nt_kernel(a_ref, wt_ref, o_ref):
    o_ref[...] = lax.dot_general(a_ref[...], wt_ref[...], (((1,), (1,)), ((), ())), preferred_element_type=F32)


def _matmul_nt(a, wt, name, tm=1024, tn=1024):
    m, k = a.shape
    n = wt.shape[0]
    tm, tn = min(tm, m), min(tn, n)
    return pl.pallas_call(
        _mm_nt_kernel,
        grid=(m // tm, n // tn),
        in_specs=[pl.BlockSpec((tm, k), lambda i, j: (i, 0)),
                  pl.BlockSpec((tn, k), lambda i, j: (j, 0))],
        out_specs=pl.BlockSpec((tm, tn), lambda i, j: (i, j)),
        out_shape=jax.ShapeDtypeStruct((m, n), F32),
        compiler_params=_params(("parallel", "arbitrary"), 48),
        name=name,
    )(a, wt)


def _mm_kernel(*refs, n_a, has_res):
    a_refs, w_ref = refs[:n_a], refs[n_a]
    res_ref = refs[n_a + 1] if has_res else None
    o_ref, wb = refs[-2], refs[-1]

    @pl.when(pl.program_id(1) == 0)
    def _():
        wb[...] = w_ref[...].astype(BF16)

    acc, k0 = None, 0
    for a_ref in a_refs:
        kk = a_ref.shape[1]
        part = jnp.dot(a_ref[...], wb[k0:k0 + kk, :], preferred_element_type=F32)
        acc = part if acc is None else acc + part
        k0 += kk
    if has_res:
        acc = acc + res_ref[...]
    o_ref[...] = acc.astype(o_ref.dtype)


def _matmul(a_list, w, n_out, name, res=None, tm=1024, tn=512):
    m = a_list[0].shape[0]
    k = w.shape[0]
    assert sum(a.shape[1] for a in a_list) == k
    tm, tn = min(tm, m), min(tn, n_out)
    in_specs = [pl.BlockSpec((tm, a.shape[1]), lambda j, i: (i, 0)) for a in a_list]
    in_specs.append(pl.BlockSpec((k, tn), lambda j, i: (0, j)))
    args = list(a_list) + [w]
    if res is not None:
        in_specs.append(pl.BlockSpec((tm, tn), lambda j, i: (i, j)))
        args.append(res)
    return pl.pallas_call(
        functools.partial(_mm_kernel, n_a=len(a_list), has_res=res is not None),
        grid=(n_out // tn, m // tm),
        in_specs=in_specs,
        out_specs=pl.BlockSpec((tm, tn), lambda j, i: (i, j)),
        out_shape=jax.ShapeDtypeStruct((m, n_out), F32),
        scratch_shapes=[pltpu.VMEM((k, tn), BF16)],
        compiler_params=_params(("parallel", "arbitrary"), 52),
        name=name,
    )(*args)


def _gla_kernel(q_ref, k_ref, v_ref, r_ref, al_ref, wa_ref, ba_ref, g_ref, o_ref, st_ref, *, chunk, scale):
    @pl.when(pl.program_id(2) == 0)
    def _():
        st_ref[...] = jnp.zeros_like(st_ref)

    ts = q_ref.shape[1]
    heads, dv, dk = st_ref.shape
    rows = lax.broadcasted_iota(jnp.int32, (ts, ts), 0)
    cols = lax.broadcasted_iota(jnp.int32, (ts, ts), 1)
    assert chunk & (chunk - 1) == 0
    same_chunk = (rows & -chunk) == (cols & -chunk)
    causal = jnp.logical_and(same_chunk, rows >= cols)
    tril = jnp.where(causal, 1.0, 0.0).astype(BF16)
    ones = jnp.where(same_chunk, 1.0, 0.0).astype(BF16)
    nt = (((1,), (1,)), ((), ()))
    tn = (((0,), (0,)), ((), ()))

    z = jnp.dot(al_ref[0].astype(BF16), wa_ref[...], preferred_element_type=F32) + ba_ref[...]
    la = (jnp.minimum(z, 0.0) - jnp.log1p(jnp.exp(-jnp.abs(z)))) * (1.0 / GLA_TAU)
    la_hi = la.astype(BF16)
    la_lo = (la - la_hi.astype(F32)).astype(BF16)
    bcum_all = jnp.dot(tril, la_hi, preferred_element_type=F32) + jnp.dot(tril, la_lo, preferred_element_type=F32)
    btot_all = jnp.dot(ones, la_hi, preferred_element_type=F32) + jnp.dot(ones, la_lo, preferred_element_type=F32)
    for i in range(heads):
        ks, vs = slice(i * dk, (i + 1) * dk), slice(i * dv, (i + 1) * dv)
        q, k, v = q_ref[0, :, ks], k_ref[0, :, ks], v_ref[0, :, vs]
        bcum, btot = bcum_all[:, ks], btot_all[:, ks]
        q_e = (q * jnp.exp(bcum) * scale).astype(BF16)
        k_e = (k * jnp.exp(-bcum)).astype(BF16)
        k_d = (k * jnp.exp(btot - bcum)).astype(BF16)
        decay = jnp.exp(btot)
        vb = v.astype(BF16)
        s = lax.dot_general(q_e, k_e, nt, preferred_element_type=F32)
        s = jnp.where(causal, s, 0.0).astype(BF16)
        o_intra = jnp.dot(s, vb, preferred_element_type=F32)

        st = st_ref[i]
        outs = []
        for c in range(ts // chunk):
            lo, hi = c * chunk, (c + 1) * chunk
            outs.append(o_intra[lo:hi]
                        + lax.dot_general(q_e[lo:hi], st.astype(BF16), nt, preferred_element_type=F32))
            st = st * decay[lo:lo + 1] + lax.dot_general(vb[lo:hi], k_d[lo:hi], tn, preferred_element_type=F32)
        st_ref[i] = st
        o = jnp.concatenate(outs, axis=0)
        o = o * lax.rsqrt(jnp.mean(o * o, axis=-1, keepdims=True) + EPS) * g_ref[...]
        r = r_ref[0, :, vs]
        o_ref[0, :, vs] = (o * (r * jax.nn.sigmoid(r))).astype(o_ref.dtype)


def _gla(proj, a_low, wa, ba, g, dk, dv, ts=256, heads_per_step=4):
    b, s, _ = proj.shape
    hp = heads_per_step
    hg = GLA_HEADS // hp
    ts = min(ts, s)
    wk, wv = hp * dk, hp * dv
    kq, kv = (GLA_HEADS * dk) // wk, (2 * GLA_HEADS * dk) // wv
    return pl.pallas_call(
        functools.partial(_gla_kernel, chunk=GLA_CHUNK, scale=dk ** -0.5),
        grid=(b, hg, s // ts),
        in_specs=[pl.BlockSpec((1, ts, wk), lambda bi, hi, si: (bi, si, hi)),
                  pl.BlockSpec((1, ts, wk), lambda bi, hi, si: (bi, si, kq + hi)),
                  pl.BlockSpec((1, ts, wv), lambda bi, hi, si: (bi, si, kv + hi)),
                  pl.BlockSpec((1, ts, wv), lambda bi, hi, si: (bi, si, kv + hg + hi)),
                  pl.BlockSpec((1, ts, LANES), lambda bi, hi, si: (bi, si, 0)),
                  pl.BlockSpec((LANES, wk), lambda bi, hi, si: (0, hi)),
                  pl.BlockSpec((1, wk), lambda bi, hi, si: (0, hi)),
                  pl.BlockSpec((1, dv), lambda bi, hi, si: (0, 0))],
        out_specs=pl.BlockSpec((1, ts, wv), lambda bi, hi, si: (bi, si, hi)),
        out_shape=jax.ShapeDtypeStruct((b, s, GLA_HEADS * dv), BF16),
        scratch_shapes=[pltpu.VMEM((hp, dv, dk), F32)],
        compiler_params=_params(("parallel", "parallel", "arbitrary"), 40),
        name="gla",
    )(proj, proj, proj, proj, a_low, wa, ba, g)


def _conv_kernel(a_ref, b_ref, bga_ref, bgb_ref, w_ref, bdw_ref, lg_ref, lb_ref, o_ref, ubuf, sbuf, cbuf, *,
                 width, cw, rc):
    ts, dc = a_ref.shape[1], a_ref.shape[2]
    si = pl.program_id(1)

    @pl.when(si == 0)
    def _():
        ubuf[0:CONV_HALO, :] = jnp.zeros((CONV_HALO, dc), F32)

    @pl.when(si > 0)
    def _():
        ubuf[0:CONV_HALO, :] = ubuf[ts:ts + CONV_HALO, :]

    ubuf[CONV_HALO:CONV_HALO + ts, :] = (a_ref[0] + bga_ref[...]) * jax.nn.sigmoid(b_ref[0] + bgb_ref[...])

    span = ts + CONV_HALO - SUBLANES

    def col_body(cb, carry):
        cs = pl.ds(pl.multiple_of(cb * cw, cw), cw)
        for b in range(1, SUBLANES):
            sbuf[b, 0:span, :] = ubuf[b:b + span, cs]
        for rb in range(ts // rc):
            acc = jnp.broadcast_to(bdw_ref[:, cs], (rc, cw))
            for j in range(width):
                off = CONV_HALO - (width - 1) + j
                b = off % SUBLANES
                r0 = off - b + rb * rc
                src = ubuf[r0:r0 + rc, cs] if b == 0 else sbuf[b, r0:r0 + rc, :]
                acc = acc + w_ref[j:j + 1, cs] * src
            cbuf[rb * rc:(rb + 1) * rc, cs] = acc
        return carry

    lax.fori_loop(0, dc // cw, col_body, 0)
    c = cbuf[...]
    mu = jnp.mean(c, axis=-1, keepdims=True)
    cen = c - mu
    var = jnp.mean(cen * cen, axis=-1, keepdims=True)
    un = cen * lax.rsqrt(var + EPS) * lg_ref[...] + lb_ref[...]
    o_ref[0] = (un * jax.nn.sigmoid(un)).astype(o_ref.dtype)


def _conv_module(proj, col0, dc, b_glu, w_dw, b_dw, ln_g, ln_b, ts=256):
    b, s, _ = proj.shape
    ts = min(ts, s)
    width = w_dw.shape[0]
    cw = min(128, dc)
    assert width - 1 <= CONV_HALO <= ts and col0 % dc == 0
    cblk = col0 // dc
    row = lambda v: v.reshape(1, dc)
    vec = pl.BlockSpec((1, dc), lambda bi, si: (0, 0))
    return pl.pallas_call(
        functools.partial(_conv_kernel, width=width, cw=cw, rc=min(128, ts)),
        grid=(b, s // ts),
        in_specs=[pl.BlockSpec((1, ts, dc), lambda bi, si: (bi, si, cblk)),
                  pl.BlockSpec((1, ts, dc), lambda bi, si: (bi, si, cblk + 1)),
                  vec, vec,
                  pl.BlockSpec((width, dc), lambda bi, si: (0, 0)),
                  vec, vec, vec],
        out_specs=pl.BlockSpec((1, ts, dc), lambda bi, si: (bi, si, 0)),
        out_shape=jax.ShapeDtypeStruct((b, s, dc), BF16),
        scratch_shapes=[pltpu.VMEM((CONV_HALO + ts, dc), F32),
                        pltpu.VMEM((SUBLANES, CONV_HALO + ts, cw), F32),
                        pltpu.VMEM((ts, dc), F32)],
        compiler_params=_params(("parallel", "arbitrary"), 32),
        name="conv_module",
    )(proj, proj, row(b_glu[:dc]), row(b_glu[dc:]), w_dw, row(b_dw), row(ln_g), row(ln_b))


def _router_kernel(h_ref, g_ref, wr_ref, br_ref, n_ref, idx_ref, gate_ref, *, n_exp):
    h = h_ref[...]
    d2 = h.shape[1] // 2
    n = h * lax.rsqrt(jnp.mean(h * h, axis=-1, keepdims=True) + EPS) * g_ref[...]
    n_ref[...] = _pack_halves(n[:, :d2], n[:, d2:])
    logits = jnp.dot(n.astype(BF16), wr_ref[...], preferred_element_type=F32) + br_ref[...]
    lane = lax.broadcasted_iota(jnp.int32, logits.shape, 1)
    lane_f = lane.astype(F32)
    neg = jnp.float32(-jnp.inf)
    l = jnp.where(lane < n_exp, logits, neg)
    vals, idxs = [], []
    for _ in range(TOP_K):
        m = jnp.max(l, axis=-1, keepdims=True)
        i = jnp.min(jnp.where(l == m, lane_f, float(LANES)), axis=-1, keepdims=True)
        vals.append(m)
        idxs.append(i)
        l = jnp.where(lane_f == i, neg, l)
    es = [jnp.exp(v - vals[0]) for v in vals]
    tot = es[0]
    for e in es[1:]:
        tot = tot + e
    idx_out = jnp.zeros(logits.shape, F32)
    gate_out = jnp.zeros(logits.shape, F32)
    for kk in range(TOP_K):
        idx_out = jnp.where(lane == kk, idxs[kk], idx_out)
        gate_out = jnp.where(lane == kk, es[kk] / tot, gate_out)
    idx_ref[...] = idx_out.astype(jnp.int32)
    gate_ref[...] = gate_out


def _router(h, g, wr, br, n_exp, rows=256):
    t, d = h.shape
    rows = min(rows, t)
    blk = lambda w: pl.BlockSpec((rows, w), lambda i: (i, 0))
    return pl.pallas_call(
        functools.partial(_router_kernel, n_exp=n_exp),
        grid=(t // rows,),
        in_specs=[blk(d),
                  pl.BlockSpec((1, d), lambda i: (0, 0)),
                  pl.BlockSpec((d, LANES), lambda i: (0, 0)),
                  pl.BlockSpec((1, LANES), lambda i: (0, 0))],
        out_specs=[blk(d // 2), blk(LANES), blk(LANES)],
        out_shape=[jax.ShapeDtypeStruct((t, d // 2), U32),
                   jax.ShapeDtypeStruct((t, LANES), jnp.int32),
                   jax.ShapeDtypeStruct((t, LANES), F32)],
        compiler_params=_params(("parallel",), 32),
        name="ffn_norm_router",
    )(h, g.reshape(1, d), wr, br)


def _dispatch_kernel(zs_ref, nu_ref, slot_ref, n_ref, xs_hbm, zbuf, zsem, sem, *, rows, n_exp):
    def zero_copy(row):
        return pltpu.make_async_copy(zbuf, xs_hbm.at[pl.ds(pl.multiple_of(row, MOE_ROWS), MOE_ROWS)], zsem)

    @pl.when(pl.program_id(0) == 0)
    def _():
        zbuf[...] = jnp.zeros_like(zbuf)
        n_blocks = xs_hbm.shape[0] // MOE_ROWS
        for e in range(n_exp):
            @pl.when(zs_ref[e] >= 0)
            def _():
                zero_copy(zs_ref[e]).start()

        def tail_start(bi, carry):
            zero_copy(bi * MOE_ROWS).start()
            return carry

        def tail_wait(bi, carry):
            zero_copy(bi * MOE_ROWS).wait()
            return carry

        lax.fori_loop(nu_ref[0], n_blocks, tail_start, 0)
        for e in range(n_exp):
            @pl.when(zs_ref[e] >= 0)
            def _():
                zero_copy(zs_ref[e]).wait()
        lax.fori_loop(nu_ref[0], n_blocks, tail_wait, 0)

    def row_copy(r, slot):
        return pltpu.make_async_copy(n_ref.at[pl.ds(r, 1)], xs_hbm.at[pl.ds(slot, 1)], sem)

    def start(r, carry):
        for kk in range(TOP_K):
            row_copy(r, slot_ref[0, 0, r * TOP_K + kk]).start()
        return carry

    lax.fori_loop(0, rows, start, 0)
    for kk in range(TOP_K):
        pltpu.make_async_copy(n_ref, xs_hbm.at[pl.ds(0, rows)], sem).wait()


def _dispatch(n_packed, slot, zero_start, n_used, n_slots, rows=256):
    t, d2 = n_packed.shape
    rows = min(rows, t)
    n_exp = zero_start.shape[0]
    grid_spec = pltpu.PrefetchScalarGridSpec(
        num_scalar_prefetch=2,
        grid=(t // rows,),
        in_specs=[pl.BlockSpec((1, 1, rows * TOP_K), lambda i, zs, nu: (i, 0, 0), memory_space=pltpu.SMEM),
                  pl.BlockSpec((rows, d2), lambda i, zs, nu: (i, 0))],
        out_specs=pl.BlockSpec(memory_space=pl.ANY),
        scratch_shapes=[pltpu.VMEM((MOE_ROWS, d2), U32), pltpu.SemaphoreType.DMA(()), pltpu.SemaphoreType.DMA(())],
    )
    return pl.pallas_call(
        functools.partial(_dispatch_kernel, rows=rows, n_exp=n_exp),
        grid_spec=grid_spec,
        out_shape=jax.ShapeDtypeStruct((n_slots, d2), U32),
        compiler_params=_params(("arbitrary",), 32),
        name="moe_dispatch",
    )(zero_start, n_used, slot.reshape(t // rows, 1, rows * TOP_K), n_packed)


def _block_state(be_ref, nu_ref, bi):
    used = bi < nu_ref[0]
    fresh = jnp.logical_or(bi == 0, be_ref[bi] != be_ref[jnp.maximum(bi - 1, 0)])
    return used, jnp.logical_and(used, fresh)


def _run_weights(copies, first, fresh, cast, then):
    first_cond, first_args = first

    @pl.when(first_cond)
    def _():
        for c in copies(*first_args):
            c.start()

    @pl.when(fresh)
    def _():
        for c in copies(*first_args):
            c.wait()
        cast()
        for cond, args in then:
            @pl.when(cond)
            def _():
                for c in copies(*args):
                    c.start()


def _moe_up_kernel(be_ref, nu_ref, nx_ref, x_ref, bg_ref, bl_ref, w_hbm, o_ref, sg, sl, wgb, wlb, sem, *, nf):
    fi, bi = pl.program_id(0), pl.program_id(1)
    used, fresh = _block_state(be_ref, nu_ref, bi)
    tf = sg.shape[1]
    f_all = w_hbm.shape[2] // 2

    def copies(e, f):
        c0 = pl.multiple_of(f * tf, tf)
        return (pltpu.make_async_copy(w_hbm.at[e, :, pl.ds(c0, tf)], sg, sem.at[0]),
                pltpu.make_async_copy(w_hbm.at[e, :, pl.ds(f_all + c0, tf)], sl, sem.at[1]))

    def cast():
        wgb[...] = sg[...].astype(BF16)
        wlb[...] = sl[...].astype(BF16)

    nx = nx_ref[bi]
    _run_weights(copies, (jnp.logical_and(fi == 0, bi == 0), (be_ref[0], 0)), fresh, cast,
                 [(nx >= 0, (nx, fi)),
                  (jnp.logical_and(nx < 0, fi + 1 < nf), (be_ref[0], fi + 1))])

    @pl.when(used)
    def _():
        d2 = x_ref.shape[1]
        lo, hi = _unpack_halves(x_ref[...])
        lo, hi = lo.astype(BF16), hi.astype(BF16)

        def proj(wb, b_ref):
            return (jnp.dot(lo, wb[0:d2, :], preferred_element_type=F32)
                    + jnp.dot(hi, wb[d2:2 * d2, :], preferred_element_type=F32) + b_ref[0])

        xg = jnp.minimum(proj(wgb, bg_ref), SWIGLU_LIMIT)
        xl = jnp.clip(proj(wlb, bl_ref), -SWIGLU_LIMIT, SWIGLU_LIMIT)
        o_ref[...] = (xg * jax.nn.sigmoid(SWIGLU_ALPHA * xg) * (xl + 1.0)).astype(o_ref.dtype)

    @pl.when(jnp.logical_not(used))
    def _():
        o_ref[...] = jnp.zeros_like(o_ref)


def _moe_up(xs, block_expert, n_used, next_expert, w1, b1, tf=512):
    n_slots, d2 = xs.shape
    e, d, f2 = w1.shape
    f = f2 // 2
    tf = min(tf, f)
    nf = f // tf
    nb = n_slots // MOE_ROWS
    blk = lambda bi, nu: jnp.minimum(bi, nu[0] - 1)
    grid_spec = pltpu.PrefetchScalarGridSpec(
        num_scalar_prefetch=3,
        grid=(nf, nb),
        in_specs=[pl.BlockSpec((MOE_ROWS, d2), lambda fi, bi, be, nu, nx: (blk(bi, nu), 0)),
                  pl.BlockSpec((1, 1, tf), lambda fi, bi, be, nu, nx: (be[blk(bi, nu)], 0, fi)),
                  pl.BlockSpec((1, 1, tf), lambda fi, bi, be, nu, nx: (be[blk(bi, nu)], 0, nf + fi)),
                  pl.BlockSpec(memory_space=pl.ANY)],
        out_specs=pl.BlockSpec((MOE_ROWS, tf), lambda fi, bi, be, nu, nx: (bi, fi)),
        scratch_shapes=[pltpu.VMEM((d, tf), F32), pltpu.VMEM((d, tf), F32),
                        pltpu.VMEM((d, tf), BF16), pltpu.VMEM((d, tf), BF16),
                        pltpu.SemaphoreType.DMA((2,))],
    )
    return pl.pallas_call(
        functools.partial(_moe_up_kernel, nf=nf),
        grid_spec=grid_spec,
        out_shape=jax.ShapeDtypeStruct((n_slots, f), BF16),
        compiler_params=_params(("arbitrary", "arbitrary"), 48),
        name="moe_up",
    )(block_expert, n_used, next_expert, xs, b1.reshape(e, 1, f2), b1.reshape(e, 1, f2), w1)


def _moe_down_kernel(be_ref, nu_ref, nx_ref, h_ref, b_ref, w_hbm, o_ref, sw, wb, sem):
    bi = pl.program_id(0)
    used, fresh = _block_state(be_ref, nu_ref, bi)

    def copies(e):
        return (pltpu.make_async_copy(w_hbm.at[e], sw, sem.at[0]),)

    def cast():
        wb[...] = sw[...].astype(BF16)

    nx = nx_ref[bi]
    _run_weights(copies, (bi == 0, (be_ref[0],)), fresh, cast, [(nx >= 0, (nx,))])

    @pl.when(used)
    def _():
        d2 = o_ref.shape[1]
        h = h_ref[...]
        y_lo = jnp.dot(h, wb[:, 0:d2], preferred_element_type=F32) + b_ref[0, :, 0:d2]
        y_hi = jnp.dot(h, wb[:, d2:2 * d2], preferred_element_type=F32) + b_ref[0, :, d2:2 * d2]
        o_ref[...] = _pack_halves(y_lo, y_hi)

    @pl.when(jnp.logical_not(used))
    def _():
        o_ref[...] = jnp.zeros_like(o_ref)


def _moe_down(hid, block_expert, n_used, next_expert, w2, b2):
    n_slots, f = hid.shape
    e, _, d = w2.shape
    nb = n_slots // MOE_ROWS
    blk = lambda bi, nu: jnp.minimum(bi, nu[0] - 1)
    grid_spec = pltpu.PrefetchScalarGridSpec(
        num_scalar_prefetch=3,
        grid=(nb,),
        in_specs=[pl.BlockSpec((MOE_ROWS, f), lambda bi, be, nu, nx: (blk(bi, nu), 0)),
                  pl.BlockSpec((1, 1, d), lambda bi, be, nu, nx: (be[blk(bi, nu)], 0, 0)),
                  pl.BlockSpec(memory_space=pl.ANY)],
        out_specs=pl.BlockSpec((MOE_ROWS, d // 2), lambda bi, be, nu, nx: (bi, 0)),
        scratch_shapes=[pltpu.VMEM((f, d), F32), pltpu.VMEM((f, d), BF16), pltpu.SemaphoreType.DMA((1,))],
    )
    return pl.pallas_call(
        _moe_down_kernel,
        grid_spec=grid_spec,
        out_shape=jax.ShapeDtypeStruct((n_slots, d // 2), U32),
        compiler_params=_params(("arbitrary",), 48),
        name="moe_down",
    )(block_expert, n_used, next_expert, hid, b2.reshape(e, 1, d), w2)


def _combine_kernel(slot_ref, nslot_ref, gate_ref, h_ref, g_ref, y_hbm, o_ref, ybuf, sem, *, rows):
    i = pl.program_id(0)
    cur = lax.rem(i, 2)

    def issue(s_ref, buf):
        def body(r, carry):
            for kk in range(TOP_K):
                pltpu.make_async_copy(y_hbm.at[pl.ds(s_ref[0, 0, r * TOP_K + kk], 1)],
                                      ybuf.at[buf, kk, pl.ds(r, 1)], sem.at[buf]).start()
            return carry
        lax.fori_loop(0, rows, body, 0)

    @pl.when(i == 0)
    def _():
        issue(slot_ref, 0)

    @pl.when(i + 1 < pl.num_programs(0))
    def _():
        issue(nslot_ref, 1 - cur)

    for kk in range(TOP_K):
        pltpu.make_async_copy(y_hbm.at[pl.ds(0, rows)], ybuf.at[cur, kk], sem.at[cur]).wait()

    d2 = ybuf.shape[3]
    gates = gate_ref[...]
    h = h_ref[...]
    acc_lo, acc_hi = h[:, :d2], h[:, d2:]
    for kk in range(TOP_K):
        lo, hi = _unpack_halves(ybuf[cur, kk])
        gk = gates[:, kk:kk + 1]
        acc_lo = acc_lo + gk * lo
        acc_hi = acc_hi + gk * hi
    ms = (jnp.sum(acc_lo * acc_lo, axis=-1, keepdims=True)
          + jnp.sum(acc_hi * acc_hi, axis=-1, keepdims=True)) * (1.0 / (2 * d2))
    inv = lax.rsqrt(ms + EPS)
    o_ref[:, :d2] = acc_lo * inv * g_ref[:, :d2]
    o_ref[:, d2:] = acc_hi * inv * g_ref[:, d2:]


def _combine(y_packed, slot, gates, h, g, rows=128):
    t, d = h.shape
    rows = min(rows, t)
    nsteps = t // rows
    slot3 = slot.reshape(nsteps, 1, rows * TOP_K)
    return pl.pallas_call(
        functools.partial(_combine_kernel, rows=rows),
        grid=(nsteps,),
        in_specs=[pl.BlockSpec((1, 1, rows * TOP_K), lambda i: (i, 0, 0), memory_space=pltpu.SMEM),
                  pl.BlockSpec((1, 1, rows * TOP_K), lambda i: (jnp.minimum(i + 1, nsteps - 1), 0, 0),
                               memory_space=pltpu.SMEM),
                  pl.BlockSpec((rows, LANES), lambda i: (i, 0)),
                  pl.BlockSpec((rows, d), lambda i: (i, 0)),
                  pl.BlockSpec((1, d), lambda i: (0, 0)),
                  pl.BlockSpec(memory_space=pl.ANY)],
        out_specs=pl.BlockSpec((rows, d), lambda i: (i, 0)),
        out_shape=jax.ShapeDtypeStruct((t, d), F32),
        scratch_shapes=[pltpu.VMEM((2, TOP_K, rows, d // 2), U32), pltpu.SemaphoreType.DMA((2,))],
        compiler_params=_params(("arbitrary",), 32),
        name="moe_combine",
    )(slot3, slot3, gates, h, g.reshape(1, d), y_packed)


def _routing(top_idx, n_exp):
    t = top_idx.shape[0]
    experts = jnp.arange(n_exp, dtype=jnp.int32)
    onehot = (top_idx[:, :, None] == experts[None, None, :]).any(axis=1).astype(jnp.int32)
    incl = jnp.cumsum(onehot, axis=0)
    counts = incl[-1]
    padded = (counts + MOE_ROWS - 1) // MOE_ROWS * MOE_ROWS
    pad_ends = jnp.cumsum(padded)
    slot = jnp.take_along_axis(incl - onehot + (pad_ends - padded)[None, :], top_idx, axis=1)
    n_blocks = -(-(t * TOP_K) // MOE_ROWS) + n_exp
    block_rows = jnp.arange(n_blocks, dtype=jnp.int32) * MOE_ROWS
    block_expert = jnp.minimum(jnp.sum(pad_ends[None, :] <= block_rows[:, None], axis=1), n_exp - 1)
    n_used = (pad_ends[-1] // MOE_ROWS).reshape(1)
    zero_start = jnp.where(padded > 0, pad_ends - MOE_ROWS, -1)
    later = jnp.logical_and(experts[None, :] > experts[:, None], padded[None, :] > 0)
    next_of = jnp.min(jnp.where(later, experts[None, :], n_exp), axis=1)
    next_expert = jnp.where(next_of < n_exp, next_of, -1)[block_expert]
    return (slot.astype(jnp.int32), block_expert.astype(jnp.int32), n_used.astype(jnp.int32),
            next_expert.astype(jnp.int32), zero_start.astype(jnp.int32), n_blocks * MOE_ROWS)


def kernel(x, attn_norm_g, w_in, w_a2, b_a, gla_norm_g, b_glu, w_dw, b_dw, conv_ln_g, conv_ln_b, w_out,
           ffn_norm_g, w_router, b_router, w1, b1, w2, b2, final_norm_g):
    b, s, d = x.shape
    assert w_in.shape[0] == 1, "single-layer stack"
    t = b * s
    dv = gla_norm_g.shape[1]
    dk = w_a2.shape[2] // GLA_HEADS
    lowrank = w_a2.shape[1]
    dc = w_dw.shape[2]
    n_exp = w_router.shape[2]
    c_qkvr = 2 * GLA_HEADS * (dk + dv)
    assert lowrank <= LANES and n_exp <= LANES
    x2 = x.reshape(t, d)

    n1 = _rmsnorm(x2, attn_norm_g[0], BF16)
    w_main, w_low = _wprep(jnp.swapaxes(w_in[0], 0, 1), c_qkvr, lowrank)
    proj = _matmul_nt(n1, w_main, "inproj").reshape(b, s, -1)
    a_low = _matmul_nt(n1, w_low, "inproj_lowrank").reshape(b, s, LANES)

    wa = jnp.pad(w_a2[0], ((0, LANES - lowrank), (0, 0))).astype(BF16)
    gla_out = _gla(proj, a_low, wa, b_a[0].reshape(1, -1), gla_norm_g[0].reshape(1, dv), dk, dv)
    conv_out = _conv_module(proj, c_qkvr, dc, b_glu[0], w_dw[0], b_dw[0], conv_ln_g[0], conv_ln_b[0])
    h = _matmul([gla_out.reshape(t, -1), conv_out.reshape(t, dc)], w_out[0], d, "outproj", res=x2)

    wr = jnp.pad(w_router[0], ((0, 0), (0, LANES - n_exp))).astype(BF16)
    br = jnp.pad(b_router[0], (0, LANES - n_exp)).reshape(1, LANES)
    n2, idx_pad, gate_pad = _router(h, ffn_norm_g[0], wr, br, n_exp)
    slot, block_expert, n_used, next_expert, zero_start, n_slots = _routing(idx_pad[:, :TOP_K], n_exp)
    xs = _dispatch(n2, slot, zero_start, n_used, n_slots)
    hid = _moe_up(xs, block_expert, n_used, next_expert, w1[0], b1[0])
    y = _moe_down(hid, block_expert, n_used, next_expert, w2[0], b2[0])
    out = _combine(y, slot, gate_pad, h, final_norm_g)
    return out.reshape(b, s, d)
```

```python
import functools

import jax
import jax.numpy as jnp
from jax import lax
from jax.experimental import pallas as pl
from jax.experimental.pallas import tpu as pltpu

GLA_HEADS = 4
GLA_TAU = 16.0
GLA_CHUNK = 64
TOP_K = 4
SWIGLU_ALPHA = 1.702
SWIGLU_LIMIT = 7.0
EPS = 1e-5

LANES = 128
SUBLANES = 8
MOE_ROWS = 256
CONV_HALO = 32
BULK_DMA_PRIORITY = 1
MIB = 1024 * 1024

F32 = jnp.float32
BF16 = jnp.bfloat16
U32 = jnp.uint32


def _params(semantics, vmem_mib):
    return pltpu.CompilerParams(dimension_semantics=semantics, vmem_limit_bytes=vmem_mib * MIB)


def _pack_halves(lo, hi):
    lo_u = lax.bitcast_convert_type(lo.astype(BF16).astype(F32), U32)
    hi_u = lax.bitcast_convert_type(hi.astype(BF16).astype(F32), U32)
    return (lo_u >> 16) | (hi_u & jnp.uint32(0xFFFF0000))


def _unpack_halves(p):
    lo = lax.bitcast_convert_type(p << 16, F32)
    hi = lax.bitcast_convert_type(p & jnp.uint32(0xFFFF0000), F32)
    return lo, hi


def _rmsnorm_kernel(x_ref, g_ref, o_ref):
    x = x_ref[...]
    y = x * lax.rsqrt(jnp.mean(x * x, axis=-1, keepdims=True) + EPS)
    o_ref[...] = (y * g_ref[...]).astype(o_ref.dtype)


def _rmsnorm(x, g, out_dtype, rows=256):
    t, d = x.shape
    rows = min(rows, t)
    return pl.pallas_call(
        _rmsnorm_kernel,
        grid=(t // rows,),
        in_specs=[pl.BlockSpec((rows, d), lambda i: (i, 0)),
                  pl.BlockSpec((1, d), lambda i: (0, 0))],
        out_specs=pl.BlockSpec((rows, d), lambda i: (i, 0)),
        out_shape=jax.ShapeDtypeStruct((t, d), out_dtype),
        compiler_params=_params(("parallel",), 32),
        name="rmsnorm",
    )(x, g.reshape(1, d))


def _wprep_kernel(a_ref, b_ref, o_ref, low_ref, prev, *, nb_main, lowrank):
    i = pl.program_id(0)
    rows = o_ref.shape[0]

    def straddle(first_ref):
        o_ref[0:rows - lowrank, :] = first_ref[lowrank:rows, :].astype(BF16)
        o_ref[rows - lowrank:rows, :] = b_ref[0:lowrank, :].astype(BF16)
        prev[...] = b_ref[...]

    @pl.when(i < nb_main)
    def _():
        o_ref[...] = a_ref[...].astype(BF16)

    @pl.when(i == nb_main)
    def _():
        low_ref[...] = jnp.zeros_like(low_ref)
        low_ref[0:lowrank, :] = a_ref[0:lowrank, :].astype(BF16)
        straddle(a_ref)

    @pl.when(i > nb_main)
    def _():
        straddle(prev)


def _wprep(wt, c_main, lowrank):
    n_in, k = wt.shape
    rows = 2 * LANES
    n_out = n_in - lowrank
    assert c_main % rows == 0 and n_out % rows == 0 and lowrank % 16 == 0
    last = pl.cdiv(n_in, rows) - 1
    nb_main = c_main // rows
    return pl.pallas_call(
        functools.partial(_wprep_kernel, nb_main=nb_main, lowrank=lowrank),
        grid=(n_out // rows,),
        in_specs=[pl.BlockSpec((rows, k), lambda i: (jnp.minimum(i, nb_main), 0)),
                  pl.BlockSpec((rows, k), lambda i: (jnp.minimum(jnp.maximum(i, nb_main) + 1, last), 0))],
        out_specs=[pl.BlockSpec((rows, k), lambda i: (i, 0)),
                   pl.BlockSpec((LANES, k), lambda i: (0, 0))],
        out_shape=[jax.ShapeDtypeStruct((n_out, k), BF16), jax.ShapeDtypeStruct((LANES, k), BF16)],
        scratch_shapes=[pltpu.VMEM((rows, k), F32)],
        compiler_params=_params(("arbitrary",), 32),
        name="inproj_weight_prep",
    )(wt, wt)


def _mm_nt_kernel(a_ref, wt_ref, o_ref):
    o_ref[...] = lax.dot_general(a_ref[...], wt_ref[...], (((1,), (1,)), ((), ())), preferred_element_type=F32)


def _matmul_nt(a, wt, name, tm=1024, tn=1024):
    m, k = a.shape
    n = wt.shape[0]
    tm, tn = min(tm, m), min(tn, n)
    return pl.pallas_call(
        _mm_nt_kernel,
        grid=(m // tm, n // tn),
        in_specs=[pl.BlockSpec((tm, k), lambda i, j: (i, 0)),
                  pl.BlockSpec((tn, k), lambda i, j: (j, 0))],
        out_specs=pl.BlockSpec((tm, tn), lambda i, j: (i, j)),
        out_shape=jax.ShapeDtypeStruct((m, n), F32),
        compiler_params=_params(("parallel", "arbitrary"), 48),
        name=name,
    )(a, wt)


def _mm_kernel(*refs, n_a, has_res):
    a_refs, w_ref = refs[:n_a], refs[n_a]
    res_ref = refs[n_a + 1] if has_res else None
    o_ref, wb = refs[-2], refs[-1]

    @pl.when(pl.program_id(1) == 0)
    def _():
        wb[...] = w_ref[...].astype(BF16)

    acc, k0 = None, 0
    for a_ref in a_refs:
        kk = a_ref.shape[1]
        part = jnp.dot(a_ref[...], wb[k0:k0 + kk, :], preferred_element_type=F32)
        acc = part if acc is None else acc + part
        k0 += kk
    if has_res:
        acc = acc + res_ref[...]
    o_ref[...] = acc.astype(o_ref.dtype)


def _matmul(a_list, w, n_out, name, res=None, tm=1024, tn=512):
    m = a_list[0].shape[0]
    k = w.shape[0]
    assert sum(a.shape[1] for a in a_list) == k
    tm, tn = min(tm, m), min(tn, n_out)
    in_specs = [pl.BlockSpec((tm, a.shape[1]), lambda j, i: (i, 0)) for a in a_list]
    in_specs.append(pl.BlockSpec((k, tn), lambda j, i: (0, j)))
    args = list(a_list) + [w]
    if res is not None:
        in_specs.append(pl.BlockSpec((tm, tn), lambda j, i: (i, j)))
        args.append(res)
    return pl.pallas_call(
        functools.partial(_mm_kernel, n_a=len(a_list), has_res=res is not None),
        grid=(n_out // tn, m // tm),
        in_specs=in_specs,
        out_specs=pl.BlockSpec((tm, tn), lambda j, i: (i, j)),
        out_shape=jax.ShapeDtypeStruct((m, n_out), F32),
        scratch_shapes=[pltpu.VMEM((k, tn), BF16)],
        compiler_params=_params(("parallel", "arbitrary"), 52),
        name=name,
    )(*args)


def _gla_kernel(q_ref, k_ref, v_ref, r_ref, al_ref, wa_ref, ba_ref, g_ref, o_ref, st_ref, *, chunk, scale):
    @pl.when(pl.program_id(2) == 0)
    def _():
        st_ref[...] = jnp.zeros_like(st_ref)

    ts = q_ref.shape[1]
    heads, dv, dk = st_ref.shape
    rows = lax.broadcasted_iota(jnp.int32, (ts, ts), 0)
    cols = lax.broadcasted_iota(jnp.int32, (ts, ts), 1)
    assert chunk & (chunk - 1) == 0
    same_chunk = (rows & -chunk) == (cols & -chunk)
    causal = jnp.logical_and(same_chunk, rows >= cols)
    tril = jnp.where(causal, 1.0, 0.0).astype(BF16)
    ones = jnp.where(same_chunk, 1.0, 0.0).astype(BF16)
    nt = (((1,), (1,)), ((), ()))
    tn = (((0,), (0,)), ((), ()))

    z = jnp.dot(al_ref[0].astype(BF16), wa_ref[...], preferred_element_type=F32) + ba_ref[...]
    la = (jnp.minimum(z, 0.0) - jnp.log1p(jnp.exp(-jnp.abs(z)))) * (1.0 / GLA_TAU)
    la_hi = la.astype(BF16)
    la_lo = (la - la_hi.astype(F32)).astype(BF16)
    bcum_all = jnp.dot(tril, la_hi, preferred_element_type=F32) + jnp.dot(tril, la_lo, preferred_element_type=F32)
    btot_all = jnp.dot(ones, la_hi, preferred_element_type=F32) + jnp.dot(ones, la_lo, preferred_element_type=F32)
    for i in range(heads):
        ks, vs = slice(i * dk, (i + 1) * dk), slice(i * dv, (i + 1) * dv)
        q, k, v = q_ref[0, :, ks], k_ref[0, :, ks], v_ref[0, :, vs]
        bcum, btot = bcum_all[:, ks], btot_all[:, ks]
        q_e = (q * jnp.exp(bcum) * scale).astype(BF16)
        k_e = (k * jnp.exp(-bcum)).astype(BF16)
        k_d = (k * jnp.exp(btot - bcum)).astype(BF16)
        decay = jnp.exp(btot)
        vb = v.astype(BF16)
        s = lax.dot_general(q_e, k_e, nt, preferred_element_type=F32)
        s = jnp.where(causal, s, 0.0).astype(BF16)
        o_intra = jnp.dot(s, vb, preferred_element_type=F32)

        st = st_ref[i]
        outs = []
        for c in range(ts // chunk):
            lo, hi = c * chunk, (c + 1) * chunk
            outs.append(o_intra[lo:hi]
                        + lax.dot_general(q_e[lo:hi], st.astype(BF16), nt, preferred_element_type=F32))
            st = st * decay[lo:lo + 1] + lax.dot_general(vb[lo:hi], k_d[lo:hi], tn, preferred_element_type=F32)
        st_ref[i] = st
        o = jnp.concatenate(outs, axis=0)
        o = o * lax.rsqrt(jnp.mean(o * o, axis=-1, keepdims=True) + EPS) * g_ref[...]
        r = r_ref[0, :, vs]
        o_ref[0, :, vs] = (o * (r * jax.nn.sigmoid(r))).astype(o_ref.dtype)


def _gla(proj, a_low, wa, ba, g, dk, dv, ts=256, heads_per_step=4):
    b, s, _ = proj.shape
    hp = heads_per_step
    hg = GLA_HEADS // hp
    ts = min(ts, s)
    wk, wv = hp * dk, hp * dv
    kq, kv = (GLA_HEADS * dk) // wk, (2 * GLA_HEADS * dk) // wv
    return pl.pallas_call(
        functools.partial(_gla_kernel, chunk=GLA_CHUNK, scale=dk ** -0.5),
        grid=(b, hg, s // ts),
        in_specs=[pl.BlockSpec((1, ts, wk), lambda bi, hi, si: (bi, si, hi)),
                  pl.BlockSpec((1, ts, wk), lambda bi, hi, si: (bi, si, kq + hi)),
                  pl.BlockSpec((1, ts, wv), lambda bi, hi, si: (bi, si, kv + hi)),
                  pl.BlockSpec((1, ts, wv), lambda bi, hi, si: (bi, si, kv + hg + hi)),
                  pl.BlockSpec((1, ts, LANES), lambda bi, hi, si: (bi, si, 0)),
                  pl.BlockSpec((LANES, wk), lambda bi, hi, si: (0, hi)),
                  pl.BlockSpec((1, wk), lambda bi, hi, si: (0, hi)),
                  pl.BlockSpec((1, dv), lambda bi, hi, si: (0, 0))],
        out_specs=pl.BlockSpec((1, ts, wv), lambda bi, hi, si: (bi, si, hi)),
        out_shape=jax.ShapeDtypeStruct((b, s, GLA_HEADS * dv), BF16),
        scratch_shapes=[pltpu.VMEM((hp, dv, dk), F32)],
        compiler_params=_params(("parallel", "parallel", "arbitrary"), 40),
        name="gla",
    )(proj, proj, proj, proj, a_low, wa, ba, g)


def _conv_kernel(a_ref, b_ref, bga_ref, bgb_ref, w_ref, bdw_ref, lg_ref, lb_ref, o_ref, ubuf, sbuf, cbuf, *,
                 width, cw, rc):
    ts, dc = a_ref.shape[1], a_ref.shape[2]
    si = pl.program_id(1)

    @pl.when(si == 0)
    def _():
        ubuf[0:CONV_HALO, :] = jnp.zeros((CONV_HALO, dc), F32)

    @pl.when(si > 0)
    def _():
        ubuf[0:CONV_HALO, :] = ubuf[ts:ts + CONV_HALO, :]

    ubuf[CONV_HALO:CONV_HALO + ts, :] = (a_ref[0] + bga_ref[...]) * jax.nn.sigmoid(b_ref[0] + bgb_ref[...])

    span = ts + CONV_HALO - SUBLANES

    def col_body(cb, carry):
        cs = pl.ds(pl.multiple_of(cb * cw, cw), cw)
        for b in range(1, SUBLANES):
            sbuf[b, 0:span, :] = ubuf[b:b + span, cs]
        for rb in range(ts // rc):
            acc = jnp.broadcast_to(bdw_ref[:, cs], (rc, cw))
            for j in range(width):
                off = CONV_HALO - (width - 1) + j
                b = off % SUBLANES
                r0 = off - b + rb * rc
                src = ubuf[r0:r0 + rc, cs] if b == 0 else sbuf[b, r0:r0 + rc, :]
                acc = acc + w_ref[j:j + 1, cs] * src
            cbuf[rb * rc:(rb + 1) * rc, cs] = acc
        return carry

    lax.fori_loop(0, dc // cw, col_body, 0)
    c = cbuf[...]
    mu = jnp.mean(c, axis=-1, keepdims=True)
    cen = c - mu
    var = jnp.mean(cen * cen, axis=-1, keepdims=True)
    un = cen * lax.rsqrt(var + EPS) * lg_ref[...] + lb_ref[...]
    o_ref[0] = (un * jax.nn.sigmoid(un)).astype(o_ref.dtype)


def _conv_module(proj, col0, dc, b_glu, w_dw, b_dw, ln_g, ln_b, ts=256):
    b, s, _ = proj.shape
    ts = min(ts, s)
    width = w_dw.shape[0]
    cw = min(128, dc)
    assert width - 1 <= CONV_HALO <= ts and col0 % dc == 0
    cblk = col0 // dc
    row = lambda v: v.reshape(1, dc)
    vec = pl.BlockSpec((1, dc), lambda bi, si: (0, 0))
    return pl.pallas_call(
        functools.partial(_conv_kernel, width=width, cw=cw, rc=min(128, ts)),
        grid=(b, s // ts),
        in_specs=[pl.BlockSpec((1, ts, dc), lambda bi, si: (bi, si, cblk)),
                  pl.BlockSpec((1, ts, dc), lambda bi, si: (bi, si, cblk + 1)),
                  vec, vec,
                  pl.BlockSpec((width, dc), lambda bi, si: (0, 0)),
                  vec, vec, vec],
        out_specs=pl.BlockSpec((1, ts, dc), lambda bi, si: (bi, si, 0)),
        out_shape=jax.ShapeDtypeStruct((b, s, dc), BF16),
        scratch_shapes=[pltpu.VMEM((CONV_HALO + ts, dc), F32),
                        pltpu.VMEM((SUBLANES, CONV_HALO + ts, cw), F32),
                        pltpu.VMEM((ts, dc), F32)],
        compiler_params=_params(("parallel", "arbitrary"), 32),
        name="conv_module",
    )(proj, proj, row(b_glu[:dc]), row(b_glu[dc:]), w_dw, row(b_dw), row(ln_g), row(ln_b))


def _router_kernel(h_ref, g_ref, wr_ref, br_ref, n_ref, idx_ref, gate_ref, *, n_exp):
    h = h_ref[...]
    d2 = h.shape[1] // 2
    n = h * lax.rsqrt(jnp.mean(h * h, axis=-1, keepdims=True) + EPS) * g_ref[...]
    n_ref[...] = _pack_halves(n[:, :d2], n[:, d2:])
    logits = jnp.dot(n.astype(BF16), wr_ref[...], preferred_element_type=F32) + br_ref[...]
    lane = lax.broadcasted_iota(jnp.int32, logits.shape, 1)
    lane_f = lane.astype(F32)
    neg = jnp.float32(-jnp.inf)
    l = jnp.where(lane < n_exp, logits, neg)
    vals, idxs = [], []
    for _ in range(TOP_K):
        m = jnp.max(l, axis=-1, keepdims=True)
        i = jnp.min(jnp.where(l == m, lane_f, float(LANES)), axis=-1, keepdims=True)
        vals.append(m)
        idxs.append(i)
        l = jnp.where(lane_f == i, neg, l)
    es = [jnp.exp(v - vals[0]) for v in vals]
    tot = es[0]
    for e in es[1:]:
        tot = tot + e
    idx_out = jnp.zeros(logits.shape, F32)
    gate_out = jnp.zeros(logits.shape, F32)
    for kk in range(TOP_K):
        idx_out = jnp.where(lane == kk, idxs[kk], idx_out)
        gate_out = jnp.where(lane == kk, es[kk] / tot, gate_out)
    idx_ref[...] = idx_out.astype(jnp.int32)
    gate_ref[...] = gate_out


def _router(h, g, wr, br, n_exp, rows=256):
    t, d = h.shape
    rows = min(rows, t)
    blk = lambda w: pl.BlockSpec((rows, w), lambda i: (i, 0))
    return pl.pallas_call(
        functools.partial(_router_kernel, n_exp=n_exp),
        grid=(t // rows,),
        in_specs=[blk(d),
                  pl.BlockSpec((1, d), lambda i: (0, 0)),
                  pl.BlockSpec((d, LANES), lambda i: (0, 0)),
                  pl.BlockSpec((1, LANES), lambda i: (0, 0))],
        out_specs=[blk(d // 2), blk(LANES), blk(LANES)],
        out_shape=[jax.ShapeDtypeStruct((t, d // 2), U32),
                   jax.ShapeDtypeStruct((t, LANES), jnp.int32),
                   jax.ShapeDtypeStruct((t, LANES), F32)],
        compiler_params=_params(("parallel",), 32),
        name="ffn_norm_router",
    )(h, g.reshape(1, d), wr, br)


def _dispatch_kernel(zs_ref, nu_ref, slot_ref, n_ref, xs_hbm, zbuf, zsem, sem, *, rows, n_exp):
    def zero_copy(row):
        return pltpu.make_async_copy(zbuf, xs_hbm.at[pl.ds(pl.multiple_of(row, MOE_ROWS), MOE_ROWS)], zsem)

    @pl.when(pl.program_id(0) == 0)
    def _():
        zbuf[...] = jnp.zeros_like(zbuf)
        n_blocks = xs_hbm.shape[0] // MOE_ROWS
        for e in range(n_exp):
            @pl.when(zs_ref[e] >= 0)
            def _():
                zero_copy(zs_ref[e]).start()

        def tail_start(bi, carry):
            zero_copy(bi * MOE_ROWS).start()
            return carry

        def tail_wait(bi, carry):
            zero_copy(bi * MOE_ROWS).wait()
            return carry

        lax.fori_loop(nu_ref[0], n_blocks, tail_start, 0)
        for e in range(n_exp):
            @pl.when(zs_ref[e] >= 0)
            def _():
                zero_copy(zs_ref[e]).wait()
        lax.fori_loop(nu_ref[0], n_blocks, tail_wait, 0)

    def row_copy(r, slot):
        return pltpu.make_async_copy(n_ref.at[pl.ds(r, 1)], xs_hbm.at[pl.ds(slot, 1)], sem)

    def start(r, carry):
        for kk in range(TOP_K):
            row_copy(r, slot_ref[0, 0, r * TOP_K + kk]).start(priority=kk % 2)
        return carry

    lax.fori_loop(0, rows, start, 0)
    for kk in range(TOP_K):
        pltpu.make_async_copy(n_ref, xs_hbm.at[pl.ds(0, rows)], sem).wait()


def _dispatch(n_packed, slot, zero_start, n_used, n_slots, rows=256):
    t, d2 = n_packed.shape
    rows = min(rows, t)
    n_exp = zero_start.shape[0]
    grid_spec = pltpu.PrefetchScalarGridSpec(
        num_scalar_prefetch=2,
        grid=(t // rows,),
        in_specs=[pl.BlockSpec((1, 1, rows * TOP_K), lambda i, zs, nu: (i, 0, 0), memory_space=pltpu.SMEM),
                  pl.BlockSpec((rows, d2), lambda i, zs, nu: (i, 0))],
        out_specs=pl.BlockSpec(memory_space=pl.ANY),
        scratch_shapes=[pltpu.VMEM((MOE_ROWS, d2), U32), pltpu.SemaphoreType.DMA(()), pltpu.SemaphoreType.DMA(())],
    )
    return pl.pallas_call(
        functools.partial(_dispatch_kernel, rows=rows, n_exp=n_exp),
        grid_spec=grid_spec,
        out_shape=jax.ShapeDtypeStruct((n_slots, d2), U32),
        compiler_params=_params(("arbitrary",), 32),
        name="moe_dispatch",
    )(zero_start, n_used, slot.reshape(t // rows, 1, rows * TOP_K), n_packed)


def _block_state(be_ref, nu_ref, bi):
    used = bi < nu_ref[0]
    fresh = jnp.logical_or(bi == 0, be_ref[bi] != be_ref[jnp.maximum(bi - 1, 0)])
    return used, jnp.logical_and(used, fresh)


def _run_weights(copies, first, fresh, cast, then):
    first_cond, first_args = first

    @pl.when(first_cond)
    def _():
        for c in copies(*first_args):
            c.start(priority=BULK_DMA_PRIORITY)

    @pl.when(fresh)
    def _():
        for c in copies(*first_args):
            c.wait()
        cast()
        for cond, args in then:
            @pl.when(cond)
            def _():
                for c in copies(*args):
                    c.start(priority=BULK_DMA_PRIORITY)


def _moe_up_kernel(be_ref, nu_ref, nx_ref, x_ref, bg_ref, bl_ref, w_hbm, o_ref, sg, sl, wgb, wlb, sem, *, nf):
    fi, bi = pl.program_id(0), pl.program_id(1)
    used, fresh = _block_state(be_ref, nu_ref, bi)
    tf = sg.shape[1]
    f_all = w_hbm.shape[2] // 2

    def copies(e, f):
        c0 = pl.multiple_of(f * tf, tf)
        return (pltpu.make_async_copy(w_hbm.at[e, :, pl.ds(c0, tf)], sg, sem.at[0]),
                pltpu.make_async_copy(w_hbm.at[e, :, pl.ds(f_all + c0, tf)], sl, sem.at[1]))

    def cast():
        wgb[...] = sg[...].astype(BF16)
        wlb[...] = sl[...].astype(BF16)

    nx = nx_ref[bi]
    _run_weights(copies, (jnp.logical_and(fi == 0, bi == 0), (be_ref[0], 0)), fresh, cast,
                 [(nx >= 0, (nx, fi)),
                  (jnp.logical_and(nx < 0, fi + 1 < nf), (be_ref[0], fi + 1))])

    @pl.when(used)
    def _():
        d2 = x_ref.shape[1]
        lo, hi = _unpack_halves(x_ref[...])
        lo, hi = lo.astype(BF16), hi.astype(BF16)

        def proj(wb, b_ref):
            return (jnp.dot(lo, wb[0:d2, :], preferred_element_type=F32)
                    + jnp.dot(hi, wb[d2:2 * d2, :], preferred_element_type=F32) + b_ref[0])

        xg = jnp.minimum(proj(wgb, bg_ref), SWIGLU_LIMIT)
        xl = jnp.clip(proj(wlb, bl_ref), -SWIGLU_LIMIT, SWIGLU_LIMIT)
        o_ref[...] = (xg * jax.nn.sigmoid(SWIGLU_ALPHA * xg) * (xl + 1.0)).astype(o_ref.dtype)

    @pl.when(jnp.logical_not(used))
    def _():
        o_ref[...] = jnp.zeros_like(o_ref)


def _moe_up(xs, block_expert, n_used, next_expert, w1, b1, tf=512):
    n_slots, d2 = xs.shape
    e, d, f2 = w1.shape
    f = f2 // 2
    tf = min(tf, f)
    nf = f // tf
    nb = n_slots // MOE_ROWS
    blk = lambda bi, nu: jnp.minimum(bi, nu[0] - 1)
    grid_spec = pltpu.PrefetchScalarGridSpec(
        num_scalar_prefetch=3,
        grid=(nf, nb),
        in_specs=[pl.BlockSpec((MOE_ROWS, d2), lambda fi, bi, be, nu, nx: (blk(bi, nu), 0)),
                  pl.BlockSpec((1, 1, tf), lambda fi, bi, be, nu, nx: (be[blk(bi, nu)], 0, fi)),
                  pl.BlockSpec((1, 1, tf), lambda fi, bi, be, nu, nx: (be[blk(bi, nu)], 0, nf + fi)),
                  pl.BlockSpec(memory_space=pl.ANY)],
        out_specs=pl.BlockSpec((MOE_ROWS, tf), lambda fi, bi, be, nu, nx: (bi, fi)),
        scratch_shapes=[pltpu.VMEM((d, tf), F32), pltpu.VMEM((d, tf), F32),
                        pltpu.VMEM((d, tf), BF16), pltpu.VMEM((d, tf), BF16),
                        pltpu.SemaphoreType.DMA((2,))],
    )
    return pl.pallas_call(
        functools.partial(_moe_up_kernel, nf=nf),
        grid_spec=grid_spec,
        out_shape=jax.ShapeDtypeStruct((n_slots, f), BF16),
        compiler_params=_params(("arbitrary", "arbitrary"), 48),
        name="moe_up",
    )(block_expert, n_used, next_expert, xs, b1.reshape(e, 1, f2), b1.reshape(e, 1, f2), w1)


def _moe_down_kernel(be_ref, nu_ref, nx_ref, h_ref, b_ref, w_hbm, o_ref, sw, wb, sem):
    bi = pl.program_id(0)
    used, fresh = _block_state(be_ref, nu_ref, bi)

    def copies(e):
        return (pltpu.make_async_copy(w_hbm.at[e], sw, sem.at[0]),)

    def cast():
        wb[...] = sw[...].astype(BF16)

    nx = nx_ref[bi]
    _run_weights(copies, (bi == 0, (be_ref[0],)), fresh, cast, [(nx >= 0, (nx,))])

    @pl.when(used)
    def _():
        d2 = o_ref.shape[1]
        h = h_ref[...]
        y_lo = jnp.dot(h, wb[:, 0:d2], preferred_element_type=F32) + b_ref[0, :, 0:d2]
        y_hi = jnp.dot(h, wb[:, d2:2 * d2], preferred_element_type=F32) + b_ref[0, :, d2:2 * d2]
        o_ref[...] = _pack_halves(y_lo, y_hi)

    @pl.when(jnp.logical_not(used))
    def _():
        o_ref[...] = jnp.zeros_like(o_ref)


def _moe_down(hid, block_expert, n_used, next_expert, w2, b2):
    n_slots, f = hid.shape
    e, _, d = w2.shape
    nb = n_slots // MOE_ROWS
    blk = lambda bi, nu: jnp.minimum(bi, nu[0] - 1)
    grid_spec = pltpu.PrefetchScalarGridSpec(
        num_scalar_prefetch=3,
        grid=(nb,),
        in_specs=[pl.BlockSpec((MOE_ROWS, f), lambda bi, be, nu, nx: (blk(bi, nu), 0)),
                  pl.BlockSpec((1, 1, d), lambda bi, be, nu, nx: (be[blk(bi, nu)], 0, 0)),
                  pl.BlockSpec(memory_space=pl.ANY)],
        out_specs=pl.BlockSpec((MOE_ROWS, d // 2), lambda bi, be, nu, nx: (bi, 0)),
        scratch_shapes=[pltpu.VMEM((f, d), F32), pltpu.VMEM((f, d), BF16), pltpu.SemaphoreType.DMA((1,))],
    )
    return pl.pallas_call(
        _moe_down_kernel,
        grid_spec=grid_spec,
        out_shape=jax.ShapeDtypeStruct((n_slots, d // 2), U32),
        compiler_params=_params(("arbitrary",), 48),
        name="moe_down",
    )(block_expert, n_used, next_expert, hid, b2.reshape(e, 1, d), w2)


def _combine_kernel(slot_ref, nslot_ref, gate_ref, h_ref, g_ref, y_hbm, o_ref, ybuf, sem, *, rows):
    i = pl.program_id(0)
    cur = lax.rem(i, 2)

    def issue(s_ref, buf):
        def body(r, carry):
            for kk in range(TOP_K):
                pltpu.make_async_copy(y_hbm.at[pl.ds(s_ref[0, 0, r * TOP_K + kk], 1)],
                                      ybuf.at[buf, kk, pl.ds(r, 1)], sem.at[buf]).start(priority=kk % 2)
            return carry
        lax.fori_loop(0, rows, body, 0)

    @pl.when(i == 0)
    def _():
        issue(slot_ref, 0)

    @pl.when(i + 1 < pl.num_programs(0))
    def _():
        issue(nslot_ref, 1 - cur)

    for kk in range(TOP_K):
        pltpu.make_async_copy(y_hbm.at[pl.ds(0, rows)], ybuf.at[cur, kk], sem.at[cur]).wait()

    d2 = ybuf.shape[3]
    gates = gate_ref[...]
    h = h_ref[...]
    acc_lo, acc_hi = h[:, :d2], h[:, d2:]
    for kk in range(TOP_K):
        lo, hi = _unpack_halves(ybuf[cur, kk])
        gk = gates[:, kk:kk + 1]
        acc_lo = acc_lo + gk * lo
        acc_hi = acc_hi + gk * hi
    ms = (jnp.sum(acc_lo * acc_lo, axis=-1, keepdims=True)
          + jnp.sum(acc_hi * acc_hi, axis=-1, keepdims=True)) * (1.0 / (2 * d2))
    inv = lax.rsqrt(ms + EPS)
    o_ref[:, :d2] = acc_lo * inv * g_ref[:, :d2]
    o_ref[:, d2:] = acc_hi * inv * g_ref[:, d2:]


def _combine(y_packed, slot, gates, h, g, rows=128):
    t, d = h.shape
    rows = min(rows, t)
    nsteps = t // rows
    slot3 = slot.reshape(nsteps, 1, rows * TOP_K)
    return pl.pallas_call(
        functools.partial(_combine_kernel, rows=rows),
        grid=(nsteps,),
        in_specs=[pl.BlockSpec((1, 1, rows * TOP_K), lambda i: (i, 0, 0), memory_space=pltpu.SMEM),
                  pl.BlockSpec((1, 1, rows * TOP_K), lambda i: (jnp.minimum(i + 1, nsteps - 1), 0, 0),
                               memory_space=pltpu.SMEM),
                  pl.BlockSpec((rows, LANES), lambda i: (i, 0)),
                  pl.BlockSpec((rows, d), lambda i: (i, 0)),
                  pl.BlockSpec((1, d), lambda i: (0, 0)),
                  pl.BlockSpec(memory_space=pl.ANY)],
        out_specs=pl.BlockSpec((rows, d), lambda i: (i, 0)),
        out_shape=jax.ShapeDtypeStruct((t, d), F32),
        scratch_shapes=[pltpu.VMEM((2, TOP_K, rows, d // 2), U32), pltpu.SemaphoreType.DMA((2,))],
        compiler_params=_params(("arbitrary",), 32),
        name="moe_combine",
    )(slot3, slot3, gates, h, g.reshape(1, d), y_packed)


def _routing(top_idx, n_exp):
    t = top_idx.shape[0]
    experts = jnp.arange(n_exp, dtype=jnp.int32)
    onehot = (top_idx[:, :, None] == experts[None, None, :]).any(axis=1).astype(jnp.int32)
    incl = jnp.cumsum(onehot, axis=0)
    counts = incl[-1]
    padded = (counts + MOE_ROWS - 1) // MOE_ROWS * MOE_ROWS
    pad_ends = jnp.cumsum(padded)
    slot = jnp.take_along_axis(incl - onehot + (pad_ends - padded)[None, :], top_idx, axis=1)
    n_blocks = -(-(t * TOP_K) // MOE_ROWS) + n_exp
    block_rows = jnp.arange(n_blocks, dtype=jnp.int32) * MOE_ROWS
    block_expert = jnp.minimum(jnp.sum(pad_ends[None, :] <= block_rows[:, None], axis=1), n_exp - 1)
    n_used = (pad_ends[-1] // MOE_ROWS).reshape(1)
    zero_start = jnp.where(padded > 0, pad_ends - MOE_ROWS, -1)
    later = jnp.logical_and(experts[None, :] > experts[:, None], padded[None, :] > 0)
    next_of = jnp.min(jnp.where(later, experts[None, :], n_exp), axis=1)
    next_expert = jnp.where(next_of < n_exp, next_of, -1)[block_expert]
    return (slot.astype(jnp.int32), block_expert.astype(jnp.int32), n_used.astype(jnp.int32),
            next_expert.astype(jnp.int32), zero_start.astype(jnp.int32), n_blocks * MOE_ROWS)


def kernel(x, attn_norm_g, w_in, w_a2, b_a, gla_norm_g, b_glu, w_dw, b_dw, conv_ln_g, conv_ln_b, w_out,
           ffn_norm_g, w_router, b_router, w1, b1, w2, b2, final_norm_g):
    b, s, d = x.shape
    assert w_in.shape[0] == 1, "single-layer stack"
    t = b * s
    dv = gla_norm_g.shape[1]
    dk = w_a2.shape[2] // GLA_HEADS
    lowrank = w_a2.shape[1]
    dc = w_dw.shape[2]
    n_exp = w_router.shape[2]
    c_qkvr = 2 * GLA_HEADS * (dk + dv)
    assert lowrank <= LANES and n_exp <= LANES
    x2 = x.reshape(t, d)

    n1 = _rmsnorm(x2, attn_norm_g[0], BF16)
    w_main, w_low = _wprep(jnp.swapaxes(w_in[0], 0, 1), c_qkvr, lowrank)
    proj = _matmul_nt(n1, w_main, "inproj").reshape(b, s, -1)
    a_low = _matmul_nt(n1, w_low, "inproj_lowrank").reshape(b, s, LANES)

    wa = jnp.pad(w_a2[0], ((0, LANES - lowrank), (0, 0))).astype(BF16)
    gla_out = _gla(proj, a_low, wa, b_a[0].reshape(1, -1), gla_norm_g[0].reshape(1, dv), dk, dv)
    conv_out = _conv_module(proj, c_qkvr, dc, b_glu[0], w_dw[0], b_dw[0], conv_ln_g[0], conv_ln_b[0])
    h = _matmul([gla_out.reshape(t, -1), conv_out.reshape(t, dc)], w_out[0], d, "outproj", res=x2)

    wr = jnp.pad(w_router[0], ((0, 0), (0, LANES - n_exp))).astype(BF16)
    br = jnp.pad(b_router[0], (0, LANES - n_exp)).reshape(1, LANES)
    n2, idx_pad, gate_pad = _router(h, ffn_norm_g[0], wr, br, n_exp)
    slot, block_expert, n_used, next_expert, zero_start, n_slots = _routing(idx_pad[:, :TOP_K], n_exp)
    xs = _dispatch(n2, slot, zero_start, n_used, n_slots)
    hid = _moe_up(xs, block_expert, n_used, next_expert, w1[0], b1[0])
    y = _moe_down(hid, block_expert, n_used, next_expert, w2[0], b2[0])
    out = _combine(y, slot, gate_pad, h, final_norm_g)
    return out.reshape(b, s, d)
```

```python
import functools

import jax
import jax.numpy as jnp
from jax import lax
from jax.experimental import pallas as pl
from jax.experimental.pallas import tpu as pltpu

GLA_HEADS = 4
GLA_TAU = 16.0
GLA_CHUNK = 64
TOP_K = 4
SWIGLU_ALPHA = 1.702
SWIGLU_LIMIT = 7.0
EPS = 1e-5

LANES = 128
SUBLANES = 8
MOE_ROWS = 512
MOE_HALF = MOE_ROWS // 2
CONV_HALO = 32
CAST_ROWS = 128
BULK_DMA_PRIORITY = 1
MIB = 1024 * 1024

F32 = jnp.float32
BF16 = jnp.bfloat16
U32 = jnp.uint32


def _params(semantics, vmem_mib):
    return pltpu.CompilerParams(dimension_semantics=semantics, vmem_limit_bytes=vmem_mib * MIB)


def _pack_halves(lo, hi):
    lo_u = lax.bitcast_convert_type(lo.astype(BF16).astype(F32), U32)
    hi_u = lax.bitcast_convert_type(hi.astype(BF16).astype(F32), U32)
    return (lo_u >> 16) | (hi_u & jnp.uint32(0xFFFF0000))


def _unpack_halves(p):
    lo = lax.bitcast_convert_type(p << 16, F32)
    hi = lax.bitcast_convert_type(p & jnp.uint32(0xFFFF0000), F32)
    return lo, hi


def _rmsnorm_kernel(x_ref, g_ref, o_ref):
    x = x_ref[...]
    y = x * lax.rsqrt(jnp.mean(x * x, axis=-1, keepdims=True) + EPS)
    o_ref[...] = (y * g_ref[...]).astype(o_ref.dtype)


def _rmsnorm(x, g, out_dtype, rows=256):
    t, d = x.shape
    rows = min(rows, t)
    return pl.pallas_call(
        _rmsnorm_kernel,
        grid=(t // rows,),
        in_specs=[pl.BlockSpec((rows, d), lambda i: (i, 0)),
                  pl.BlockSpec((1, d), lambda i: (0, 0))],
        out_specs=pl.BlockSpec((rows, d), lambda i: (i, 0)),
        out_shape=jax.ShapeDtypeStruct((t, d), out_dtype),
        compiler_params=_params(("parallel",), 32),
        name="rmsnorm",
    )(x, g.reshape(1, d))


def _wprep_kernel(a_ref, b_ref, o_ref, low_ref, prev, *, nb_main, lowrank):
    i = pl.program_id(0)
    rows = o_ref.shape[0]

    def straddle(first_ref):
        o_ref[0:rows - lowrank, :] = first_ref[lowrank:rows, :].astype(BF16)
        o_ref[rows - lowrank:rows, :] = b_ref[0:lowrank, :].astype(BF16)
        prev[...] = b_ref[...]

    @pl.when(i < nb_main)
    def _():
        o_ref[...] = a_ref[...].astype(BF16)

    @pl.when(i == nb_main)
    def _():
        low_ref[...] = jnp.zeros_like(low_ref)
        low_ref[0:lowrank, :] = a_ref[0:lowrank, :].astype(BF16)
        straddle(a_ref)

    @pl.when(i > nb_main)
    def _():
        straddle(prev)


def _wprep(wt, c_main, lowrank):
    n_in, k = wt.shape
    rows = 2 * LANES
    n_out = n_in - lowrank
    assert c_main % rows == 0 and n_out % rows == 0 and lowrank % 16 == 0
    last = pl.cdiv(n_in, rows) - 1
    nb_main = c_main // rows
    return pl.pallas_call(
        functools.partial(_wprep_kernel, nb_main=nb_main, lowrank=lowrank),
        grid=(n_out // rows,),
        in_specs=[pl.BlockSpec((rows, k), lambda i: (jnp.minimum(i, nb_main), 0)),
                  pl.BlockSpec((rows, k), lambda i: (jnp.minimum(jnp.maximum(i, nb_main) + 1, last), 0))],
        out_specs=[pl.BlockSpec((rows, k), lambda i: (i, 0)),
                   pl.BlockSpec((LANES, k), lambda i: (0, 0))],
        out_shape=[jax.ShapeDtypeStruct((n_out, k), BF16), jax.ShapeDtypeStruct((LANES, k), BF16)],
        scratch_shapes=[pltpu.VMEM((rows, k), F32)],
        compiler_params=_params(("arbitrary",), 32),
        name="inproj_weight_prep",
    )(wt, wt)


def _mm_nt_kernel(a_ref, wt_ref, o_ref):
    o_ref[...] = lax.dot_general(a_ref[...], wt_ref[...], (((1,), (1,)), ((), ())), preferred_element_type=F32)


def _matmul_nt(a, wt, name, tm=1024, tn=1024):
    m, k = a.shape
    n = wt.shape[0]
    tm, tn = min(tm, m), min(tn, n)
    return pl.pallas_call(
        _mm_nt_kernel,
        grid=(m // tm, n // tn),
        in_specs=[pl.BlockSpec((tm, k), lambda i, j: (i, 0)),
                  pl.BlockSpec((tn, k), lambda i, j: (j, 0))],
        out_specs=pl.BlockSpec((tm, tn), lambda i, j: (i, j)),
        out_shape=jax.ShapeDtypeStruct((m, n), F32),
        compiler_params=_params(("parallel", "arbitrary"), 48),
        name=name,
    )(a, wt)


def _mm_kernel(*refs, n_a, has_res):
    a_refs, w_ref = refs[:n_a], refs[n_a]
    res_ref = refs[n_a + 1] if has_res else None
    o_ref, wb = refs[-2], refs[-1]

    @pl.when(pl.program_id(1) == 0)
    def _():
        wb[...] = w_ref[...].astype(BF16)

    acc, k0 = None, 0
    for a_ref in a_refs:
        kk = a_ref.shape[1]
        part = jnp.dot(a_ref[...], wb[k0:k0 + kk, :], preferred_element_type=F32)
        acc = part if acc is None else acc + part
        k0 += kk
    if has_res:
        acc = acc + res_ref[...]
    o_ref[...] = acc.astype(o_ref.dtype)


def _matmul(a_list, w, n_out, name, res=None, tm=1024, tn=512):
    m = a_list[0].shape[0]
    k = w.shape[0]
    assert sum(a.shape[1] for a in a_list) == k
    tm, tn = min(tm, m), min(tn, n_out)
    in_specs = [pl.BlockSpec((tm, a.shape[1]), lambda j, i: (i, 0)) for a in a_list]
    in_specs.append(pl.BlockSpec((k, tn), lambda j, i: (0, j)))
    args = list(a_list) + [w]
    if res is not None:
        in_specs.append(pl.BlockSpec((tm, tn), lambda j, i: (i, j)))
        args.append(res)
    return pl.pallas_call(
        functools.partial(_mm_kernel, n_a=len(a_list), has_res=res is not None),
        grid=(n_out // tn, m // tm),
        in_specs=in_specs,
        out_specs=pl.BlockSpec((tm, tn), lambda j, i: (i, j)),
        out_shape=jax.ShapeDtypeStruct((m, n_out), F32),
        scratch_shapes=[pltpu.VMEM((k, tn), BF16)],
        compiler_params=_params(("parallel", "arbitrary"), 52),
        name=name,
    )(*args)


def _gla_kernel(q_ref, k_ref, v_ref, r_ref, al_ref, wa_ref, ba_ref, g_ref, o_ref, st_ref, *, chunk, scale):
    @pl.when(pl.program_id(2) == 0)
    def _():
        st_ref[...] = jnp.zeros_like(st_ref)

    ts = q_ref.shape[1]
    heads, dv, dk = st_ref.shape
    rows = lax.broadcasted_iota(jnp.int32, (ts, ts), 0)
    cols = lax.broadcasted_iota(jnp.int32, (ts, ts), 1)
    assert chunk & (chunk - 1) == 0
    same_chunk = (rows & -chunk) == (cols & -chunk)
    causal = jnp.logical_and(same_chunk, rows >= cols)
    tril = jnp.where(causal, 1.0, 0.0).astype(BF16)
    ones = jnp.where(same_chunk, 1.0, 0.0).astype(BF16)
    nt = (((1,), (1,)), ((), ()))
    tn = (((0,), (0,)), ((), ()))

    z = jnp.dot(al_ref[0].astype(BF16), wa_ref[...], preferred_element_type=F32) + ba_ref[...]
    la = (jnp.minimum(z, 0.0) - jnp.log1p(jnp.exp(-jnp.abs(z)))) * (1.0 / GLA_TAU)
    la_hi = la.astype(BF16)
    la_lo = (la - la_hi.astype(F32)).astype(BF16)
    bcum_all = jnp.dot(tril, la_hi, preferred_element_type=F32) + jnp.dot(tril, la_lo, preferred_element_type=F32)
    btot_all = jnp.dot(ones, la_hi, preferred_element_type=F32) + jnp.dot(ones, la_lo, preferred_element_type=F32)
    for i in range(heads):
        ks, vs = slice(i * dk, (i + 1) * dk), slice(i * dv, (i + 1) * dv)
        q, k, v = q_ref[0, :, ks], k_ref[0, :, ks], v_ref[0, :, vs]
        bcum, btot = bcum_all[:, ks], btot_all[:, ks]
        q_e = (q * jnp.exp(bcum) * scale).astype(BF16)
        k_e = (k * jnp.exp(-bcum)).astype(BF16)
        k_d = (k * jnp.exp(btot - bcum)).astype(BF16)
        decay = jnp.exp(btot)
        vb = v.astype(BF16)
        s = lax.dot_general(q_e, k_e, nt, preferred_element_type=F32)
        s = jnp.where(causal, s, 0.0).astype(BF16)
        o_intra = jnp.dot(s, vb, preferred_element_type=F32)

        st = st_ref[i]
        outs = []
        for c in range(ts // chunk):
            lo, hi = c * chunk, (c + 1) * chunk
            outs.append(o_intra[lo:hi]
                        + lax.dot_general(q_e[lo:hi], st.astype(BF16), nt, preferred_element_type=F32))
            st = st * decay[lo:lo + 1] + lax.dot_general(vb[lo:hi], k_d[lo:hi], tn, preferred_element_type=F32)
        st_ref[i] = st
        o = jnp.concatenate(outs, axis=0)
        o = o * lax.rsqrt(jnp.mean(o * o, axis=-1, keepdims=True) + EPS) * g_ref[...]
        r = r_ref[0, :, vs]
        o_ref[0, :, vs] = (o * (r * jax.nn.sigmoid(r))).astype(o_ref.dtype)


def _gla(proj, a_low, wa, ba, g, dk, dv, ts=256, heads_per_step=4):
    b, s, _ = proj.shape
    hp = heads_per_step
    hg = GLA_HEADS // hp
    ts = min(ts, s)
    wk, wv = hp * dk, hp * dv
    kq, kv = (GLA_HEADS * dk) // wk, (2 * GLA_HEADS * dk) // wv
    return pl.pallas_call(
        functools.partial(_gla_kernel, chunk=GLA_CHUNK, scale=dk ** -0.5),
        grid=(b, hg, s // ts),
        in_specs=[pl.BlockSpec((1, ts, wk), lambda bi, hi, si: (bi, si, hi)),
                  pl.BlockSpec((1, ts, wk), lambda bi, hi, si: (bi, si, kq + hi)),
                  pl.BlockSpec((1, ts, wv), lambda bi, hi, si: (bi, si, kv + hi)),
                  pl.BlockSpec((1, ts, wv), lambda bi, hi, si: (bi, si, kv + hg + hi)),
                  pl.BlockSpec((1, ts, LANES), lambda bi, hi, si: (bi, si, 0)),
                  pl.BlockSpec((LANES, wk), lambda bi, hi, si: (0, hi)),
                  pl.BlockSpec((1, wk), lambda bi, hi, si: (0, hi)),
                  pl.BlockSpec((1, dv), lambda bi, hi, si: (0, 0))],
        out_specs=pl.BlockSpec((1, ts, wv), lambda bi, hi, si: (bi, si, hi)),
        out_shape=jax.ShapeDtypeStruct((b, s, GLA_HEADS * dv), BF16),
        scratch_shapes=[pltpu.VMEM((hp, dv, dk), F32)],
        compiler_params=_params(("parallel", "parallel", "arbitrary"), 40),
        name="gla",
    )(proj, proj, proj, proj, a_low, wa, ba, g)


def _conv_kernel(a_ref, b_ref, bga_ref, bgb_ref, w_ref, bdw_ref, lg_ref, lb_ref, o_ref, ubuf, sbuf, cbuf, *,
                 width, cw, rc):
    ts, dc = a_ref.shape[1], a_ref.shape[2]
    si = pl.program_id(1)

    @pl.when(si == 0)
    def _():
        ubuf[0:CONV_HALO, :] = jnp.zeros((CONV_HALO, dc), F32)

    @pl.when(si > 0)
    def _():
        ubuf[0:CONV_HALO, :] = ubuf[ts:ts + CONV_HALO, :]

    ubuf[CONV_HALO:CONV_HALO + ts, :] = (a_ref[0] + bga_ref[...]) * jax.nn.sigmoid(b_ref[0] + bgb_ref[...])

    span = ts + CONV_HALO - SUBLANES

    def col_body(cb, carry):
        cs = pl.ds(pl.multiple_of(cb * cw, cw), cw)
        for b in range(1, SUBLANES):
            sbuf[b, 0:span, :] = ubuf[b:b + span, cs]
        for rb in range(ts // rc):
            acc = jnp.broadcast_to(bdw_ref[:, cs], (rc, cw))
            for j in range(width):
                off = CONV_HALO - (width - 1) + j
                b = off % SUBLANES
                r0 = off - b + rb * rc
                src = ubuf[r0:r0 + rc, cs] if b == 0 else sbuf[b, r0:r0 + rc, :]
                acc = acc + w_ref[j:j + 1, cs] * src
            cbuf[rb * rc:(rb + 1) * rc, cs] = acc
        return carry

    lax.fori_loop(0, dc // cw, col_body, 0)
    c = cbuf[...]
    mu = jnp.mean(c, axis=-1, keepdims=True)
    cen = c - mu
    var = jnp.mean(cen * cen, axis=-1, keepdims=True)
    un = cen * lax.rsqrt(var + EPS) * lg_ref[...] + lb_ref[...]
    o_ref[0] = (un * jax.nn.sigmoid(un)).astype(o_ref.dtype)


def _conv_module(proj, col0, dc, b_glu, w_dw, b_dw, ln_g, ln_b, ts=256):
    b, s, _ = proj.shape
    ts = min(ts, s)
    width = w_dw.shape[0]
    cw = min(128, dc)
    assert width - 1 <= CONV_HALO <= ts and col0 % dc == 0
    cblk = col0 // dc
    row = lambda v: v.reshape(1, dc)
    vec = pl.BlockSpec((1, dc), lambda bi, si: (0, 0))
    return pl.pallas_call(
        functools.partial(_conv_kernel, width=width, cw=cw, rc=min(128, ts)),
        grid=(b, s // ts),
        in_specs=[pl.BlockSpec((1, ts, dc), lambda bi, si: (bi, si, cblk)),
                  pl.BlockSpec((1, ts, dc), lambda bi, si: (bi, si, cblk + 1)),
                  vec, vec,
                  pl.BlockSpec((width, dc), lambda bi, si: (0, 0)),
                  vec, vec, vec],
        out_specs=pl.BlockSpec((1, ts, dc), lambda bi, si: (bi, si, 0)),
        out_shape=jax.ShapeDtypeStruct((b, s, dc), BF16),
        scratch_shapes=[pltpu.VMEM((CONV_HALO + ts, dc), F32),
                        pltpu.VMEM((SUBLANES, CONV_HALO + ts, cw), F32),
                        pltpu.VMEM((ts, dc), F32)],
        compiler_params=_params(("parallel", "arbitrary"), 32),
        name="conv_module",
    )(proj, proj, row(b_glu[:dc]), row(b_glu[dc:]), w_dw, row(b_dw), row(ln_g), row(ln_b))


def _router_kernel(h_ref, g_ref, wr_ref, br_ref, n_ref, idx_ref, gate_ref, *, n_exp):
    h = h_ref[...]
    d2 = h.shape[1] // 2
    n = h * lax.rsqrt(jnp.mean(h * h, axis=-1, keepdims=True) + EPS) * g_ref[...]
    n_ref[...] = _pack_halves(n[:, :d2], n[:, d2:])
    logits = jnp.dot(n.astype(BF16), wr_ref[...], preferred_element_type=F32) + br_ref[...]
    lane = lax.broadcasted_iota(jnp.int32, logits.shape, 1)
    lane_f = lane.astype(F32)
    neg = jnp.float32(-jnp.inf)
    l = jnp.where(lane < n_exp, logits, neg)
    vals, idxs = [], []
    for _ in range(TOP_K):
        m = jnp.max(l, axis=-1, keepdims=True)
        i = jnp.min(jnp.where(l == m, lane_f, float(LANES)), axis=-1, keepdims=True)
        vals.append(m)
        idxs.append(i)
        l = jnp.where(lane_f == i, neg, l)
    es = [jnp.exp(v - vals[0]) for v in vals]
    tot = es[0]
    for e in es[1:]:
        tot = tot + e
    idx_out = jnp.zeros(logits.shape, F32)
    gate_out = jnp.zeros(logits.shape, F32)
    for kk in range(TOP_K):
        idx_out = jnp.where(lane == kk, idxs[kk], idx_out)
        gate_out = jnp.where(lane == kk, es[kk] / tot, gate_out)
    idx_ref[...] = idx_out.astype(jnp.int32)
    gate_ref[...] = gate_out


def _router(h, g, wr, br, n_exp, rows=256):
    t, d = h.shape
    rows = min(rows, t)
    blk = lambda w: pl.BlockSpec((rows, w), lambda i: (i, 0))
    return pl.pallas_call(
        functools.partial(_router_kernel, n_exp=n_exp),
        grid=(t // rows,),
        in_specs=[blk(d),
                  pl.BlockSpec((1, d), lambda i: (0, 0)),
                  pl.BlockSpec((d, LANES), lambda i: (0, 0)),
                  pl.BlockSpec((1, LANES), lambda i: (0, 0))],
        out_specs=[blk(d // 2), blk(LANES), blk(LANES)],
        out_shape=[jax.ShapeDtypeStruct((t, d // 2), U32),
                   jax.ShapeDtypeStruct((t, LANES), jnp.int32),
                   jax.ShapeDtypeStruct((t, LANES), F32)],
        compiler_params=_params(("parallel",), 32),
        name="ffn_norm_router",
    )(h, g.reshape(1, d), wr, br)


def _dispatch_kernel(zs_ref, nu_ref, slot_ref, n_ref, xs_hbm, zbuf, zsem, sem, *, rows, n_zero):
    def zero_copy(row):
        return pltpu.make_async_copy(zbuf, xs_hbm.at[pl.ds(pl.multiple_of(row, MOE_HALF), MOE_HALF)], zsem)

    @pl.when(pl.program_id(0) == 0)
    def _():
        zbuf[...] = jnp.zeros_like(zbuf)
        n_halves = xs_hbm.shape[0] // MOE_HALF
        first_unused = nu_ref[0] * (MOE_ROWS // MOE_HALF)
        for z in range(n_zero):
            @pl.when(zs_ref[z] >= 0)
            def _():
                zero_copy(zs_ref[z]).start()

        def tail_start(hi, carry):
            zero_copy(hi * MOE_HALF).start()
            return carry

        def tail_wait(hi, carry):
            zero_copy(hi * MOE_HALF).wait()
            return carry

        lax.fori_loop(first_unused, n_halves, tail_start, 0)
        for z in range(n_zero):
            @pl.when(zs_ref[z] >= 0)
            def _():
                zero_copy(zs_ref[z]).wait()
        lax.fori_loop(first_unused, n_halves, tail_wait, 0)

    def row_copy(r, slot):
        return pltpu.make_async_copy(n_ref.at[pl.ds(r, 1)], xs_hbm.at[pl.ds(slot, 1)], sem)

    def start(r, carry):
        for kk in range(TOP_K):
            row_copy(r, slot_ref[0, 0, r * TOP_K + kk]).start(priority=kk % 2)
        return carry

    lax.fori_loop(0, rows, start, 0)
    for kk in range(TOP_K):
        pltpu.make_async_copy(n_ref, xs_hbm.at[pl.ds(0, rows)], sem).wait()


def _dispatch(n_packed, slot, zero_start, n_used, n_slots, rows=256):
    t, d2 = n_packed.shape
    rows = min(rows, t)
    grid_spec = pltpu.PrefetchScalarGridSpec(
        num_scalar_prefetch=2,
        grid=(t // rows,),
        in_specs=[pl.BlockSpec((1, 1, rows * TOP_K), lambda i, zs, nu: (i, 0, 0), memory_space=pltpu.SMEM),
                  pl.BlockSpec((rows, d2), lambda i, zs, nu: (i, 0))],
        out_specs=pl.BlockSpec(memory_space=pl.ANY),
        scratch_shapes=[pltpu.VMEM((MOE_HALF, d2), U32), pltpu.SemaphoreType.DMA(()), pltpu.SemaphoreType.DMA(())],
    )
    return pl.pallas_call(
        functools.partial(_dispatch_kernel, rows=rows, n_zero=zero_start.shape[0]),
        grid_spec=grid_spec,
        out_shape=jax.ShapeDtypeStruct((n_slots, d2), U32),
        compiler_params=_params(("arbitrary",), 32),
        name="moe_dispatch",
    )(zero_start, n_used, slot.reshape(t // rows, 1, rows * TOP_K), n_packed)


def _block_state(be_ref, nu_ref, bi):
    used = bi < nu_ref[0]
    fresh = jnp.logical_or(bi == 0, be_ref[bi] != be_ref[jnp.maximum(bi - 1, 0)])
    return used, jnp.logical_and(used, fresh)


def _run_weights(copies, first, fresh, cast, then):
    first_cond, first_args = first

    @pl.when(first_cond)
    def _():
        for c in copies(*first_args):
            c.start(priority=BULK_DMA_PRIORITY)

    @pl.when(fresh)
    def _():
        for c in copies(*first_args):
            c.wait()
        cast()
        for cond, args in then:
            @pl.when(cond)
            def _():
                for c in copies(*args):
                    c.start(priority=BULK_DMA_PRIORITY)


def _cast_rows(src, dst, rows):
    def body(i, carry):
        r = pl.ds(pl.multiple_of(i * rows, rows), rows)
        dst[r, :] = src[r, :].astype(BF16)
        return carry
    lax.fori_loop(0, src.shape[0] // rows, body, 0)


def _per_block_halves(halves, compute, o_ref):
    @pl.when(halves == 2)
    def _():
        compute(MOE_ROWS)

    @pl.when(halves == 1)
    def _():
        compute(MOE_HALF)
        o_ref[MOE_HALF:MOE_ROWS, :] = jnp.zeros((MOE_ROWS - MOE_HALF, o_ref.shape[1]), o_ref.dtype)

    @pl.when(halves == 0)
    def _():
        o_ref[...] = jnp.zeros_like(o_ref)


def _moe_up_kernel(be_ref, nu_ref, nx_ref, bh_ref, x_ref, bg_ref, bl_ref, w_hbm, o_ref, sg, sl, wgb, wlb, sem, *,
                   nf):
    fi, bi = pl.program_id(0), pl.program_id(1)
    _, fresh = _block_state(be_ref, nu_ref, bi)
    tf = sg.shape[1]
    f_all = w_hbm.shape[2] // 2

    def copies(e, f):
        c0 = pl.multiple_of(f * tf, tf)
        return (pltpu.make_async_copy(w_hbm.at[e, :, pl.ds(c0, tf)], sg, sem.at[0]),
                pltpu.make_async_copy(w_hbm.at[e, :, pl.ds(f_all + c0, tf)], sl, sem.at[1]))

    def cast():
        _cast_rows(sg, wgb, CAST_ROWS)
        _cast_rows(sl, wlb, CAST_ROWS)

    nx = nx_ref[bi]
    _run_weights(copies, (jnp.logical_and(fi == 0, bi == 0), (be_ref[0], 0)), fresh, cast,
                 [(nx >= 0, (nx, fi)),
                  (jnp.logical_and(nx < 0, fi + 1 < nf), (be_ref[0], fi + 1))])

    def compute(rows):
        d2 = x_ref.shape[1]
        lo, hi = _unpack_halves(x_ref[0:rows, :])
        lo, hi = lo.astype(BF16), hi.astype(BF16)

        def proj(wb, b_ref):
            return (jnp.dot(lo, wb[0:d2, :], preferred_element_type=F32)
                    + jnp.dot(hi, wb[d2:2 * d2, :], preferred_element_type=F32) + b_ref[0])

        xg = jnp.minimum(proj(wgb, bg_ref), SWIGLU_LIMIT)
        xl = jnp.clip(proj(wlb, bl_ref), -SWIGLU_LIMIT, SWIGLU_LIMIT)
        o_ref[0:rows, :] = (xg * jax.nn.sigmoid(SWIGLU_ALPHA * xg) * (xl + 1.0)).astype(o_ref.dtype)

    _per_block_halves(bh_ref[bi], compute, o_ref)


def _moe_up(xs, block_expert, n_used, next_expert, block_halves, w1, b1, tf=512):
    n_slots, d2 = xs.shape
    e, d, f2 = w1.shape
    f = f2 // 2
    tf = min(tf, f)
    nf = f // tf
    nb = n_slots // MOE_ROWS
    blk = lambda bi, nu: jnp.minimum(bi, nu[0] - 1)
    grid_spec = pltpu.PrefetchScalarGridSpec(
        num_scalar_prefetch=4,
        grid=(nf, nb),
        in_specs=[pl.BlockSpec((MOE_ROWS, d2), lambda fi, bi, be, nu, nx, bh: (blk(bi, nu), 0)),
                  pl.BlockSpec((1, 1, tf), lambda fi, bi, be, nu, nx, bh: (be[blk(bi, nu)], 0, fi)),
                  pl.BlockSpec((1, 1, tf), lambda fi, bi, be, nu, nx, bh: (be[blk(bi, nu)], 0, nf + fi)),
                  pl.BlockSpec(memory_space=pl.ANY)],
        out_specs=pl.BlockSpec((MOE_ROWS, tf), lambda fi, bi, be, nu, nx, bh: (bi, fi)),
        scratch_shapes=[pltpu.VMEM((d, tf), F32), pltpu.VMEM((d, tf), F32),
                        pltpu.VMEM((d, tf), BF16), pltpu.VMEM((d, tf), BF16),
                        pltpu.SemaphoreType.DMA((2,))],
    )
    return pl.pallas_call(
        functools.partial(_moe_up_kernel, nf=nf),
        grid_spec=grid_spec,
        out_shape=jax.ShapeDtypeStruct((n_slots, f), BF16),
        compiler_params=_params(("arbitrary", "arbitrary"), 48),
        name="moe_up",
    )(block_expert, n_used, next_expert, block_halves, xs, b1.reshape(e, 1, f2), b1.reshape(e, 1, f2), w1)


def _moe_down_kernel(be_ref, nu_ref, nx_ref, bh_ref, h_ref, b_ref, w_hbm, o_ref, sw, wb, sem):
    bi = pl.program_id(0)
    _, fresh = _block_state(be_ref, nu_ref, bi)

    def copies(e):
        return (pltpu.make_async_copy(w_hbm.at[e], sw, sem.at[0]),)

    def cast():
        wb[...] = sw[...].astype(BF16)

    nx = nx_ref[bi]
    _run_weights(copies, (bi == 0, (be_ref[0],)), fresh, cast, [(nx >= 0, (nx,))])

    def compute(rows):
        d2 = o_ref.shape[1]
        h = h_ref[0:rows, :]
        y_lo = jnp.dot(h, wb[:, 0:d2], preferred_element_type=F32) + b_ref[0, :, 0:d2]
        y_hi = jnp.dot(h, wb[:, d2:2 * d2], preferred_element_type=F32) + b_ref[0, :, d2:2 * d2]
        o_ref[0:rows, :] = _pack_halves(y_lo, y_hi)

    _per_block_halves(bh_ref[bi], compute, o_ref)


def _moe_down(hid, block_expert, n_used, next_expert, block_halves, w2, b2):
    n_slots, f = hid.shape
    e, _, d = w2.shape
    nb = n_slots // MOE_ROWS
    blk = lambda bi, nu: jnp.minimum(bi, nu[0] - 1)
    grid_spec = pltpu.PrefetchScalarGridSpec(
        num_scalar_prefetch=4,
        grid=(nb,),
        in_specs=[pl.BlockSpec((MOE_ROWS, f), lambda bi, be, nu, nx, bh: (blk(bi, nu), 0)),
                  pl.BlockSpec((1, 1, d), lambda bi, be, nu, nx, bh: (be[blk(bi, nu)], 0, 0)),
                  pl.BlockSpec(memory_space=pl.ANY)],
        out_specs=pl.BlockSpec((MOE_ROWS, d // 2), lambda bi, be, nu, nx, bh: (bi, 0)),
        scratch_shapes=[pltpu.VMEM((f, d), F32), pltpu.VMEM((f, d), BF16), pltpu.SemaphoreType.DMA((1,))],
    )
    return pl.pallas_call(
        _moe_down_kernel,
        grid_spec=grid_spec,
        out_shape=jax.ShapeDtypeStruct((n_slots, d // 2), U32),
        compiler_params=_params(("arbitrary",), 48),
        name="moe_down",
    )(block_expert, n_used, next_expert, block_halves, hid, b2.reshape(e, 1, d), w2)


def _combine_kernel(slot_ref, nslot_ref, gate_ref, h_ref, g_ref, y_hbm, o_ref, ybuf, sem, *, rows):
    i = pl.program_id(0)
    cur = lax.rem(i, 2)

    def issue(s_ref, buf):
        def body(r, carry):
            for kk in range(TOP_K):
                pltpu.make_async_copy(y_hbm.at[pl.ds(s_ref[0, 0, r * TOP_K + kk], 1)],
                                      ybuf.at[buf, kk, pl.ds(r, 1)], sem.at[buf]).start(priority=kk % 2)
            return carry
        lax.fori_loop(0, rows, body, 0)

    @pl.when(i == 0)
    def _():
        issue(slot_ref, 0)

    @pl.when(i + 1 < pl.num_programs(0))
    def _():
        issue(nslot_ref, 1 - cur)

    for kk in range(TOP_K):
        pltpu.make_async_copy(y_hbm.at[pl.ds(0, rows)], ybuf.at[cur, kk], sem.at[cur]).wait()

    d2 = ybuf.shape[3]
    gates = gate_ref[...]
    h = h_ref[...]
    acc_lo, acc_hi = h[:, :d2], h[:, d2:]
    for kk in range(TOP_K):
        lo, hi = _unpack_halves(ybuf[cur, kk])
        gk = gates[:, kk:kk + 1]
        acc_lo = acc_lo + gk * lo
        acc_hi = acc_hi + gk * hi
    ms = (jnp.sum(acc_lo * acc_lo, axis=-1, keepdims=True)
          + jnp.sum(acc_hi * acc_hi, axis=-1, keepdims=True)) * (1.0 / (2 * d2))
    inv = lax.rsqrt(ms + EPS)
    o_ref[:, :d2] = acc_lo * inv * g_ref[:, :d2]
    o_ref[:, d2:] = acc_hi * inv * g_ref[:, d2:]


def _combine(y_packed, slot, gates, h, g, rows=128):
    t, d = h.shape
    rows = min(rows, t)
    nsteps = t // rows
    slot3 = slot.reshape(nsteps, 1, rows * TOP_K)
    return pl.pallas_call(
        functools.partial(_combine_kernel, rows=rows),
        grid=(nsteps,),
        in_specs=[pl.BlockSpec((1, 1, rows * TOP_K), lambda i: (i, 0, 0), memory_space=pltpu.SMEM),
                  pl.BlockSpec((1, 1, rows * TOP_K), lambda i: (jnp.minimum(i + 1, nsteps - 1), 0, 0),
                               memory_space=pltpu.SMEM),
                  pl.BlockSpec((rows, LANES), lambda i: (i, 0)),
                  pl.BlockSpec((rows, d), lambda i: (i, 0)),
                  pl.BlockSpec((1, d), lambda i: (0, 0)),
                  pl.BlockSpec(memory_space=pl.ANY)],
        out_specs=pl.BlockSpec((rows, d), lambda i: (i, 0)),
        out_shape=jax.ShapeDtypeStruct((t, d), F32),
        scratch_shapes=[pltpu.VMEM((2, TOP_K, rows, d // 2), U32), pltpu.SemaphoreType.DMA((2,))],
        compiler_params=_params(("arbitrary",), 32),
        name="moe_combine",
    )(slot3, slot3, gates, h, g.reshape(1, d), y_packed)


def _routing(top_idx, n_exp):
    t = top_idx.shape[0]
    experts = jnp.arange(n_exp, dtype=jnp.int32)
    onehot = (top_idx[:, :, None] == experts[None, None, :]).any(axis=1).astype(jnp.int32)
    incl = jnp.cumsum(onehot, axis=0)
    counts = incl[-1]
    padded = (counts + MOE_ROWS - 1) // MOE_ROWS * MOE_ROWS
    pad_ends = jnp.cumsum(padded)
    slot = jnp.take_along_axis(incl - onehot + (pad_ends - padded)[None, :], top_idx, axis=1)
    n_blocks = -(-(t * TOP_K) // MOE_ROWS) + n_exp
    block_rows = jnp.arange(n_blocks, dtype=jnp.int32) * MOE_ROWS
    block_expert = jnp.minimum(jnp.sum(pad_ends[None, :] <= block_rows[:, None], axis=1), n_exp - 1)
    n_used = (pad_ends[-1] // MOE_ROWS).reshape(1)
    rows_in_block = jnp.clip((counts - padded + pad_ends)[block_expert] - block_rows, 0, MOE_ROWS)
    block_halves = jnp.where(block_rows < pad_ends[-1], jnp.where(rows_in_block <= MOE_HALF, 1, 2), 0)
    last_rows = counts - padded + MOE_ROWS
    zero_start = jnp.concatenate([
        jnp.where(jnp.logical_and(counts > 0, last_rows < MOE_ROWS), pad_ends - MOE_HALF, -1),
        jnp.where(jnp.logical_and(counts > 0, last_rows < MOE_HALF), pad_ends - MOE_ROWS, -1)])
    later = jnp.logical_and(experts[None, :] > experts[:, None], padded[None, :] > 0)
    next_of = jnp.min(jnp.where(later, experts[None, :], n_exp), axis=1)
    next_expert = jnp.where(next_of < n_exp, next_of, -1)[block_expert]
    i32 = lambda a: a.astype(jnp.int32)
    return (i32(slot), i32(block_expert), i32(n_used), i32(next_expert), i32(block_halves), i32(zero_start),
            n_blocks * MOE_ROWS)


def kernel(x, attn_norm_g, w_in, w_a2, b_a, gla_norm_g, b_glu, w_dw, b_dw, conv_ln_g, conv_ln_b, w_out,
           ffn_norm_g, w_router, b_router, w1, b1, w2, b2, final_norm_g):
    b, s, d = x.shape
    assert w_in.shape[0] == 1, "single-layer stack"
    t = b * s
    dv = gla_norm_g.shape[1]
    dk = w_a2.shape[2] // GLA_HEADS
    lowrank = w_a2.shape[1]
    dc = w_dw.shape[2]
    n_exp = w_router.shape[2]
    c_qkvr = 2 * GLA_HEADS * (dk + dv)
    assert lowrank <= LANES and n_exp <= LANES
    x2 = x.reshape(t, d)

    n1 = _rmsnorm(x2, attn_norm_g[0], BF16)
    w_main, w_low = _wprep(jnp.swapaxes(w_in[0], 0, 1), c_qkvr, lowrank)
    proj = _matmul_nt(n1, w_main, "inproj").reshape(b, s, -1)
    a_low = _matmul_nt(n1, w_low, "inproj_lowrank").reshape(b, s, LANES)

    wa = jnp.pad(w_a2[0], ((0, LANES - lowrank), (0, 0))).astype(BF16)
    gla_out = _gla(proj, a_low, wa, b_a[0].reshape(1, -1), gla_norm_g[0].reshape(1, dv), dk, dv)
    conv_out = _conv_module(proj, c_qkvr, dc, b_glu[0], w_dw[0], b_dw[0], conv_ln_g[0], conv_ln_b[0])
    h = _matmul([gla_out.reshape(t, -1), conv_out.reshape(t, dc)], w_out[0], d, "outproj", res=x2)

    wr = jnp.pad(w_router[0], ((0, 0), (0, LANES - n_exp))).astype(BF16)
    br = jnp.pad(b_router[0], (0, LANES - n_exp)).reshape(1, LANES)
    n2, idx_pad, gate_pad = _router(h, ffn_norm_g[0], wr, br, n_exp)
    slot, block_expert, n_used, next_expert, block_halves, zero_start, n_slots = _routing(idx_pad[:, :TOP_K], n_exp)
    xs = _dispatch(n2, slot, zero_start, n_used, n_slots)
    hid = _moe_up(xs, block_expert, n_used, next_expert, block_halves, w1[0], b1[0])
    y = _moe_down(hid, block_expert, n_used, next_expert, block_halves, w2[0], b2[0])
    out = _combine(y, slot, gate_pad, h, final_norm_g)
    return out.reshape(b, s, d)
```

```python
import functools

import jax
import jax.numpy as jnp
from jax import lax
from jax.experimental import pallas as pl
from jax.experimental.pallas import tpu as pltpu

GLA_HEADS = 4
GLA_TAU = 16.0
GLA_CHUNK = 64
TOP_K = 4
SWIGLU_ALPHA = 1.702
SWIGLU_LIMIT = 7.0
EPS = 1e-5

LANES = 128
SUBLANES = 8
MOE_ROWS = 512
MOE_HALF = MOE_ROWS // 2
CONV_HALO = 32
CAST_ROWS = 128
BULK_DMA_PRIORITY = 1
MIB = 1024 * 1024

F32 = jnp.float32
BF16 = jnp.bfloat16
U32 = jnp.uint32


def _params(semantics, vmem_mib):
    return pltpu.CompilerParams(dimension_semantics=semantics, vmem_limit_bytes=vmem_mib * MIB)


def _pack_halves(lo, hi):
    lo_u = lax.bitcast_convert_type(lo.astype(BF16).astype(F32), U32)
    hi_u = lax.bitcast_convert_type(hi.astype(BF16).astype(F32), U32)
    return (lo_u >> 16) | (hi_u & jnp.uint32(0xFFFF0000))


def _unpack_halves(p):
    lo = lax.bitcast_convert_type(p << 16, F32)
    hi = lax.bitcast_convert_type(p & jnp.uint32(0xFFFF0000), F32)
    return lo, hi


def _rmsnorm_kernel(x_ref, g_ref, o_ref):
    x = x_ref[...]
    y = x * lax.rsqrt(jnp.mean(x * x, axis=-1, keepdims=True) + EPS)
    o_ref[...] = (y * g_ref[...]).astype(o_ref.dtype)


def _rmsnorm(x, g, out_dtype, rows=256):
    t, d = x.shape
    rows = min(rows, t)
    return pl.pallas_call(
        _rmsnorm_kernel,
        grid=(t // rows,),
        in_specs=[pl.BlockSpec((rows, d), lambda i: (i, 0)),
                  pl.BlockSpec((1, d), lambda i: (0, 0))],
        out_specs=pl.BlockSpec((rows, d), lambda i: (i, 0)),
        out_shape=jax.ShapeDtypeStruct((t, d), out_dtype),
        compiler_params=_params(("parallel",), 32),
        name="rmsnorm",
    )(x, g.reshape(1, d))


def _wprep_kernel(a_ref, b_ref, o_ref, low_ref, prev, *, nb_main, lowrank):
    i = pl.program_id(0)
    rows = o_ref.shape[0]

    def straddle(first_ref):
        o_ref[0:rows - lowrank, :] = first_ref[lowrank:rows, :].astype(BF16)
        o_ref[rows - lowrank:rows, :] = b_ref[0:lowrank, :].astype(BF16)
        prev[...] = b_ref[...]

    @pl.when(i < nb_main)
    def _():
        o_ref[...] = a_ref[...].astype(BF16)

    @pl.when(i == nb_main)
    def _():
        low_ref[...] = jnp.zeros_like(low_ref)
        low_ref[0:lowrank, :] = a_ref[0:lowrank, :].astype(BF16)
        straddle(a_ref)

    @pl.when(i > nb_main)
    def _():
        straddle(prev)


def _wprep(wt, c_main, lowrank):
    n_in, k = wt.shape
    rows = 2 * LANES
    n_out = n_in - lowrank
    assert c_main % rows == 0 and n_out % rows == 0 and lowrank % 16 == 0
    last = pl.cdiv(n_in, rows) - 1
    nb_main = c_main // rows
    return pl.pallas_call(
        functools.partial(_wprep_kernel, nb_main=nb_main, lowrank=lowrank),
        grid=(n_out // rows,),
        in_specs=[pl.BlockSpec((rows, k), lambda i: (jnp.minimum(i, nb_main), 0)),
                  pl.BlockSpec((rows, k), lambda i: (jnp.minimum(jnp.maximum(i, nb_main) + 1, last), 0))],
        out_specs=[pl.BlockSpec((rows, k), lambda i: (i, 0)),
                   pl.BlockSpec((LANES, k), lambda i: (0, 0))],
        out_shape=[jax.ShapeDtypeStruct((n_out, k), BF16), jax.ShapeDtypeStruct((LANES, k), BF16)],
        scratch_shapes=[pltpu.VMEM((rows, k), F32)],
        compiler_params=_params(("arbitrary",), 32),
        name="inproj_weight_prep",
    )(wt, wt)


def _mm_nt_kernel(a_ref, wt_ref, o_ref):
    o_ref[...] = lax.dot_general(a_ref[...], wt_ref[...], (((1,), (1,)), ((), ())), preferred_element_type=F32)


def _matmul_nt(a, wt, name, tm=1024, tn=1024):
    m, k = a.shape
    n = wt.shape[0]
    tm, tn = min(tm, m), min(tn, n)
    return pl.pallas_call(
        _mm_nt_kernel,
        grid=(m // tm, n // tn),
        in_specs=[pl.BlockSpec((tm, k), lambda i, j: (i, 0)),
                  pl.BlockSpec((tn, k), lambda i, j: (j, 0))],
        out_specs=pl.BlockSpec((tm, tn), lambda i, j: (i, j)),
        out_shape=jax.ShapeDtypeStruct((m, n), F32),
        compiler_params=_params(("parallel", "arbitrary"), 48),
        name=name,
    )(a, wt)


def _mm_kernel(*refs, n_a, has_res):
    a_refs, w_ref = refs[:n_a], refs[n_a]
    res_ref = refs[n_a + 1] if has_res else None
    o_ref, wb = refs[-2], refs[-1]

    @pl.when(pl.program_id(1) == 0)
    def _():
        wb[...] = w_ref[...].astype(BF16)

    acc, k0 = None, 0
    for a_ref in a_refs:
        kk = a_ref.shape[1]
        part = jnp.dot(a_ref[...], wb[k0:k0 + kk, :], preferred_element_type=F32)
        acc = part if acc is None else acc + part
        k0 += kk
    if has_res:
        acc = acc + res_ref[...]
    o_ref[...] = acc.astype(o_ref.dtype)


def _matmul(a_list, w, n_out, name, res=None, tm=1024, tn=512):
    m = a_list[0].shape[0]
    k = w.shape[0]
    assert sum(a.shape[1] for a in a_list) == k
    tm, tn = min(tm, m), min(tn, n_out)
    in_specs = [pl.BlockSpec((tm, a.shape[1]), lambda j, i: (i, 0)) for a in a_list]
    in_specs.append(pl.BlockSpec((k, tn), lambda j, i: (0, j)))
    args = list(a_list) + [w]
    if res is not None:
        in_specs.append(pl.BlockSpec((tm, tn), lambda j, i: (i, j)))
        args.append(res)
    return pl.pallas_call(
        functools.partial(_mm_kernel, n_a=len(a_list), has_res=res is not None),
        grid=(n_out // tn, m // tm),
        in_specs=in_specs,
        out_specs=pl.BlockSpec((tm, tn), lambda j, i: (i, j)),
        out_shape=jax.ShapeDtypeStruct((m, n_out), F32),
        scratch_shapes=[pltpu.VMEM((k, tn), BF16)],
        compiler_params=_params(("parallel", "arbitrary"), 52),
        name=name,
    )(*args)


def _gla_kernel(q_ref, k_ref, v_ref, r_ref, al_ref, wa_ref, ba_ref, g_ref, o_ref, st_ref, *, chunk, scale):
    @pl.when(pl.program_id(2) == 0)
    def _():
        st_ref[...] = jnp.zeros_like(st_ref)

    ts = q_ref.shape[1]
    heads, dv, dk = st_ref.shape
    rows = lax.broadcasted_iota(jnp.int32, (ts, ts), 0)
    cols = lax.broadcasted_iota(jnp.int32, (ts, ts), 1)
    assert chunk & (chunk - 1) == 0
    same_chunk = (rows & -chunk) == (cols & -chunk)
    causal = jnp.logical_and(same_chunk, rows >= cols)
    tril = jnp.where(causal, 1.0, 0.0).astype(BF16)
    ones = jnp.where(same_chunk, 1.0, 0.0).astype(BF16)
    nt = (((1,), (1,)), ((), ()))
    tn = (((0,), (0,)), ((), ()))

    z = jnp.dot(al_ref[0].astype(BF16), wa_ref[...], preferred_element_type=F32) + ba_ref[...]
    la = (jnp.minimum(z, 0.0) - jnp.log1p(jnp.exp(-jnp.abs(z)))) * (1.0 / GLA_TAU)
    la_hi = la.astype(BF16)
    la_lo = (la - la_hi.astype(F32)).astype(BF16)
    bcum_all = jnp.dot(tril, la_hi, preferred_element_type=F32) + jnp.dot(tril, la_lo, preferred_element_type=F32)
    btot_all = jnp.dot(ones, la_hi, preferred_element_type=F32) + jnp.dot(ones, la_lo, preferred_element_type=F32)
    for i in range(heads):
        ks, vs = slice(i * dk, (i + 1) * dk), slice(i * dv, (i + 1) * dv)
        q, k, v = q_ref[0, :, ks], k_ref[0, :, ks], v_ref[0, :, vs]
        bcum, btot = bcum_all[:, ks], btot_all[:, ks]
        q_e = (q * jnp.exp(bcum) * scale).astype(BF16)
        k_e = (k * jnp.exp(-bcum)).astype(BF16)
        k_d = (k * jnp.exp(btot - bcum)).astype(BF16)
        decay = jnp.exp(btot)
        vb = v.astype(BF16)
        s = lax.dot_general(q_e, k_e, nt, preferred_element_type=F32)
        s = jnp.where(causal, s, 0.0).astype(BF16)
        o_intra = jnp.dot(s, vb, preferred_element_type=F32)

        st = st_ref[i]
        outs = []
        for c in range(ts // chunk):
            lo, hi = c * chunk, (c + 1) * chunk
            outs.append(o_intra[lo:hi]
                        + lax.dot_general(q_e[lo:hi], st.astype(BF16), nt, preferred_element_type=F32))
            st = st * decay[lo:lo + 1] + lax.dot_general(vb[lo:hi], k_d[lo:hi], tn, preferred_element_type=F32)
        st_ref[i] = st
        o = jnp.concatenate(outs, axis=0)
        o = o * lax.rsqrt(jnp.mean(o * o, axis=-1, keepdims=True) + EPS) * g_ref[...]
        r = r_ref[0, :, vs]
        o_ref[0, :, vs] = (o * (r * jax.nn.sigmoid(r))).astype(o_ref.dtype)


def _gla(proj, a_low, wa, ba, g, dk, dv, ts=256, heads_per_step=4):
    b, s, _ = proj.shape
    hp = heads_per_step
    hg = GLA_HEADS // hp
    ts = min(ts, s)
    wk, wv = hp * dk, hp * dv
    kq, kv = (GLA_HEADS * dk) // wk, (2 * GLA_HEADS * dk) // wv
    return pl.pallas_call(
        functools.partial(_gla_kernel, chunk=GLA_CHUNK, scale=dk ** -0.5),
        grid=(b, hg, s // ts),
        in_specs=[pl.BlockSpec((1, ts, wk), lambda bi, hi, si: (bi, si, hi)),
                  pl.BlockSpec((1, ts, wk), lambda bi, hi, si: (bi, si, kq + hi)),
                  pl.BlockSpec((1, ts, wv), lambda bi, hi, si: (bi, si, kv + hi)),
                  pl.BlockSpec((1, ts, wv), lambda bi, hi, si: (bi, si, kv + hg + hi)),
                  pl.BlockSpec((1, ts, LANES), lambda bi, hi, si: (bi, si, 0)),
                  pl.BlockSpec((LANES, wk), lambda bi, hi, si: (0, hi)),
                  pl.BlockSpec((1, wk), lambda bi, hi, si: (0, hi)),
                  pl.BlockSpec((1, dv), lambda bi, hi, si: (0, 0))],
        out_specs=pl.BlockSpec((1, ts, wv), lambda bi, hi, si: (bi, si, hi)),
        out_shape=jax.ShapeDtypeStruct((b, s, GLA_HEADS * dv), BF16),
        scratch_shapes=[pltpu.VMEM((hp, dv, dk), F32)],
        compiler_params=_params(("parallel", "parallel", "arbitrary"), 40),
        name="gla",
    )(proj, proj, proj, proj, a_low, wa, ba, g)


def _conv_kernel(a_ref, b_ref, bga_ref, bgb_ref, w_ref, bdw_ref, lg_ref, lb_ref, o_ref, ubuf, sbuf, cbuf, *,
                 width, cw, rc):
    ts, dc = a_ref.shape[1], a_ref.shape[2]
    si = pl.program_id(1)

    @pl.when(si == 0)
    def _():
        ubuf[0:CONV_HALO, :] = jnp.zeros((CONV_HALO, dc), F32)

    @pl.when(si > 0)
    def _():
        ubuf[0:CONV_HALO, :] = ubuf[ts:ts + CONV_HALO, :]

    ubuf[CONV_HALO:CONV_HALO + ts, :] = (a_ref[0] + bga_ref[...]) * jax.nn.sigmoid(b_ref[0] + bgb_ref[...])

    span = ts + CONV_HALO - SUBLANES

    def col_body(cb, carry):
        cs = pl.ds(pl.multiple_of(cb * cw, cw), cw)
        for b in range(1, SUBLANES):
            sbuf[b, 0:span, :] = ubuf[b:b + span, cs]
        for rb in range(ts // rc):
            acc = jnp.broadcast_to(bdw_ref[:, cs], (rc, cw))
            for j in range(width):
                off = CONV_HALO - (width - 1) + j
                b = off % SUBLANES
                r0 = off - b + rb * rc
                src = ubuf[r0:r0 + rc, cs] if b == 0 else sbuf[b, r0:r0 + rc, :]
                acc = acc + w_ref[j:j + 1, cs] * src
            cbuf[rb * rc:(rb + 1) * rc, cs] = acc
        return carry

    lax.fori_loop(0, dc // cw, col_body, 0)
    c = cbuf[...]
    mu = jnp.mean(c, axis=-1, keepdims=True)
    cen = c - mu
    var = jnp.mean(cen * cen, axis=-1, keepdims=True)
    un = cen * lax.rsqrt(var + EPS) * lg_ref[...] + lb_ref[...]
    o_ref[0] = (un * jax.nn.sigmoid(un)).astype(o_ref.dtype)


def _conv_module(proj, col0, dc, b_glu, w_dw, b_dw, ln_g, ln_b, ts=256):
    b, s, _ = proj.shape
    ts = min(ts, s)
    width = w_dw.shape[0]
    cw = min(128, dc)
    assert width - 1 <= CONV_HALO <= ts and col0 % dc == 0
    cblk = col0 // dc
    row = lambda v: v.reshape(1, dc)
    vec = pl.BlockSpec((1, dc), lambda bi, si: (0, 0))
    return pl.pallas_call(
        functools.partial(_conv_kernel, width=width, cw=cw, rc=min(128, ts)),
        grid=(b, s // ts),
        in_specs=[pl.BlockSpec((1, ts, dc), lambda bi, si: (bi, si, cblk)),
                  pl.BlockSpec((1, ts, dc), lambda bi, si: (bi, si, cblk + 1)),
                  vec, vec,
                  pl.BlockSpec((width, dc), lambda bi, si: (0, 0)),
                  vec, vec, vec],
        out_specs=pl.BlockSpec((1, ts, dc), lambda bi, si: (bi, si, 0)),
        out_shape=jax.ShapeDtypeStruct((b, s, dc), BF16),
        scratch_shapes=[pltpu.VMEM((CONV_HALO + ts, dc), F32),
                        pltpu.VMEM((SUBLANES, CONV_HALO + ts, cw), F32),
                        pltpu.VMEM((ts, dc), F32)],
        compiler_params=_params(("parallel", "arbitrary"), 32),
        name="conv_module",
    )(proj, proj, row(b_glu[:dc]), row(b_glu[dc:]), w_dw, row(b_dw), row(ln_g), row(ln_b))


def _router_kernel(h_ref, g_ref, wr_ref, br_ref, n_ref, idx_ref, gate_ref, *, n_exp):
    h = h_ref[...]
    d2 = h.shape[1] // 2
    n = h * lax.rsqrt(jnp.mean(h * h, axis=-1, keepdims=True) + EPS) * g_ref[...]
    n_ref[...] = _pack_halves(n[:, :d2], n[:, d2:])
    logits = jnp.dot(n.astype(BF16), wr_ref[...], preferred_element_type=F32) + br_ref[...]
    lane = lax.broadcasted_iota(jnp.int32, logits.shape, 1)
    lane_f = lane.astype(F32)
    neg = jnp.float32(-jnp.inf)
    l = jnp.where(lane < n_exp, logits, neg)
    vals, idxs = [], []
    for _ in range(TOP_K):
        m = jnp.max(l, axis=-1, keepdims=True)
        i = jnp.min(jnp.where(l == m, lane_f, float(LANES)), axis=-1, keepdims=True)
        vals.append(m)
        idxs.append(i)
        l = jnp.where(lane_f == i, neg, l)
    es = [jnp.exp(v - vals[0]) for v in vals]
    tot = es[0]
    for e in es[1:]:
        tot = tot + e
    idx_out = jnp.zeros(logits.shape, F32)
    gate_out = jnp.zeros(logits.shape, F32)
    for kk in range(TOP_K):
        idx_out = jnp.where(lane == kk, idxs[kk], idx_out)
        gate_out = jnp.where(lane == kk, es[kk] / tot, gate_out)
    idx_ref[...] = idx_out.astype(jnp.int32)
    gate_ref[...] = gate_out


def _router(h, g, wr, br, n_exp, rows=256):
    t, d = h.shape
    rows = min(rows, t)
    blk = lambda w: pl.BlockSpec((rows, w), lambda i: (i, 0))
    return pl.pallas_call(
        functools.partial(_router_kernel, n_exp=n_exp),
        grid=(t // rows,),
        in_specs=[blk(d),
                  pl.BlockSpec((1, d), lambda i: (0, 0)),
                  pl.BlockSpec((d, LANES), lambda i: (0, 0)),
                  pl.BlockSpec((1, LANES), lambda i: (0, 0))],
        out_specs=[blk(d // 2), blk(LANES), blk(LANES)],
        out_shape=[jax.ShapeDtypeStruct((t, d // 2), U32),
                   jax.ShapeDtypeStruct((t, LANES), jnp.int32),
                   jax.ShapeDtypeStruct((t, LANES), F32)],
        compiler_params=_params(("parallel",), 32),
        name="ffn_norm_router",
    )(h, g.reshape(1, d), wr, br)


def _dispatch_kernel(zs_ref, nu_ref, slot_ref, n_ref, xs_hbm, zbuf, zsem, sem, *, rows, n_zero):
    def zero_copy(row):
        return pltpu.make_async_copy(zbuf, xs_hbm.at[pl.ds(pl.multiple_of(row, MOE_HALF), MOE_HALF)], zsem)

    @pl.when(pl.program_id(0) == 0)
    def _():
        zbuf[...] = jnp.zeros_like(zbuf)
        n_halves = xs_hbm.shape[0] // MOE_HALF
        first_unused = nu_ref[0] * (MOE_ROWS // MOE_HALF)
        for z in range(n_zero):
            @pl.when(zs_ref[z] >= 0)
            def _():
                zero_copy(zs_ref[z]).start()

        def tail_start(hi, carry):
            zero_copy(hi * MOE_HALF).start()
            return carry

        def tail_wait(hi, carry):
            zero_copy(hi * MOE_HALF).wait()
            return carry

        lax.fori_loop(first_unused, n_halves, tail_start, 0)
        for z in range(n_zero):
            @pl.when(zs_ref[z] >= 0)
            def _():
                zero_copy(zs_ref[z]).wait()
        lax.fori_loop(first_unused, n_halves, tail_wait, 0)

    def row_copy(r, slot):
        return pltpu.make_async_copy(n_ref.at[pl.ds(r, 1)], xs_hbm.at[pl.ds(slot, 1)], sem)

    def start(r, carry):
        for kk in range(TOP_K):
            row_copy(r, slot_ref[0, 0, r * TOP_K + kk]).start(priority=kk % 2)
        return carry

    lax.fori_loop(0, rows, start, 0)
    for kk in range(TOP_K):
        pltpu.make_async_copy(n_ref, xs_hbm.at[pl.ds(0, rows)], sem).wait()


def _dispatch(n_packed, slot, zero_start, n_used, n_slots, rows=256):
    t, d2 = n_packed.shape
    rows = min(rows, t)
    grid_spec = pltpu.PrefetchScalarGridSpec(
        num_scalar_prefetch=2,
        grid=(t // rows,),
        in_specs=[pl.BlockSpec((1, 1, rows * TOP_K), lambda i, zs, nu: (i, 0, 0), memory_space=pltpu.SMEM),
                  pl.BlockSpec((rows, d2), lambda i, zs, nu: (i, 0))],
        out_specs=pl.BlockSpec(memory_space=pl.ANY),
        scratch_shapes=[pltpu.VMEM((MOE_HALF, d2), U32), pltpu.SemaphoreType.DMA(()), pltpu.SemaphoreType.DMA(())],
    )
    return pl.pallas_call(
        functools.partial(_dispatch_kernel, rows=rows, n_zero=zero_start.shape[0]),
        grid_spec=grid_spec,
        out_shape=jax.ShapeDtypeStruct((n_slots, d2), U32),
        compiler_params=_params(("arbitrary",), 32),
        name="moe_dispatch",
    )(zero_start, n_used, slot.reshape(t // rows, 1, rows * TOP_K), n_packed)


def _block_state(be_ref, nu_ref, bi):
    used = bi < nu_ref[0]
    fresh = jnp.logical_or(bi == 0, be_ref[bi] != be_ref[jnp.maximum(bi - 1, 0)])
    return used, jnp.logical_and(used, fresh)


def _run_weights(copies, first, fresh, cast, then):
    first_cond, first_args = first

    @pl.when(first_cond)
    def _():
        for c in copies(*first_args):
            c.start(priority=BULK_DMA_PRIORITY)

    @pl.when(fresh)
    def _():
        for c in copies(*first_args):
            c.wait()
        cast()
        for cond, args in then:
            @pl.when(cond)
            def _():
                for c in copies(*args):
                    c.start(priority=BULK_DMA_PRIORITY)


def _cast_rows(src, dst, rows):
    def body(i, carry):
        r = pl.ds(pl.multiple_of(i * rows, rows), rows)
        dst[r, :] = src[r, :].astype(BF16)
        return carry
    lax.fori_loop(0, src.shape[0] // rows, body, 0)


def _per_block_halves(halves, compute, o_ref):
    @pl.when(halves == 2)
    def _():
        compute(MOE_ROWS)

    @pl.when(halves == 1)
    def _():
        compute(MOE_HALF)
        o_ref[MOE_HALF:MOE_ROWS, :] = jnp.zeros((MOE_ROWS - MOE_HALF, o_ref.shape[1]), o_ref.dtype)

    @pl.when(halves == 0)
    def _():
        o_ref[...] = jnp.zeros_like(o_ref)


def _moe_up_kernel(be_ref, nu_ref, nx_ref, bh_ref, x_ref, bg_ref, bl_ref, w_hbm, o_ref, sg, sl, wgb, wlb, sem, *,
                   nf):
    fi, bi = pl.program_id(0), pl.program_id(1)
    _, fresh = _block_state(be_ref, nu_ref, bi)
    tf = sg.shape[1]
    f_all = w_hbm.shape[2] // 2

    def copies(e, f):
        c0 = pl.multiple_of(f * tf, tf)
        return (pltpu.make_async_copy(w_hbm.at[e, :, pl.ds(c0, tf)], sg, sem.at[0]),
                pltpu.make_async_copy(w_hbm.at[e, :, pl.ds(f_all + c0, tf)], sl, sem.at[1]))

    def cast():
        _cast_rows(sg, wgb, CAST_ROWS)
        _cast_rows(sl, wlb, CAST_ROWS)

    nx = nx_ref[bi]
    _run_weights(copies, (jnp.logical_and(fi == 0, bi == 0), (be_ref[0], 0)), fresh, cast,
                 [(nx >= 0, (nx, fi)),
                  (jnp.logical_and(nx < 0, fi + 1 < nf), (be_ref[0], fi + 1))])

    def compute(rows):
        d2 = x_ref.shape[1]
        lo, hi = _unpack_halves(x_ref[0:rows, :])
        lo, hi = lo.astype(BF16), hi.astype(BF16)

        def proj(wb, b_ref):
            return (jnp.dot(lo, wb[0:d2, :], preferred_element_type=F32)
                    + jnp.dot(hi, wb[d2:2 * d2, :], preferred_element_type=F32) + b_ref[0])

        xg = jnp.minimum(proj(wgb, bg_ref), SWIGLU_LIMIT)
        xl = jnp.clip(proj(wlb, bl_ref), -SWIGLU_LIMIT, SWIGLU_LIMIT)
        o_ref[0:rows, :] = (xg * jax.nn.sigmoid(SWIGLU_ALPHA * xg) * (xl + 1.0)).astype(o_ref.dtype)

    _per_block_halves(bh_ref[bi], compute, o_ref)


def _moe_up(xs, block_expert, n_used, next_expert, block_halves, w1, b1, tf=512):
    n_slots, d2 = xs.shape
    e, d, f2 = w1.shape
    f = f2 // 2
    tf = min(tf, f)
    nf = f // tf
    nb = n_slots // MOE_ROWS
    blk = lambda bi, nu: jnp.minimum(bi, nu[0] - 1)
    grid_spec = pltpu.PrefetchScalarGridSpec(
        num_scalar_prefetch=4,
        grid=(nf, nb),
        in_specs=[pl.BlockSpec((MOE_ROWS, d2), lambda fi, bi, be, nu, nx, bh: (blk(bi, nu), 0)),
                  pl.BlockSpec((1, 1, tf), lambda fi, bi, be, nu, nx, bh: (be[blk(bi, nu)], 0, fi)),
                  pl.BlockSpec((1, 1, tf), lambda fi, bi, be, nu, nx, bh: (be[blk(bi, nu)], 0, nf + fi)),
                  pl.BlockSpec(memory_space=pl.ANY)],
        out_specs=pl.BlockSpec((MOE_ROWS, tf), lambda fi, bi, be, nu, nx, bh: (bi, fi)),
        scratch_shapes=[pltpu.VMEM((d, tf), F32), pltpu.VMEM((d, tf), F32),
                        pltpu.VMEM((d, tf), BF16), pltpu.VMEM((d, tf), BF16),
                        pltpu.SemaphoreType.DMA((2,))],
    )
    return pl.pallas_call(
        functools.partial(_moe_up_kernel, nf=nf),
        grid_spec=grid_spec,
        out_shape=jax.ShapeDtypeStruct((n_slots, f), BF16),
        compiler_params=_params(("arbitrary", "arbitrary"), 48),
        name="moe_up",
    )(block_expert, n_used, next_expert, block_halves, xs, b1.reshape(e, 1, f2), b1.reshape(e, 1, f2), w1)


def _moe_down_kernel(be_ref, nu_ref, nx_ref, bh_ref, h_ref, b_ref, w_hbm, o_ref, sw, wb, sem):
    bi = pl.program_id(0)
    _, fresh = _block_state(be_ref, nu_ref, bi)

    def copies(e):
        return (pltpu.make_async_copy(w_hbm.at[e], sw, sem.at[0]),)

    def cast():
        wb[...] = sw[...].astype(BF16)

    nx = nx_ref[bi]
    _run_weights(copies, (bi == 0, (be_ref[0],)), fresh, cast, [(nx >= 0, (nx,))])

    def compute(rows):
        d2 = o_ref.shape[1]
        h = h_ref[0:rows, :]
        y_lo = jnp.dot(h, wb[:, 0:d2], preferred_element_type=F32) + b_ref[0, :, 0:d2]
        y_hi = jnp.dot(h, wb[:, d2:2 * d2], preferred_element_type=F32) + b_ref[0, :, d2:2 * d2]
        o_ref[0:rows, :] = _pack_halves(y_lo, y_hi)

    _per_block_halves(bh_ref[bi], compute, o_ref)


def _moe_down(hid, block_expert, n_used, next_expert, block_halves, w2, b2):
    n_slots, f = hid.shape
    e, _, d = w2.shape
    nb = n_slots // MOE_ROWS
    blk = lambda bi, nu: jnp.minimum(bi, nu[0] - 1)
    grid_spec = pltpu.PrefetchScalarGridSpec(
        num_scalar_prefetch=4,
        grid=(nb,),
        in_specs=[pl.BlockSpec((MOE_ROWS, f), lambda bi, be, nu, nx, bh: (blk(bi, nu), 0)),
                  pl.BlockSpec((1, 1, d), lambda bi, be, nu, nx, bh: (be[blk(bi, nu)], 0, 0)),
                  pl.BlockSpec(memory_space=pl.ANY)],
        out_specs=pl.BlockSpec((MOE_ROWS, d // 2), lambda bi, be, nu, nx, bh: (bi, 0)),
        scratch_shapes=[pltpu.VMEM((f, d), F32), pltpu.VMEM((f, d), BF16), pltpu.SemaphoreType.DMA((1,))],
    )
    return pl.pallas_call(
        _moe_down_kernel,
        grid_spec=grid_spec,
        out_shape=jax.ShapeDtypeStruct((n_slots, d // 2), U32),
        compiler_params=_params(("arbitrary",), 48),
        name="moe_down",
    )(block_expert, n_used, next_expert, block_halves, hid, b2.reshape(e, 1, d), w2)


def _combine_kernel(slot_ref, nslot_ref, gate_ref, h_ref, g_ref, y_hbm, o_ref, buf_a, buf_b, sem, *, rows):
    i = pl.program_id(0)
    d2 = buf_a.shape[2]
    group = SUBLANES

    def start_rows(s_ref, base, buf, sem_i, r0):
        for r in range(r0, r0 + group):
            for kk in range(TOP_K):
                pltpu.make_async_copy(y_hbm.at[pl.ds(s_ref[0, 0, (base + r) * TOP_K + kk], 1)],
                                      buf.at[kk, pl.ds(r, 1)], sem.at[sem_i]).start(priority=kk % 2)

    def wait_all(buf, sem_i):
        for kk in range(TOP_K):
            pltpu.make_async_copy(y_hbm.at[pl.ds(0, rows)], buf.at[kk], sem.at[sem_i]).wait()

    def compute_rows(buf, base, r0):
        tok = slice(base + r0, base + r0 + group)
        gates = gate_ref[tok, :]
        acc_lo, acc_hi = h_ref[tok, 0:d2], h_ref[tok, d2:2 * d2]
        for kk in range(TOP_K):
            lo, hi = _unpack_halves(buf[kk, r0:r0 + group, :])
            gk = gates[:, kk:kk + 1]
            acc_lo = acc_lo + gk * lo
            acc_hi = acc_hi + gk * hi
        ms = (jnp.sum(acc_lo * acc_lo, axis=-1, keepdims=True)
              + jnp.sum(acc_hi * acc_hi, axis=-1, keepdims=True)) * (1.0 / (2 * d2))
        inv = lax.rsqrt(ms + EPS)
        o_ref[tok, 0:d2] = acc_lo * inv * g_ref[:, 0:d2]
        o_ref[tok, d2:2 * d2] = acc_hi * inv * g_ref[:, d2:2 * d2]

    @pl.when(i == 0)
    def _():
        def body(r, carry):
            for kk in range(TOP_K):
                pltpu.make_async_copy(y_hbm.at[pl.ds(slot_ref[0, 0, r * TOP_K + kk], 1)],
                                      buf_a.at[kk, pl.ds(r, 1)], sem.at[0]).start(priority=kk % 2)
            return carry
        lax.fori_loop(0, rows, body, 0)

    wait_all(buf_a, 0)
    for r0 in range(0, rows, group):
        start_rows(slot_ref, rows, buf_b, 1, r0)
        compute_rows(buf_a, 0, r0)
    wait_all(buf_b, 1)
    for r0 in range(0, rows, group):
        start_rows(nslot_ref, 0, buf_a, 0, r0)
        compute_rows(buf_b, rows, r0)

    @pl.when(i == pl.num_programs(0) - 1)
    def _():
        wait_all(buf_a, 0)


def _combine(y_packed, slot, gates, h, g, rows=128):
    t, d = h.shape
    rows = min(rows, t // 2)
    nsteps = t // (2 * rows)
    slot3 = slot.reshape(nsteps, 1, 2 * rows * TOP_K)
    return pl.pallas_call(
        functools.partial(_combine_kernel, rows=rows),
        grid=(nsteps,),
        in_specs=[pl.BlockSpec((1, 1, 2 * rows * TOP_K), lambda i: (i, 0, 0), memory_space=pltpu.SMEM),
                  pl.BlockSpec((1, 1, 2 * rows * TOP_K), lambda i: (jnp.minimum(i + 1, nsteps - 1), 0, 0),
                               memory_space=pltpu.SMEM),
                  pl.BlockSpec((2 * rows, LANES), lambda i: (i, 0)),
                  pl.BlockSpec((2 * rows, d), lambda i: (i, 0)),
                  pl.BlockSpec((1, d), lambda i: (0, 0)),
                  pl.BlockSpec(memory_space=pl.ANY)],
        out_specs=pl.BlockSpec((2 * rows, d), lambda i: (i, 0)),
        out_shape=jax.ShapeDtypeStruct((t, d), F32),
        scratch_shapes=[pltpu.VMEM((TOP_K, rows, d // 2), U32), pltpu.VMEM((TOP_K, rows, d // 2), U32),
                        pltpu.SemaphoreType.DMA((2,))],
        compiler_params=_params(("arbitrary",), 40),
        name="moe_combine",
    )(slot3, slot3, gates, h, g.reshape(1, d), y_packed)


def _routing(top_idx, n_exp):
    t = top_idx.shape[0]
    experts = jnp.arange(n_exp, dtype=jnp.int32)
    onehot = (top_idx[:, :, None] == experts[None, None, :]).any(axis=1).astype(jnp.int32)
    incl = jnp.cumsum(onehot, axis=0)
    counts = incl[-1]
    padded = (counts + MOE_ROWS - 1) // MOE_ROWS * MOE_ROWS
    pad_ends = jnp.cumsum(padded)
    slot = jnp.take_along_axis(incl - onehot + (pad_ends - padded)[None, :], top_idx, axis=1)
    n_blocks = -(-(t * TOP_K) // MOE_ROWS) + n_exp
    block_rows = jnp.arange(n_blocks, dtype=jnp.int32) * MOE_ROWS
    block_expert = jnp.minimum(jnp.sum(pad_ends[None, :] <= block_rows[:, None], axis=1), n_exp - 1)
    n_used = (pad_ends[-1] // MOE_ROWS).reshape(1)
    rows_in_block = jnp.clip((counts - padded + pad_ends)[block_expert] - block_rows, 0, MOE_ROWS)
    block_halves = jnp.where(block_rows < pad_ends[-1], jnp.where(rows_in_block <= MOE_HALF, 1, 2), 0)
    last_rows = counts - padded + MOE_ROWS
    zero_start = jnp.concatenate([
        jnp.where(jnp.logical_and(counts > 0, last_rows < MOE_ROWS), pad_ends - MOE_HALF, -1),
        jnp.where(jnp.logical_and(counts > 0, last_rows < MOE_HALF), pad_ends - MOE_ROWS, -1)])
    later = jnp.logical_and(experts[None, :] > experts[:, None], padded[None, :] > 0)
    next_of = jnp.min(jnp.where(later, experts[None, :], n_exp), axis=1)
    next_expert = jnp.where(next_of < n_exp, next_of, -1)[block_expert]
    i32 = lambda a: a.astype(jnp.int32)
    return (i32(slot), i32(block_expert), i32(n_used), i32(next_expert), i32(block_halves), i32(zero_start),
            n_blocks * MOE_ROWS)


def kernel(x, attn_norm_g, w_in, w_a2, b_a, gla_norm_g, b_glu, w_dw, b_dw, conv_ln_g, conv_ln_b, w_out,
           ffn_norm_g, w_router, b_router, w1, b1, w2, b2, final_norm_g):
    b, s, d = x.shape
    assert w_in.shape[0] == 1, "single-layer stack"
    t = b * s
    dv = gla_norm_g.shape[1]
    dk = w_a2.shape[2] // GLA_HEADS
    lowrank = w_a2.shape[1]
    dc = w_dw.shape[2]
    n_exp = w_router.shape[2]
    c_qkvr = 2 * GLA_HEADS * (dk + dv)
    assert lowrank <= LANES and n_exp <= LANES
    x2 = x.reshape(t, d)

    n1 = _rmsnorm(x2, attn_norm_g[0], BF16)
    w_main, w_low = _wprep(jnp.swapaxes(w_in[0], 0, 1), c_qkvr, lowrank)
    proj = _matmul_nt(n1, w_main, "inproj").reshape(b, s, -1)
    a_low = _matmul_nt(n1, w_low, "inproj_lowrank").reshape(b, s, LANES)

    wa = jnp.pad(w_a2[0], ((0, LANES - lowrank), (0, 0))).astype(BF16)
    gla_out = _gla(proj, a_low, wa, b_a[0].reshape(1, -1), gla_norm_g[0].reshape(1, dv), dk, dv)
    conv_out = _conv_module(proj, c_qkvr, dc, b_glu[0], w_dw[0], b_dw[0], conv_ln_g[0], conv_ln_b[0])
    h = _matmul([gla_out.reshape(t, -1), conv_out.reshape(t, dc)], w_out[0], d, "outproj", res=x2)

    wr = jnp.pad(w_router[0], ((0, 0), (0, LANES - n_exp))).astype(BF16)
    br = jnp.pad(b_router[0], (0, LANES - n_exp)).reshape(1, LANES)
    n2, idx_pad, gate_pad = _router(h, ffn_norm_g[0], wr, br, n_exp)
    slot, block_expert, n_used, next_expert, block_halves, zero_start, n_slots = _routing(idx_pad[:, :TOP_K], n_exp)
    xs = _dispatch(n2, slot, zero_start, n_used, n_slots)
    hid = _moe_up(xs, block_expert, n_used, next_expert, block_halves, w1[0], b1[0])
    y = _moe_down(hid, block_expert, n_used, next_expert, block_halves, w2[0], b2[0])
    out = _combine(y, slot, gate_pad, h, final_norm_g)
    return out.reshape(b, s, d)
```

```python
import functools

import jax
import jax.numpy as jnp
from jax import lax
from jax.experimental import pallas as pl
from jax.experimental.pallas import tpu as pltpu

GLA_HEADS = 4
GLA_TAU = 16.0
GLA_CHUNK = 64
TOP_K = 4
SWIGLU_ALPHA = 1.702
SWIGLU_LIMIT = 7.0
EPS = 1e-5

LANES = 128
SUBLANES = 8
MOE_ROWS = 512
MOE_HALF = MOE_ROWS // 2
CONV_HALO = 32
COMBINE_PHASES = 4
COMBINE_AHEAD = 2
CAST_ROWS = 128
BULK_DMA_PRIORITY = 1
MIB = 1024 * 1024

F32 = jnp.float32
BF16 = jnp.bfloat16
U32 = jnp.uint32


def _params(semantics, vmem_mib):
    return pltpu.CompilerParams(dimension_semantics=semantics, vmem_limit_bytes=vmem_mib * MIB)


def _pack_halves(lo, hi):
    lo_u = lax.bitcast_convert_type(lo.astype(BF16).astype(F32), U32)
    hi_u = lax.bitcast_convert_type(hi.astype(BF16).astype(F32), U32)
    return (lo_u >> 16) | (hi_u & jnp.uint32(0xFFFF0000))


def _unpack_halves(p):
    lo = lax.bitcast_convert_type(p << 16, F32)
    hi = lax.bitcast_convert_type(p & jnp.uint32(0xFFFF0000), F32)
    return lo, hi


def _rmsnorm_kernel(x_ref, g_ref, o_ref):
    x = x_ref[...]
    y = x * lax.rsqrt(jnp.mean(x * x, axis=-1, keepdims=True) + EPS)
    o_ref[...] = (y * g_ref[...]).astype(o_ref.dtype)


def _rmsnorm(x, g, out_dtype, rows=256):
    t, d = x.shape
    rows = min(rows, t)
    return pl.pallas_call(
        _rmsnorm_kernel,
        grid=(t // rows,),
        in_specs=[pl.BlockSpec((rows, d), lambda i: (i, 0)),
                  pl.BlockSpec((1, d), lambda i: (0, 0))],
        out_specs=pl.BlockSpec((rows, d), lambda i: (i, 0)),
        out_shape=jax.ShapeDtypeStruct((t, d), out_dtype),
        compiler_params=_params(("parallel",), 32),
        name="rmsnorm",
    )(x, g.reshape(1, d))


def _wprep_kernel(a_ref, b_ref, o_ref, low_ref, prev, *, nb_main, lowrank):
    i = pl.program_id(0)
    rows = o_ref.shape[0]

    def straddle(first_ref):
        o_ref[0:rows - lowrank, :] = first_ref[lowrank:rows, :].astype(BF16)
        o_ref[rows - lowrank:rows, :] = b_ref[0:lowrank, :].astype(BF16)
        prev[...] = b_ref[...]

    @pl.when(i < nb_main)
    def _():
        o_ref[...] = a_ref[...].astype(BF16)

    @pl.when(i == nb_main)
    def _():
        low_ref[...] = jnp.zeros_like(low_ref)
        low_ref[0:lowrank, :] = a_ref[0:lowrank, :].astype(BF16)
        straddle(a_ref)

    @pl.when(i > nb_main)
    def _():
        straddle(prev)


def _wprep(wt, c_main, lowrank):
    n_in, k = wt.shape
    rows = 2 * LANES
    n_out = n_in - lowrank
    assert c_main % rows == 0 and n_out % rows == 0 and lowrank % 16 == 0
    last = pl.cdiv(n_in, rows) - 1
    nb_main = c_main // rows
    return pl.pallas_call(
        functools.partial(_wprep_kernel, nb_main=nb_main, lowrank=lowrank),
        grid=(n_out // rows,),
        in_specs=[pl.BlockSpec((rows, k), lambda i: (jnp.minimum(i, nb_main), 0)),
                  pl.BlockSpec((rows, k), lambda i: (jnp.minimum(jnp.maximum(i, nb_main) + 1, last), 0))],
        out_specs=[pl.BlockSpec((rows, k), lambda i: (i, 0)),
                   pl.BlockSpec((LANES, k), lambda i: (0, 0))],
        out_shape=[jax.ShapeDtypeStruct((n_out, k), BF16), jax.ShapeDtypeStruct((LANES, k), BF16)],
        scratch_shapes=[pltpu.VMEM((rows, k), F32)],
        compiler_params=_params(("arbitrary",), 32),
        name="inproj_weight_prep",
    )(wt, wt)


def _mm_nt_kernel(a_ref, wt_ref, o_ref):
    o_ref[...] = lax.dot_general(a_ref[...], wt_ref[...], (((1,), (1,)), ((), ())), preferred_element_type=F32)


def _matmul_nt(a, wt, name, tm=1024, tn=1024):
    m, k = a.shape
    n = wt.shape[0]
    tm, tn = min(tm, m), min(tn, n)
    return pl.pallas_call(
        _mm_nt_kernel,
        grid=(m // tm, n // tn),
        in_specs=[pl.BlockSpec((tm, k), lambda i, j: (i, 0)),
                  pl.BlockSpec((tn, k), lambda i, j: (j, 0))],
        out_specs=pl.BlockSpec((tm, tn), lambda i, j: (i, j)),
        out_shape=jax.ShapeDtypeStruct((m, n), F32),
        compiler_params=_params(("parallel", "arbitrary"), 48),
        name=name,
    )(a, wt)


def _mm_kernel(*refs, n_a, has_res):
    a_refs, w_ref = refs[:n_a], refs[n_a]
    res_ref = refs[n_a + 1] if has_res else None
    o_ref, wb = refs[-2], refs[-1]

    @pl.when(pl.program_id(1) == 0)
    def _():
        _cast_rows(w_ref, wb, CAST_ROWS)

    acc, k0 = None, 0
    for a_ref in a_refs:
        kk = a_ref.shape[1]
        part = jnp.dot(a_ref[...], wb[k0:k0 + kk, :], preferred_element_type=F32)
        acc = part if acc is None else acc + part
        k0 += kk
    if has_res:
        acc = acc + res_ref[...]
    o_ref[...] = acc.astype(o_ref.dtype)


def _matmul(a_list, w, n_out, name, res=None, tm=1024, tn=512):
    m = a_list[0].shape[0]
    k = w.shape[0]
    assert sum(a.shape[1] for a in a_list) == k
    tm, tn = min(tm, m), min(tn, n_out)
    in_specs = [pl.BlockSpec((tm, a.shape[1]), lambda j, i: (i, 0)) for a in a_list]
    in_specs.append(pl.BlockSpec((k, tn), lambda j, i: (0, j)))
    args = list(a_list) + [w]
    if res is not None:
        in_specs.append(pl.BlockSpec((tm, tn), lambda j, i: (i, j)))
        args.append(res)
    return pl.pallas_call(
        functools.partial(_mm_kernel, n_a=len(a_list), has_res=res is not None),
        grid=(n_out // tn, m // tm),
        in_specs=in_specs,
        out_specs=pl.BlockSpec((tm, tn), lambda j, i: (i, j)),
        out_shape=jax.ShapeDtypeStruct((m, n_out), F32),
        scratch_shapes=[pltpu.VMEM((k, tn), BF16)],
        compiler_params=_params(("parallel", "arbitrary"), 52),
        name=name,
    )(*args)


def _gla_kernel(q_ref, k_ref, v_ref, r_ref, al_ref, wa_ref, ba_ref, g_ref, o_ref, st_ref, *, chunk, scale):
    @pl.when(pl.program_id(2) == 0)
    def _():
        st_ref[...] = jnp.zeros_like(st_ref)

    ts = q_ref.shape[1]
    heads, dv, dk = st_ref.shape
    rows = lax.broadcasted_iota(jnp.int32, (ts, ts), 0)
    cols = lax.broadcasted_iota(jnp.int32, (ts, ts), 1)
    assert chunk & (chunk - 1) == 0
    same_chunk = (rows & -chunk) == (cols & -chunk)
    causal = jnp.logical_and(same_chunk, rows >= cols)
    tril = jnp.where(causal, 1.0, 0.0).astype(BF16)
    ones = jnp.where(same_chunk, 1.0, 0.0).astype(BF16)
    nt = (((1,), (1,)), ((), ()))
    tn = (((0,), (0,)), ((), ()))

    z = jnp.dot(al_ref[0].astype(BF16), wa_ref[...], preferred_element_type=F32) + ba_ref[...]
    la = (jnp.minimum(z, 0.0) - jnp.log1p(jnp.exp(-jnp.abs(z)))) * (1.0 / GLA_TAU)
    la_hi = la.astype(BF16)
    la_lo = (la - la_hi.astype(F32)).astype(BF16)
    bcum_all = jnp.dot(tril, la_hi, preferred_element_type=F32) + jnp.dot(tril, la_lo, preferred_element_type=F32)
    btot_all = jnp.dot(ones, la_hi, preferred_element_type=F32) + jnp.dot(ones, la_lo, preferred_element_type=F32)
    for i in range(heads):
        ks, vs = slice(i * dk, (i + 1) * dk), slice(i * dv, (i + 1) * dv)
        q, k, v = q_ref[0, :, ks], k_ref[0, :, ks], v_ref[0, :, vs]
        bcum, btot = bcum_all[:, ks], btot_all[:, ks]
        q_e = (q * jnp.exp(bcum) * scale).astype(BF16)
        k_e = (k * jnp.exp(-bcum)).astype(BF16)
        k_d = (k * jnp.exp(btot - bcum)).astype(BF16)
        decay = jnp.exp(btot)
        vb = v.astype(BF16)
        s = lax.dot_general(q_e, k_e, nt, preferred_element_type=F32)
        s = jnp.where(causal, s, 0.0).astype(BF16)
        o_intra = jnp.dot(s, vb, preferred_element_type=F32)

        st = st_ref[i]
        outs = []
        for c in range(ts // chunk):
            lo, hi = c * chunk, (c + 1) * chunk
            outs.append(o_intra[lo:hi]
                        + lax.dot_general(q_e[lo:hi], st.astype(BF16), nt, preferred_element_type=F32))
            st = st * decay[lo:lo + 1] + lax.dot_general(vb[lo:hi], k_d[lo:hi], tn, preferred_element_type=F32)
        st_ref[i] = st
        o = jnp.concatenate(outs, axis=0)
        o = o * lax.rsqrt(jnp.mean(o * o, axis=-1, keepdims=True) + EPS) * g_ref[...]
        r = r_ref[0, :, vs]
        o_ref[0, :, vs] = (o * (r * jax.nn.sigmoid(r))).astype(o_ref.dtype)


def _gla(proj, a_low, wa, ba, g, dk, dv, ts=256, heads_per_step=4):
    b, s, _ = proj.shape
    hp = heads_per_step
    hg = GLA_HEADS // hp
    ts = min(ts, s)
    wk, wv = hp * dk, hp * dv
    kq, kv = (GLA_HEADS * dk) // wk, (2 * GLA_HEADS * dk) // wv
    return pl.pallas_call(
        functools.partial(_gla_kernel, chunk=GLA_CHUNK, scale=dk ** -0.5),
        grid=(b, hg, s // ts),
        in_specs=[pl.BlockSpec((1, ts, wk), lambda bi, hi, si: (bi, si, hi)),
                  pl.BlockSpec((1, ts, wk), lambda bi, hi, si: (bi, si, kq + hi)),
                  pl.BlockSpec((1, ts, wv), lambda bi, hi, si: (bi, si, kv + hi)),
                  pl.BlockSpec((1, ts, wv), lambda bi, hi, si: (bi, si, kv + hg + hi)),
                  pl.BlockSpec((1, ts, LANES), lambda bi, hi, si: (bi, si, 0)),
                  pl.BlockSpec((LANES, wk), lambda bi, hi, si: (0, hi)),
                  pl.BlockSpec((1, wk), lambda bi, hi, si: (0, hi)),
                  pl.BlockSpec((1, dv), lambda bi, hi, si: (0, 0))],
        out_specs=pl.BlockSpec((1, ts, wv), lambda bi, hi, si: (bi, si, hi)),
        out_shape=jax.ShapeDtypeStruct((b, s, GLA_HEADS * dv), BF16),
        scratch_shapes=[pltpu.VMEM((hp, dv, dk), F32)],
        compiler_params=_params(("parallel", "parallel", "arbitrary"), 40),
        name="gla",
    )(proj, proj, proj, proj, a_low, wa, ba, g)


def _conv_kernel(a_ref, b_ref, bga_ref, bgb_ref, w_ref, bdw_ref, lg_ref, lb_ref, o_ref, ubuf, sbuf, cbuf, *,
                 width, cw, rc):
    ts, dc = a_ref.shape[1], a_ref.shape[2]
    si = pl.program_id(1)

    @pl.when(si == 0)
    def _():
        ubuf[0:CONV_HALO, :] = jnp.zeros((CONV_HALO, dc), F32)

    @pl.when(si > 0)
    def _():
        ubuf[0:CONV_HALO, :] = ubuf[ts:ts + CONV_HALO, :]

    ubuf[CONV_HALO:CONV_HALO + ts, :] = (a_ref[0] + bga_ref[...]) * jax.nn.sigmoid(b_ref[0] + bgb_ref[...])

    span = ts + CONV_HALO - SUBLANES

    def col_body(cb, carry):
        cs = pl.ds(pl.multiple_of(cb * cw, cw), cw)
        for b in range(1, SUBLANES):
            sbuf[b, 0:span, :] = ubuf[b:b + span, cs]
        for rb in range(ts // rc):
            acc = jnp.broadcast_to(bdw_ref[:, cs], (rc, cw))
            for j in range(width):
                off = CONV_HALO - (width - 1) + j
                b = off % SUBLANES
                r0 = off - b + rb * rc
                src = ubuf[r0:r0 + rc, cs] if b == 0 else sbuf[b, r0:r0 + rc, :]
                acc = acc + w_ref[j:j + 1, cs] * src
            cbuf[rb * rc:(rb + 1) * rc, cs] = acc
        return carry

    lax.fori_loop(0, dc // cw, col_body, 0)
    c = cbuf[...]
    mu = jnp.mean(c, axis=-1, keepdims=True)
    cen = c - mu
    var = jnp.mean(cen * cen, axis=-1, keepdims=True)
    un = cen * lax.rsqrt(var + EPS) * lg_ref[...] + lb_ref[...]
    o_ref[0] = (un * jax.nn.sigmoid(un)).astype(o_ref.dtype)


def _conv_module(proj, col0, dc, b_glu, w_dw, b_dw, ln_g, ln_b, ts=256):
    b, s, _ = proj.shape
    ts = min(ts, s)
    width = w_dw.shape[0]
    cw = min(128, dc)
    assert width - 1 <= CONV_HALO <= ts and col0 % dc == 0
    cblk = col0 // dc
    row = lambda v: v.reshape(1, dc)
    vec = pl.BlockSpec((1, dc), lambda bi, si: (0, 0))
    return pl.pallas_call(
        functools.partial(_conv_kernel, width=width, cw=cw, rc=min(128, ts)),
        grid=(b, s // ts),
        in_specs=[pl.BlockSpec((1, ts, dc), lambda bi, si: (bi, si, cblk)),
                  pl.BlockSpec((1, ts, dc), lambda bi, si: (bi, si, cblk + 1)),
                  vec, vec,
                  pl.BlockSpec((width, dc), lambda bi, si: (0, 0)),
                  vec, vec, vec],
        out_specs=pl.BlockSpec((1, ts, dc), lambda bi, si: (bi, si, 0)),
        out_shape=jax.ShapeDtypeStruct((b, s, dc), BF16),
        scratch_shapes=[pltpu.VMEM((CONV_HALO + ts, dc), F32),
                        pltpu.VMEM((SUBLANES, CONV_HALO + ts, cw), F32),
                        pltpu.VMEM((ts, dc), F32)],
        compiler_params=_params(("parallel", "arbitrary"), 32),
        name="conv_module",
    )(proj, proj, row(b_glu[:dc]), row(b_glu[dc:]), w_dw, row(b_dw), row(ln_g), row(ln_b))


def _router_kernel(h_ref, g_ref, wr_ref, br_ref, n_ref, idx_ref, gate_ref, *, n_exp):
    h = h_ref[...]
    d2 = h.shape[1] // 2
    n = h * lax.rsqrt(jnp.mean(h * h, axis=-1, keepdims=True) + EPS) * g_ref[...]
    n_ref[...] = _pack_halves(n[:, :d2], n[:, d2:])
    logits = jnp.dot(n.astype(BF16), wr_ref[...], preferred_element_type=F32) + br_ref[...]
    lane = lax.broadcasted_iota(jnp.int32, logits.shape, 1)
    lane_f = lane.astype(F32)
    neg = jnp.float32(-jnp.inf)
    l = jnp.where(lane < n_exp, logits, neg)
    vals, idxs = [], []
    for _ in range(TOP_K):
        m = jnp.max(l, axis=-1, keepdims=True)
        i = jnp.min(jnp.where(l == m, lane_f, float(LANES)), axis=-1, keepdims=True)
        vals.append(m)
        idxs.append(i)
        l = jnp.where(lane_f == i, neg, l)
    es = [jnp.exp(v - vals[0]) for v in vals]
    tot = es[0]
    for e in es[1:]:
        tot = tot + e
    idx_out = jnp.zeros(logits.shape, F32)
    gate_out = jnp.zeros(logits.shape, F32)
    for kk in range(TOP_K):
        idx_out = jnp.where(lane == kk, idxs[kk], idx_out)
        gate_out = jnp.where(lane == kk, es[kk] / tot, gate_out)
    idx_ref[...] = idx_out.astype(jnp.int32)
    gate_ref[...] = gate_out


def _router(h, g, wr, br, n_exp, rows=256):
    t, d = h.shape
    rows = min(rows, t)
    blk = lambda w: pl.BlockSpec((rows, w), lambda i: (i, 0))
    return pl.pallas_call(
        functools.partial(_router_kernel, n_exp=n_exp),
        grid=(t // rows,),
        in_specs=[blk(d),
                  pl.BlockSpec((1, d), lambda i: (0, 0)),
                  pl.BlockSpec((d, LANES), lambda i: (0, 0)),
                  pl.BlockSpec((1, LANES), lambda i: (0, 0))],
        out_specs=[blk(d // 2), blk(LANES), blk(LANES)],
        out_shape=[jax.ShapeDtypeStruct((t, d // 2), U32),
                   jax.ShapeDtypeStruct((t, LANES), jnp.int32),
                   jax.ShapeDtypeStruct((t, LANES), F32)],
        compiler_params=_params(("parallel",), 32),
        name="ffn_norm_router",
    )(h, g.reshape(1, d), wr, br)


def _dispatch_kernel(zs_ref, nu_ref, slot_ref, n_ref, xs_hbm, zbuf, zsem, sem, *, rows, n_zero):
    def zero_copy(row):
        return pltpu.make_async_copy(zbuf, xs_hbm.at[pl.ds(pl.multiple_of(row, MOE_HALF), MOE_HALF)], zsem)

    @pl.when(pl.program_id(0) == 0)
    def _():
        zbuf[...] = jnp.zeros_like(zbuf)
        n_halves = xs_hbm.shape[0] // MOE_HALF
        first_unused = nu_ref[0] * (MOE_ROWS // MOE_HALF)
        for z in range(n_zero):
            @pl.when(zs_ref[z] >= 0)
            def _():
                zero_copy(zs_ref[z]).start()

        def tail_start(hi, carry):
            zero_copy(hi * MOE_HALF).start()
            return carry

        def tail_wait(hi, carry):
            zero_copy(hi * MOE_HALF).wait()
            return carry

        lax.fori_loop(first_unused, n_halves, tail_start, 0)
        for z in range(n_zero):
            @pl.when(zs_ref[z] >= 0)
            def _():
                zero_copy(zs_ref[z]).wait()
        lax.fori_loop(first_unused, n_halves, tail_wait, 0)

    def row_copy(r, slot):
        return pltpu.make_async_copy(n_ref.at[pl.ds(r, 1)], xs_hbm.at[pl.ds(slot, 1)], sem)

    def start(r, carry):
        for kk in range(TOP_K):
            row_copy(r, slot_ref[0, 0, r * TOP_K + kk]).start(priority=kk % 2)
        return carry

    lax.fori_loop(0, rows, start, 0)
    for kk in range(TOP_K):
        pltpu.make_async_copy(n_ref, xs_hbm.at[pl.ds(0, rows)], sem).wait()


def _dispatch(n_packed, slot, zero_start, n_used, n_slots, rows=512):
    t, d2 = n_packed.shape
    rows = min(rows, t)
    grid_spec = pltpu.PrefetchScalarGridSpec(
        num_scalar_prefetch=2,
        grid=(t // rows,),
        in_specs=[pl.BlockSpec((1, 1, rows * TOP_K), lambda i, zs, nu: (i, 0, 0), memory_space=pltpu.SMEM),
                  pl.BlockSpec((rows, d2), lambda i, zs, nu: (i, 0))],
        out_specs=pl.BlockSpec(memory_space=pl.ANY),
        scratch_shapes=[pltpu.VMEM((MOE_HALF, d2), U32), pltpu.SemaphoreType.DMA(()), pltpu.SemaphoreType.DMA(())],
    )
    return pl.pallas_call(
        functools.partial(_dispatch_kernel, rows=rows, n_zero=zero_start.shape[0]),
        grid_spec=grid_spec,
        out_shape=jax.ShapeDtypeStruct((n_slots, d2), U32),
        compiler_params=_params(("arbitrary",), 32),
        name="moe_dispatch",
    )(zero_start, n_used, slot.reshape(t // rows, 1, rows * TOP_K), n_packed)


def _block_state(be_ref, nu_ref, bi):
    used = bi < nu_ref[0]
    fresh = jnp.logical_or(bi == 0, be_ref[bi] != be_ref[jnp.maximum(bi - 1, 0)])
    return used, jnp.logical_and(used, fresh)


def _run_weights(copies, first, fresh, cast, then):
    first_cond, first_args = first

    @pl.when(first_cond)
    def _():
        for c in copies(*first_args):
            c.start(priority=BULK_DMA_PRIORITY)

    @pl.when(fresh)
    def _():
        for c in copies(*first_args):
            c.wait()
        cast()
        for cond, args in then:
            @pl.when(cond)
            def _():
                for c in copies(*args):
                    c.start(priority=BULK_DMA_PRIORITY)


def _cast_rows(src, dst, rows):
    def body(i, carry):
        r = pl.ds(pl.multiple_of(i * rows, rows), rows)
        dst[r, :] = src[r, :].astype(BF16)
        return carry
    lax.fori_loop(0, src.shape[0] // rows, body, 0)


def _per_block_halves(halves, compute, o_ref):
    @pl.when(halves == 2)
    def _():
        compute(MOE_ROWS)

    @pl.when(halves == 1)
    def _():
        compute(MOE_HALF)
        o_ref[MOE_HALF:MOE_ROWS, :] = jnp.zeros((MOE_ROWS - MOE_HALF, o_ref.shape[1]), o_ref.dtype)

    @pl.when(halves == 0)
    def _():
        o_ref[...] = jnp.zeros_like(o_ref)


def _moe_up_kernel(be_ref, nu_ref, nx_ref, bh_ref, x_ref, bg_ref, bl_ref, w_hbm, o_ref, sg, sl, wgb, wlb, sem, *,
                   nf):
    fi, bi = pl.program_id(0), pl.program_id(1)
    _, fresh = _block_state(be_ref, nu_ref, bi)
    tf = sg.shape[1]
    f_all = w_hbm.shape[2] // 2

    def copies(e, f):
        c0 = pl.multiple_of(f * tf, tf)
        return (pltpu.make_async_copy(w_hbm.at[e, :, pl.ds(c0, tf)], sg, sem.at[0]),
                pltpu.make_async_copy(w_hbm.at[e, :, pl.ds(f_all + c0, tf)], sl, sem.at[1]))

    def cast():
        _cast_rows(sg, wgb, CAST_ROWS)
        _cast_rows(sl, wlb, CAST_ROWS)

    nx = nx_ref[bi]
    _run_weights(copies, (jnp.logical_and(fi == 0, bi == 0), (be_ref[0], 0)), fresh, cast,
                 [(nx >= 0, (nx, fi)),
                  (jnp.logical_and(nx < 0, fi + 1 < nf), (be_ref[0], fi + 1))])

    def compute(rows):
        d2 = x_ref.shape[1]
        lo, hi = _unpack_halves(x_ref[0:rows, :])
        lo, hi = lo.astype(BF16), hi.astype(BF16)

        def proj(wb, b_ref):
            return (jnp.dot(lo, wb[0:d2, :], preferred_element_type=F32)
                    + jnp.dot(hi, wb[d2:2 * d2, :], preferred_element_type=F32) + b_ref[0])

        xg = jnp.minimum(proj(wgb, bg_ref), SWIGLU_LIMIT)
        xl = jnp.clip(proj(wlb, bl_ref), -SWIGLU_LIMIT, SWIGLU_LIMIT)
        o_ref[0:rows, :] = (xg * jax.nn.sigmoid(SWIGLU_ALPHA * xg) * (xl + 1.0)).astype(o_ref.dtype)

    _per_block_halves(bh_ref[bi], compute, o_ref)


def _moe_up(xs, block_expert, n_used, next_expert, block_halves, w1, b1, tf=512):
    n_slots, d2 = xs.shape
    e, d, f2 = w1.shape
    f = f2 // 2
    tf = min(tf, f)
    nf = f // tf
    nb = n_slots // MOE_ROWS
    blk = lambda bi, nu: jnp.minimum(bi, nu[0] - 1)
    grid_spec = pltpu.PrefetchScalarGridSpec(
        num_scalar_prefetch=4,
        grid=(nf, nb),
        in_specs=[pl.BlockSpec((MOE_ROWS, d2), lambda fi, bi, be, nu, nx, bh: (blk(bi, nu), 0)),
                  pl.BlockSpec((1, 1, tf), lambda fi, bi, be, nu, nx, bh: (be[blk(bi, nu)], 0, fi)),
                  pl.BlockSpec((1, 1, tf), lambda fi, bi, be, nu, nx, bh: (be[blk(bi, nu)], 0, nf + fi)),
                  pl.BlockSpec(memory_space=pl.ANY)],
        out_specs=pl.BlockSpec((MOE_ROWS, tf), lambda fi, bi, be, nu, nx, bh: (bi, fi)),
        scratch_shapes=[pltpu.VMEM((d, tf), F32), pltpu.VMEM((d, tf), F32),
                        pltpu.VMEM((d, tf), BF16), pltpu.VMEM((d, tf), BF16),
                        pltpu.SemaphoreType.DMA((2,))],
    )
    return pl.pallas_call(
        functools.partial(_moe_up_kernel, nf=nf),
        grid_spec=grid_spec,
        out_shape=jax.ShapeDtypeStruct((n_slots, f), BF16),
        compiler_params=_params(("arbitrary", "arbitrary"), 48),
        name="moe_up",
    )(block_expert, n_used, next_expert, block_halves, xs, b1.reshape(e, 1, f2), b1.reshape(e, 1, f2), w1)


def _moe_down_kernel(be_ref, nu_ref, nx_ref, bh_ref, h_ref, b_ref, w_hbm, o_ref, sw, wb, sem):
    bi = pl.program_id(0)
    _, fresh = _block_state(be_ref, nu_ref, bi)

    def copies(e):
        return (pltpu.make_async_copy(w_hbm.at[e], sw, sem.at[0]),)

    def cast():
        wb[...] = sw[...].astype(BF16)

    nx = nx_ref[bi]
    _run_weights(copies, (bi == 0, (be_ref[0],)), fresh, cast, [(nx >= 0, (nx,))])

    def compute(rows):
        d2 = o_ref.shape[1]
        h = h_ref[0:rows, :]
        y_lo = jnp.dot(h, wb[:, 0:d2], preferred_element_type=F32) + b_ref[0, :, 0:d2]
        y_hi = jnp.dot(h, wb[:, d2:2 * d2], preferred_element_type=F32) + b_ref[0, :, d2:2 * d2]
        o_ref[0:rows, :] = _pack_halves(y_lo, y_hi)

    _per_block_halves(bh_ref[bi], compute, o_ref)


def _moe_down(hid, block_expert, n_used, next_expert, block_halves, w2, b2):
    n_slots, f = hid.shape
    e, _, d = w2.shape
    nb = n_slots // MOE_ROWS
    blk = lambda bi, nu: jnp.minimum(bi, nu[0] - 1)
    grid_spec = pltpu.PrefetchScalarGridSpec(
        num_scalar_prefetch=4,
        grid=(nb,),
        in_specs=[pl.BlockSpec((MOE_ROWS, f), lambda bi, be, nu, nx, bh: (blk(bi, nu), 0)),
                  pl.BlockSpec((1, 1, d), lambda bi, be, nu, nx, bh: (be[blk(bi, nu)], 0, 0)),
                  pl.BlockSpec(memory_space=pl.ANY)],
        out_specs=pl.BlockSpec((MOE_ROWS, d // 2), lambda bi, be, nu, nx, bh: (bi, 0)),
        scratch_shapes=[pltpu.VMEM((f, d), F32), pltpu.VMEM((f, d), BF16), pltpu.SemaphoreType.DMA((1,))],
    )
    return pl.pallas_call(
        _moe_down_kernel,
        grid_spec=grid_spec,
        out_shape=jax.ShapeDtypeStruct((n_slots, d // 2), U32),
        compiler_params=_params(("arbitrary",), 48),
        name="moe_down",
    )(block_expert, n_used, next_expert, block_halves, hid, b2.reshape(e, 1, d), w2)


def _combine_kernel(slot_ref, nslot_ref, gate_ref, h_ref, g_ref, y_hbm, o_ref, *scratch, rows):
    bufs, sem = scratch[:COMBINE_PHASES], scratch[COMBINE_PHASES]
    i = pl.program_id(0)
    d2 = bufs[0].shape[2]
    group = SUBLANES

    def start_rows(s_ref, base, buf, sem_i, r0):
        for r in range(r0, r0 + group):
            for kk in range(TOP_K):
                pltpu.make_async_copy(y_hbm.at[pl.ds(s_ref[0, 0, (base + r) * TOP_K + kk], 1)],
                                      buf.at[kk, pl.ds(r, 1)], sem.at[sem_i]).start(priority=kk % 2)

    def wait_all(buf, sem_i):
        for kk in range(TOP_K):
            pltpu.make_async_copy(y_hbm.at[pl.ds(0, rows)], buf.at[kk], sem.at[sem_i]).wait()

    def compute_rows(buf, base, r0):
        tok = slice(base + r0, base + r0 + group)
        gates = gate_ref[tok, :]
        acc_lo, acc_hi = h_ref[tok, 0:d2], h_ref[tok, d2:2 * d2]
        for kk in range(TOP_K):
            lo, hi = _unpack_halves(buf[kk, r0:r0 + group, :])
            gk = gates[:, kk:kk + 1]
            acc_lo = acc_lo + gk * lo
            acc_hi = acc_hi + gk * hi
        ms = (jnp.sum(acc_lo * acc_lo, axis=-1, keepdims=True)
              + jnp.sum(acc_hi * acc_hi, axis=-1, keepdims=True)) * (1.0 / (2 * d2))
        inv = lax.rsqrt(ms + EPS)
        o_ref[tok, 0:d2] = acc_lo * inv * g_ref[:, 0:d2]
        o_ref[tok, d2:2 * d2] = acc_hi * inv * g_ref[:, d2:2 * d2]

    @pl.when(i == 0)
    def _():
        for p in range(COMBINE_AHEAD):
            def body(r, carry, p=p):
                for kk in range(TOP_K):
                    pltpu.make_async_copy(y_hbm.at[pl.ds(slot_ref[0, 0, (p * rows + r) * TOP_K + kk], 1)],
                                          bufs[p].at[kk, pl.ds(r, 1)], sem.at[p]).start(priority=kk % 2)
                return carry
            lax.fori_loop(0, rows, body, 0)

    for p in range(COMBINE_PHASES):
        wait_all(bufs[p], p)
        q = p + COMBINE_AHEAD
        s_ref, qq = (slot_ref, q) if q < COMBINE_PHASES else (nslot_ref, q - COMBINE_PHASES)
        for r0 in range(0, rows, group):
            start_rows(s_ref, qq * rows, bufs[qq], qq, r0)
            compute_rows(bufs[p], p * rows, r0)

    @pl.when(i == pl.num_programs(0) - 1)
    def _():
        for p in range(COMBINE_AHEAD):
            wait_all(bufs[p], p)


def _combine(y_packed, slot, gates, h, g, rows=64):
    t, d = h.shape
    rows = min(rows, t // COMBINE_PHASES)
    block = COMBINE_PHASES * rows
    nsteps = t // block
    slot3 = slot.reshape(nsteps, 1, block * TOP_K)
    return pl.pallas_call(
        functools.partial(_combine_kernel, rows=rows),
        grid=(nsteps,),
        in_specs=[pl.BlockSpec((1, 1, block * TOP_K), lambda i: (i, 0, 0), memory_space=pltpu.SMEM),
                  pl.BlockSpec((1, 1, block * TOP_K), lambda i: (jnp.minimum(i + 1, nsteps - 1), 0, 0),
                               memory_space=pltpu.SMEM),
                  pl.BlockSpec((block, LANES), lambda i: (i, 0)),
                  pl.BlockSpec((block, d), lambda i: (i, 0)),
                  pl.BlockSpec((1, d), lambda i: (0, 0)),
                  pl.BlockSpec(memory_space=pl.ANY)],
        out_specs=pl.BlockSpec((block, d), lambda i: (i, 0)),
        out_shape=jax.ShapeDtypeStruct((t, d), F32),
        scratch_shapes=([pltpu.VMEM((TOP_K, rows, d // 2), U32) for _ in range(COMBINE_PHASES)]
                        + [pltpu.SemaphoreType.DMA((COMBINE_PHASES,))]),
        compiler_params=_params(("arbitrary",), 40),
        name="moe_combine",
    )(slot3, slot3, gates, h, g.reshape(1, d), y_packed)


def _routing(top_idx, n_exp):
    t = top_idx.shape[0]
    experts = jnp.arange(n_exp, dtype=jnp.int32)
    onehot = (top_idx[:, :, None] == experts[None, None, :]).any(axis=1).astype(jnp.int32)
    incl = jnp.cumsum(onehot, axis=0)
    counts = incl[-1]
    padded = (counts + MOE_ROWS - 1) // MOE_ROWS * MOE_ROWS
    pad_ends = jnp.cumsum(padded)
    slot = jnp.take_along_axis(incl - onehot + (pad_ends - padded)[None, :], top_idx, axis=1)
    n_blocks = -(-(t * TOP_K) // MOE_ROWS) + n_exp
    block_rows = jnp.arange(n_blocks, dtype=jnp.int32) * MOE_ROWS
    block_expert = jnp.minimum(jnp.sum(pad_ends[None, :] <= block_rows[:, None], axis=1), n_exp - 1)
    n_used = (pad_ends[-1] // MOE_ROWS).reshape(1)
    rows_in_block = jnp.clip((counts - padded + pad_ends)[block_expert] - block_rows, 0, MOE_ROWS)
    block_halves = jnp.where(block_rows < pad_ends[-1], jnp.where(rows_in_block <= MOE_HALF, 1, 2), 0)
    last_rows = counts - padded + MOE_ROWS
    zero_start = jnp.concatenate([
        jnp.where(jnp.logical_and(counts > 0, last_rows < MOE_ROWS), pad_ends - MOE_HALF, -1),
        jnp.where(jnp.logical_and(counts > 0, last_rows < MOE_HALF), pad_ends - MOE_ROWS, -1)])
    later = jnp.logical_and(experts[None, :] > experts[:, None], padded[None, :] > 0)
    next_of = jnp.min(jnp.where(later, experts[None, :], n_exp), axis=1)
    next_expert = jnp.where(next_of < n_exp, next_of, -1)[block_expert]
    i32 = lambda a: a.astype(jnp.int32)
    return (i32(slot), i32(block_expert), i32(n_used), i32(next_expert), i32(block_halves), i32(zero_start),
            n_blocks * MOE_ROWS)


def kernel(x, attn_norm_g, w_in, w_a2, b_a, gla_norm_g, b_glu, w_dw, b_dw, conv_ln_g, conv_ln_b, w_out,
           ffn_norm_g, w_router, b_router, w1, b1, w2, b2, final_norm_g):
    b, s, d = x.shape
    assert w_in.shape[0] == 1, "single-layer stack"
    t = b * s
    dv = gla_norm_g.shape[1]
    dk = w_a2.shape[2] // GLA_HEADS
    lowrank = w_a2.shape[1]
    dc = w_dw.shape[2]
    n_exp = w_router.shape[2]
    c_qkvr = 2 * GLA_HEADS * (dk + dv)
    assert lowrank <= LANES and n_exp <= LANES
    x2 = x.reshape(t, d)

    n1 = _rmsnorm(x2, attn_norm_g[0], BF16)
    w_main, w_low = _wprep(jnp.swapaxes(w_in[0], 0, 1), c_qkvr, lowrank)
    proj = _matmul_nt(n1, w_main, "inproj").reshape(b, s, -1)
    a_low = _matmul_nt(n1, w_low, "inproj_lowrank").reshape(b, s, LANES)

    wa = jnp.pad(w_a2[0], ((0, LANES - lowrank), (0, 0))).astype(BF16)
    gla_out = _gla(proj, a_low, wa, b_a[0].reshape(1, -1), gla_norm_g[0].reshape(1, dv), dk, dv)
    conv_out = _conv_module(proj, c_qkvr, dc, b_glu[0], w_dw[0], b_dw[0], conv_ln_g[0], conv_ln_b[0])
    h = _matmul([gla_out.reshape(t, -1), conv_out.reshape(t, dc)], w_out[0], d, "outproj", res=x2)

    wr = jnp.pad(w_router[0], ((0, 0), (0, LANES - n_exp))).astype(BF16)
    br = jnp.pad(b_router[0], (0, LANES - n_exp)).reshape(1, LANES)
    n2, idx_pad, gate_pad = _router(h, ffn_norm_g[0], wr, br, n_exp)
    slot, block_expert, n_used, next_expert, block_halves, zero_start, n_slots = _routing(idx_pad[:, :TOP_K], n_exp)
    xs = _dispatch(n2, slot, zero_start, n_used, n_slots)
    hid = _moe_up(xs, block_expert, n_used, next_expert, block_halves, w1[0], b1[0])
    y = _moe_down(hid, block_expert, n_used, next_expert, block_halves, w2[0], b2[0])
    out = _combine(y, slot, gate_pad, h, final_norm_g)
    return out.reshape(b, s, d)
```

```python
import functools

import jax
import jax.numpy as jnp
from jax import lax
from jax.experimental import pallas as pl
from jax.experimental.pallas import tpu as pltpu

GLA_HEADS = 4
GLA_TAU = 16.0
GLA_CHUNK = 64
TOP_K = 4
SWIGLU_ALPHA = 1.702
SWIGLU_LIMIT = 7.0
EPS = 1e-5

LANES = 128
SUBLANES = 8
MOE_ROWS = 512
MOE_HALF = MOE_ROWS // 2
CONV_HALO = 32
COMBINE_PHASES = 4
COMBINE_AHEAD = 2
CAST_ROWS = 128
BULK_DMA_PRIORITY = 1
MIB = 1024 * 1024

F32 = jnp.float32
BF16 = jnp.bfloat16
U32 = jnp.uint32


def _params(semantics, vmem_mib):
    return pltpu.CompilerParams(dimension_semantics=semantics, vmem_limit_bytes=vmem_mib * MIB)


def _pack_halves(lo, hi):
    lo_u = lax.bitcast_convert_type(lo.astype(BF16).astype(F32), U32)
    hi_u = lax.bitcast_convert_type(hi.astype(BF16).astype(F32), U32)
    return (lo_u >> 16) | (hi_u & jnp.uint32(0xFFFF0000))


def _unpack_halves(p):
    lo = lax.bitcast_convert_type(p << 16, F32)
    hi = lax.bitcast_convert_type(p & jnp.uint32(0xFFFF0000), F32)
    return lo, hi


def _rmsnorm_kernel(x_ref, g_ref, o_ref):
    x = x_ref[...]
    y = x * lax.rsqrt(jnp.mean(x * x, axis=-1, keepdims=True) + EPS)
    o_ref[...] = (y * g_ref[...]).astype(o_ref.dtype)


def _rmsnorm(x, g, out_dtype, rows=256):
    t, d = x.shape
    rows = min(rows, t)
    return pl.pallas_call(
        _rmsnorm_kernel,
        grid=(t // rows,),
        in_specs=[pl.BlockSpec((rows, d), lambda i: (i, 0)),
                  pl.BlockSpec((1, d), lambda i: (0, 0))],
        out_specs=pl.BlockSpec((rows, d), lambda i: (i, 0)),
        out_shape=jax.ShapeDtypeStruct((t, d), out_dtype),
        compiler_params=_params(("parallel",), 32),
        name="rmsnorm",
    )(x, g.reshape(1, d))


def _wprep_kernel(a_ref, b_ref, o_ref, low_ref, prev, *, nb_main, lowrank):
    i = pl.program_id(0)
    rows = o_ref.shape[0]

    def straddle(first_ref):
        o_ref[0:rows - lowrank, :] = first_ref[lowrank:rows, :].astype(BF16)
        o_ref[rows - lowrank:rows, :] = b_ref[0:lowrank, :].astype(BF16)
        prev[...] = b_ref[...]

    @pl.when(i < nb_main)
    def _():
        o_ref[...] = a_ref[...].astype(BF16)

    @pl.when(i == nb_main)
    def _():
        low_ref[...] = jnp.zeros_like(low_ref)
        low_ref[0:lowrank, :] = a_ref[0:lowrank, :].astype(BF16)
        straddle(a_ref)

    @pl.when(i > nb_main)
    def _():
        straddle(prev)


def _wprep(wt, c_main, lowrank):
    n_in, k = wt.shape
    rows = 2 * LANES
    n_out = n_in - lowrank
    assert c_main % rows == 0 and n_out % rows == 0 and lowrank % 16 == 0
    last = pl.cdiv(n_in, rows) - 1
    nb_main = c_main // rows
    return pl.pallas_call(
        functools.partial(_wprep_kernel, nb_main=nb_main, lowrank=lowrank),
        grid=(n_out // rows,),
        in_specs=[pl.BlockSpec((rows, k), lambda i: (jnp.minimum(i, nb_main), 0)),
                  pl.BlockSpec((rows, k), lambda i: (jnp.minimum(jnp.maximum(i, nb_main) + 1, last), 0))],
        out_specs=[pl.BlockSpec((rows, k), lambda i: (i, 0)),
                   pl.BlockSpec((LANES, k), lambda i: (0, 0))],
        out_shape=[jax.ShapeDtypeStruct((n_out, k), BF16), jax.ShapeDtypeStruct((LANES, k), BF16)],
        scratch_shapes=[pltpu.VMEM((rows, k), F32)],
        compiler_params=_params(("arbitrary",), 32),
        name="inproj_weight_prep",
    )(wt, wt)


def _mm_nt_kernel(a_ref, wt_ref, o_ref):
    o_ref[...] = lax.dot_general(a_ref[...], wt_ref[...], (((1,), (1,)), ((), ())), preferred_element_type=F32)


def _matmul_nt(a, wt, name, tm=1024, tn=1024):
    m, k = a.shape
    n = wt.shape[0]
    tm, tn = min(tm, m), min(tn, n)
    return pl.pallas_call(
        _mm_nt_kernel,
        grid=(m // tm, n // tn),
        in_specs=[pl.BlockSpec((tm, k), lambda i, j: (i, 0)),
                  pl.BlockSpec((tn, k), lambda i, j: (j, 0))],
        out_specs=pl.BlockSpec((tm, tn), lambda i, j: (i, j)),
        out_shape=jax.ShapeDtypeStruct((m, n), F32),
        compiler_params=_params(("parallel", "arbitrary"), 48),
        name=name,
    )(a, wt)


def _mm_kernel(*refs, n_a, has_res):
    a_refs, w_ref = refs[:n_a], refs[n_a]
    res_ref = refs[n_a + 1] if has_res else None
    o_ref, wb = refs[-2], refs[-1]

    @pl.when(pl.program_id(1) == 0)
    def _():
        _cast_rows(w_ref, wb, CAST_ROWS)

    acc, k0 = None, 0
    for a_ref in a_refs:
        kk = a_ref.shape[1]
        part = jnp.dot(a_ref[...], wb[k0:k0 + kk, :], preferred_element_type=F32)
        acc = part if acc is None else acc + part
        k0 += kk
    if has_res:
        acc = acc + res_ref[...]
    o_ref[...] = acc.astype(o_ref.dtype)


def _matmul(a_list, w, n_out, name, res=None, tm=1024, tn=512):
    m = a_list[0].shape[0]
    k = w.shape[0]
    assert sum(a.shape[1] for a in a_list) == k
    tm, tn = min(tm, m), min(tn, n_out)
    in_specs = [pl.BlockSpec((tm, a.shape[1]), lambda j, i: (i, 0)) for a in a_list]
    in_specs.append(pl.BlockSpec((k, tn), lambda j, i: (0, j)))
    args = list(a_list) + [w]
    if res is not None:
        in_specs.append(pl.BlockSpec((tm, tn), lambda j, i: (i, j)))
        args.append(res)
    return pl.pallas_call(
        functools.partial(_mm_kernel, n_a=len(a_list), has_res=res is not None),
        grid=(n_out // tn, m // tm),
        in_specs=in_specs,
        out_specs=pl.BlockSpec((tm, tn), lambda j, i: (i, j)),
        out_shape=jax.ShapeDtypeStruct((m, n_out), F32),
        scratch_shapes=[pltpu.VMEM((k, tn), BF16)],
        compiler_params=_params(("parallel", "arbitrary"), 52),
        name=name,
    )(*args)


def _gla_kernel(q_ref, k_ref, v_ref, r_ref, al_ref, wa_ref, ba_ref, g_ref, o_ref, st_ref, *, chunk, scale):
    @pl.when(pl.program_id(2) == 0)
    def _():
        st_ref[...] = jnp.zeros_like(st_ref)

    ts = q_ref.shape[1]
    heads, dv, dk = st_ref.shape
    rows = lax.broadcasted_iota(jnp.int32, (ts, ts), 0)
    cols = lax.broadcasted_iota(jnp.int32, (ts, ts), 1)
    assert chunk & (chunk - 1) == 0
    same_chunk = (rows & -chunk) == (cols & -chunk)
    causal = jnp.logical_and(same_chunk, rows >= cols)
    tril = jnp.where(causal, 1.0, 0.0).astype(BF16)
    ones = jnp.where(same_chunk, 1.0, 0.0).astype(BF16)
    nt = (((1,), (1,)), ((), ()))
    tn = (((0,), (0,)), ((), ()))

    z = jnp.dot(al_ref[0].astype(BF16), wa_ref[...], preferred_element_type=F32) + ba_ref[...]
    la = (jnp.minimum(z, 0.0) - jnp.log1p(jnp.exp(-jnp.abs(z)))) * (1.0 / GLA_TAU)
    la_hi = la.astype(BF16)
    la_lo = (la - la_hi.astype(F32)).astype(BF16)
    bcum_all = jnp.dot(tril, la_hi, preferred_element_type=F32) + jnp.dot(tril, la_lo, preferred_element_type=F32)
    btot_all = jnp.dot(ones, la_hi, preferred_element_type=F32) + jnp.dot(ones, la_lo, preferred_element_type=F32)
    for i in range(heads):
        ks, vs = slice(i * dk, (i + 1) * dk), slice(i * dv, (i + 1) * dv)
        q, k, v = q_ref[0, :, ks], k_ref[0, :, ks], v_ref[0, :, vs]
        bcum, btot = bcum_all[:, ks], btot_all[:, ks]
        q_e = (q * jnp.exp(bcum) * scale).astype(BF16)
        k_e = (k * jnp.exp(-bcum)).astype(BF16)
        k_d = (k * jnp.exp(btot - bcum)).astype(BF16)
        decay = jnp.exp(btot)
        vb = v.astype(BF16)
        s = lax.dot_general(q_e, k_e, nt, preferred_element_type=F32)
        s = jnp.where(causal, s, 0.0).astype(BF16)
        o_intra = jnp.dot(s, vb, preferred_element_type=F32)

        st = st_ref[i]
        outs = []
        for c in range(ts // chunk):
            lo, hi = c * chunk, (c + 1) * chunk
            outs.append(o_intra[lo:hi]
                        + lax.dot_general(q_e[lo:hi], st.astype(BF16), nt, preferred_element_type=F32))
            st = st * decay[lo:lo + 1] + lax.dot_general(vb[lo:hi], k_d[lo:hi], tn, preferred_element_type=F32)
        st_ref[i] = st
        o = jnp.concatenate(outs, axis=0)
        o = o * lax.rsqrt(jnp.mean(o * o, axis=-1, keepdims=True) + EPS) * g_ref[...]
        r = r_ref[0, :, vs]
        o_ref[0, :, vs] = (o * (r * jax.nn.sigmoid(r))).astype(o_ref.dtype)


def _gla(proj, a_low, wa, ba, g, dk, dv, ts=256, heads_per_step=4):
    b, s, _ = proj.shape
    hp = heads_per_step
    hg = GLA_HEADS // hp
    ts = min(ts, s)
    wk, wv = hp * dk, hp * dv
    kq, kv = (GLA_HEADS * dk) // wk, (2 * GLA_HEADS * dk) // wv
    return pl.pallas_call(
        functools.partial(_gla_kernel, chunk=GLA_CHUNK, scale=dk ** -0.5),
        grid=(b, hg, s // ts),
        in_specs=[pl.BlockSpec((1, ts, wk), lambda bi, hi, si: (bi, si, hi)),
                  pl.BlockSpec((1, ts, wk), lambda bi, hi, si: (bi, si, kq + hi)),
                  pl.BlockSpec((1, ts, wv), lambda bi, hi, si: (bi, si, kv + hi)),
                  pl.BlockSpec((1, ts, wv), lambda bi, hi, si: (bi, si, kv + hg + hi)),
                  pl.BlockSpec((1, ts, LANES), lambda bi, hi, si: (bi, si, 0)),
                  pl.BlockSpec((LANES, wk), lambda bi, hi, si: (0, hi)),
                  pl.BlockSpec((1, wk), lambda bi, hi, si: (0, hi)),
                  pl.BlockSpec((1, dv), lambda bi, hi, si: (0, 0))],
        out_specs=pl.BlockSpec((1, ts, wv), lambda bi, hi, si: (bi, si, hi)),
        out_shape=jax.ShapeDtypeStruct((b, s, GLA_HEADS * dv), BF16),
        scratch_shapes=[pltpu.VMEM((hp, dv, dk), F32)],
        compiler_params=_params(("parallel", "parallel", "arbitrary"), 40),
        name="gla",
    )(proj, proj, proj, proj, a_low, wa, ba, g)


def _conv_kernel(a_ref, b_ref, bga_ref, bgb_ref, w_ref, bdw_ref, lg_ref, lb_ref, o_ref, ubuf, sbuf, cbuf, *,
                 width, cw, rc):
    ts, dc = a_ref.shape[1], a_ref.shape[2]
    si = pl.program_id(1)

    @pl.when(si == 0)
    def _():
        ubuf[0:CONV_HALO, :] = jnp.zeros((CONV_HALO, dc), F32)

    @pl.when(si > 0)
    def _():
        ubuf[0:CONV_HALO, :] = ubuf[ts:ts + CONV_HALO, :]

    ubuf[CONV_HALO:CONV_HALO + ts, :] = (a_ref[0] + bga_ref[...]) * jax.nn.sigmoid(b_ref[0] + bgb_ref[...])

    span = ts + CONV_HALO - SUBLANES

    def col_body(cb, carry):
        cs = pl.ds(pl.multiple_of(cb * cw, cw), cw)
        for b in range(1, SUBLANES):
            sbuf[b, 0:span, :] = ubuf[b:b + span, cs]
        for rb in range(ts // rc):
            acc = jnp.broadcast_to(bdw_ref[:, cs], (rc, cw))
            for j in range(width):
                off = CONV_HALO - (width - 1) + j
                b = off % SUBLANES
                r0 = off - b + rb * rc
                src = ubuf[r0:r0 + rc, cs] if b == 0 else sbuf[b, r0:r0 + rc, :]
                acc = acc + w_ref[j:j + 1, cs] * src
            cbuf[rb * rc:(rb + 1) * rc, cs] = acc
        return carry

    lax.fori_loop(0, dc // cw, col_body, 0)
    c = cbuf[...]
    mu = jnp.mean(c, axis=-1, keepdims=True)
    cen = c - mu
    var = jnp.mean(cen * cen, axis=-1, keepdims=True)
    un = cen * lax.rsqrt(var + EPS) * lg_ref[...] + lb_ref[...]
    o_ref[0] = (un * jax.nn.sigmoid(un)).astype(o_ref.dtype)


def _conv_module(proj, col0, dc, b_glu, w_dw, b_dw, ln_g, ln_b, ts=256):
    b, s, _ = proj.shape
    ts = min(ts, s)
    width = w_dw.shape[0]
    cw = min(128, dc)
    assert width - 1 <= CONV_HALO <= ts and col0 % dc == 0
    cblk = col0 // dc
    row = lambda v: v.reshape(1, dc)
    vec = pl.BlockSpec((1, dc), lambda bi, si: (0, 0))
    return pl.pallas_call(
        functools.partial(_conv_kernel, width=width, cw=cw, rc=min(128, ts)),
        grid=(b, s // ts),
        in_specs=[pl.BlockSpec((1, ts, dc), lambda bi, si: (bi, si, cblk)),
                  pl.BlockSpec((1, ts, dc), lambda bi, si: (bi, si, cblk + 1)),
                  vec, vec,
                  pl.BlockSpec((width, dc), lambda bi, si: (0, 0)),
                  vec, vec, vec],
        out_specs=pl.BlockSpec((1, ts, dc), lambda bi, si: (bi, si, 0)),
        out_shape=jax.ShapeDtypeStruct((b, s, dc), BF16),
        scratch_shapes=[pltpu.VMEM((CONV_HALO + ts, dc), F32),
                        pltpu.VMEM((SUBLANES, CONV_HALO + ts, cw), F32),
                        pltpu.VMEM((ts, dc), F32)],
        compiler_params=_params(("parallel", "arbitrary"), 32),
        name="conv_module",
    )(proj, proj, row(b_glu[:dc]), row(b_glu[dc:]), w_dw, row(b_dw), row(ln_g), row(ln_b))


def _router_kernel(h_ref, g_ref, wr_ref, br_ref, n_ref, idx_ref, gate_ref, *, n_exp):
    h = h_ref[...]
    d2 = h.shape[1] // 2
    n = h * lax.rsqrt(jnp.mean(h * h, axis=-1, keepdims=True) + EPS) * g_ref[...]
    n_ref[...] = _pack_halves(n[:, :d2], n[:, d2:])
    logits = jnp.dot(n.astype(BF16), wr_ref[...], preferred_element_type=F32) + br_ref[...]
    lane = lax.broadcasted_iota(jnp.int32, logits.shape, 1)
    lane_f = lane.astype(F32)
    neg = jnp.float32(-jnp.inf)
    l = jnp.where(lane < n_exp, logits, neg)
    vals, idxs = [], []
    for _ in range(TOP_K):
        m = jnp.max(l, axis=-1, keepdims=True)
        i = jnp.min(jnp.where(l == m, lane_f, float(LANES)), axis=-1, keepdims=True)
        vals.append(m)
        idxs.append(i)
        l = jnp.where(lane_f == i, neg, l)
    es = [jnp.exp(v - vals[0]) for v in vals]
    tot = es[0]
    for e in es[1:]:
        tot = tot + e
    idx_out = jnp.zeros(logits.shape, F32)
    gate_out = jnp.zeros(logits.shape, F32)
    for kk in range(TOP_K):
        idx_out = jnp.where(lane == kk, idxs[kk], idx_out)
        gate_out = jnp.where(lane == kk, es[kk] / tot, gate_out)
    idx_ref[...] = idx_out.astype(jnp.int32)
    gate_ref[...] = gate_out


def _router(h, g, wr, br, n_exp, rows=256):
    t, d = h.shape
    rows = min(rows, t)
    blk = lambda w: pl.BlockSpec((rows, w), lambda i: (i, 0))
    return pl.pallas_call(
        functools.partial(_router_kernel, n_exp=n_exp),
        grid=(t // rows,),
        in_specs=[blk(d),
                  pl.BlockSpec((1, d), lambda i: (0, 0)),
                  pl.BlockSpec((d, LANES), lambda i: (0, 0)),
                  pl.BlockSpec((1, LANES), lambda i: (0, 0))],
        out_specs=[blk(d // 2), blk(LANES), blk(LANES)],
        out_shape=[jax.ShapeDtypeStruct((t, d // 2), U32),
                   jax.ShapeDtypeStruct((t, LANES), jnp.int32),
                   jax.ShapeDtypeStruct((t, LANES), F32)],
        compiler_params=_params(("parallel",), 32),
        name="ffn_norm_router",
    )(h, g.reshape(1, d), wr, br)


def _dispatch_kernel(zs_ref, nu_ref, slot_ref, n_ref, xs_hbm, zbuf, zsem, sem, *, rows, n_zero):
    def zero_copy(row):
        return pltpu.make_async_copy(zbuf, xs_hbm.at[pl.ds(pl.multiple_of(row, MOE_HALF), MOE_HALF)], zsem)

    @pl.when(pl.program_id(0) == 0)
    def _():
        zbuf[...] = jnp.zeros_like(zbuf)
        n_halves = xs_hbm.shape[0] // MOE_HALF
        first_unused = nu_ref[0] * (MOE_ROWS // MOE_HALF)
        for z in range(n_zero):
            @pl.when(zs_ref[z] >= 0)
            def _():
                zero_copy(zs_ref[z]).start()

        def tail_start(hi, carry):
            zero_copy(hi * MOE_HALF).start()
            return carry

        def tail_wait(hi, carry):
            zero_copy(hi * MOE_HALF).wait()
            return carry

        lax.fori_loop(first_unused, n_halves, tail_start, 0)
        for z in range(n_zero):
            @pl.when(zs_ref[z] >= 0)
            def _():
                zero_copy(zs_ref[z]).wait()
        lax.fori_loop(first_unused, n_halves, tail_wait, 0)

    def row_copy(r, slot):
        return pltpu.make_async_copy(n_ref.at[pl.ds(r, 1)], xs_hbm.at[pl.ds(slot, 1)], sem)

    def start(r, carry):
        for kk in range(TOP_K):
            row_copy(r, slot_ref[0, 0, r * TOP_K + kk]).start(priority=kk % 2)
        return carry

    lax.fori_loop(0, rows, start, 0)
    for kk in range(TOP_K):
        pltpu.make_async_copy(n_ref, xs_hbm.at[pl.ds(0, rows)], sem).wait()


def _dispatch(n_packed, slot, zero_start, n_used, n_slots, rows=512):
    t, d2 = n_packed.shape
    rows = min(rows, t)
    grid_spec = pltpu.PrefetchScalarGridSpec(
        num_scalar_prefetch=2,
        grid=(t // rows,),
        in_specs=[pl.BlockSpec((1, 1, rows * TOP_K), lambda i, zs, nu: (i, 0, 0), memory_space=pltpu.SMEM),
                  pl.BlockSpec((rows, d2), lambda i, zs, nu: (i, 0))],
        out_specs=pl.BlockSpec(memory_space=pl.ANY),
        scratch_shapes=[pltpu.VMEM((MOE_HALF, d2), U32), pltpu.SemaphoreType.DMA(()), pltpu.SemaphoreType.DMA(())],
    )
    return pl.pallas_call(
        functools.partial(_dispatch_kernel, rows=rows, n_zero=zero_start.shape[0]),
        grid_spec=grid_spec,
        out_shape=jax.ShapeDtypeStruct((n_slots, d2), U32),
        compiler_params=_params(("arbitrary",), 32),
        name="moe_dispatch",
    )(zero_start, n_used, slot.reshape(t // rows, 1, rows * TOP_K), n_packed)


def _block_state(be_ref, nu_ref, bi):
    used = bi < nu_ref[0]
    fresh = jnp.logical_or(bi == 0, be_ref[bi] != be_ref[jnp.maximum(bi - 1, 0)])
    return used, jnp.logical_and(used, fresh)


def _run_weights(copies, first, fresh, halves, compute, o_ref, then):
    first_cond, first_args = first

    @pl.when(first_cond)
    def _():
        for c in copies(*first_args):
            c.start(priority=BULK_DMA_PRIORITY)

    @pl.when(fresh)
    def _():
        for c in copies(*first_args):
            c.wait()

    for staged in (True, False):
        mine = fresh if staged else jnp.logical_not(fresh)
        _per_block_halves(jnp.where(mine, halves, -1), functools.partial(compute, staged=staged), o_ref)

    for cond, args in then:
        @pl.when(jnp.logical_and(fresh, cond))
        def _():
            for c in copies(*args):
                c.start(priority=BULK_DMA_PRIORITY)


def _cast_rows(src, dst, rows):
    def body(i, carry):
        r = pl.ds(pl.multiple_of(i * rows, rows), rows)
        dst[r, :] = src[r, :].astype(BF16)
        return carry
    lax.fori_loop(0, src.shape[0] // rows, body, 0)


def _per_block_halves(halves, compute, o_ref):
    @pl.when(halves == 2)
    def _():
        compute(MOE_ROWS)

    @pl.when(halves == 1)
    def _():
        compute(MOE_HALF)
        o_ref[MOE_HALF:MOE_ROWS, :] = jnp.zeros((MOE_ROWS - MOE_HALF, o_ref.shape[1]), o_ref.dtype)

    @pl.when(halves == 0)
    def _():
        o_ref[...] = jnp.zeros_like(o_ref)


def _moe_up_kernel(be_ref, nu_ref, nx_ref, bh_ref, x_ref, bg_ref, bl_ref, w_hbm, o_ref, sg, sl, wgb, wlb, sem, *,
                   nf):
    fi, bi = pl.program_id(0), pl.program_id(1)
    _, fresh = _block_state(be_ref, nu_ref, bi)
    tf = sg.shape[1]
    f_all = w_hbm.shape[2] // 2

    def copies(e, f):
        c0 = pl.multiple_of(f * tf, tf)
        return (pltpu.make_async_copy(w_hbm.at[e, :, pl.ds(c0, tf)], sg, sem.at[0]),
                pltpu.make_async_copy(w_hbm.at[e, :, pl.ds(f_all + c0, tf)], sl, sem.at[1]))

    def compute(rows, staged):
        d2 = x_ref.shape[1]
        lo, hi = _unpack_halves(x_ref[0:rows, :])
        lo, hi = lo.astype(BF16), hi.astype(BF16)
        if staged:
            kq = d2 // 2
            acc_g = acc_l = None
            for q in range(4):
                for r0 in range(q * kq, (q + 1) * kq, CAST_ROWS):
                    wgb[r0:r0 + CAST_ROWS, :] = sg[r0:r0 + CAST_ROWS, :].astype(BF16)
                    wlb[r0:r0 + CAST_ROWS, :] = sl[r0:r0 + CAST_ROWS, :].astype(BF16)
                xq = (lo, hi)[q // 2][:, (q % 2) * kq:(q % 2 + 1) * kq]
                pg = jnp.dot(xq, wgb[q * kq:(q + 1) * kq, :], preferred_element_type=F32)
                pl_ = jnp.dot(xq, wlb[q * kq:(q + 1) * kq, :], preferred_element_type=F32)
                acc_g = pg if acc_g is None else acc_g + pg
                acc_l = pl_ if acc_l is None else acc_l + pl_
            xg, xl = acc_g + bg_ref[0], acc_l + bl_ref[0]
        else:
            def proj(wb, b_ref):
                return (jnp.dot(lo, wb[0:d2, :], preferred_element_type=F32)
                        + jnp.dot(hi, wb[d2:2 * d2, :], preferred_element_type=F32) + b_ref[0])
            xg, xl = proj(wgb, bg_ref), proj(wlb, bl_ref)
        xg = jnp.minimum(xg, SWIGLU_LIMIT)
        xl = jnp.clip(xl, -SWIGLU_LIMIT, SWIGLU_LIMIT)
        o_ref[0:rows, :] = (xg * jax.nn.sigmoid(SWIGLU_ALPHA * xg) * (xl + 1.0)).astype(o_ref.dtype)

    nx = nx_ref[bi]
    _run_weights(copies, (jnp.logical_and(fi == 0, bi == 0), (be_ref[0], 0)), fresh, bh_ref[bi], compute, o_ref,
                 [(nx >= 0, (nx, fi)),
                  (jnp.logical_and(nx < 0, fi + 1 < nf), (be_ref[0], fi + 1))])


def _moe_up(xs, block_expert, n_used, next_expert, block_halves, w1, b1, tf=512):
    n_slots, d2 = xs.shape
    e, d, f2 = w1.shape
    f = f2 // 2
    tf = min(tf, f)
    nf = f // tf
    nb = n_slots // MOE_ROWS
    blk = lambda bi, nu: jnp.minimum(bi, nu[0] - 1)
    grid_spec = pltpu.PrefetchScalarGridSpec(
        num_scalar_prefetch=4,
        grid=(nf, nb),
        in_specs=[pl.BlockSpec((MOE_ROWS, d2), lambda fi, bi, be, nu, nx, bh: (blk(bi, nu), 0)),
                  pl.BlockSpec((1, 1, tf), lambda fi, bi, be, nu, nx, bh: (be[blk(bi, nu)], 0, fi)),
                  pl.BlockSpec((1, 1, tf), lambda fi, bi, be, nu, nx, bh: (be[blk(bi, nu)], 0, nf + fi)),
                  pl.BlockSpec(memory_space=pl.ANY)],
        out_specs=pl.BlockSpec((MOE_ROWS, tf), lambda fi, bi, be, nu, nx, bh: (bi, fi)),
        scratch_shapes=[pltpu.VMEM((d, tf), F32), pltpu.VMEM((d, tf), F32),
                        pltpu.VMEM((d, tf), BF16), pltpu.VMEM((d, tf), BF16),
                        pltpu.SemaphoreType.DMA((2,))],
    )
    return pl.pallas_call(
        functools.partial(_moe_up_kernel, nf=nf),
        grid_spec=grid_spec,
        out_shape=jax.ShapeDtypeStruct((n_slots, f), BF16),
        compiler_params=_params(("arbitrary", "arbitrary"), 48),
        name="moe_up",
    )(block_expert, n_used, next_expert, block_halves, xs, b1.reshape(e, 1, f2), b1.reshape(e, 1, f2), w1)


def _moe_down_kernel(be_ref, nu_ref, nx_ref, bh_ref, h_ref, b_ref, w_hbm, o_ref, sw, wb, sem):
    bi = pl.program_id(0)
    _, fresh = _block_state(be_ref, nu_ref, bi)

    def copies(e):
        return (pltpu.make_async_copy(w_hbm.at[e], sw, sem.at[0]),)

    def compute(rows, staged):
        f, d = wb.shape
        nq = d // 4
        h = h_ref[0:rows, :]
        ys = []
        for q in range(4):
            cols = slice(q * nq, (q + 1) * nq)
            if staged:
                step = CAST_ROWS // 2
                for r0 in range(0, f, step):
                    wb[r0:r0 + step, cols] = sw[r0:r0 + step, cols].astype(BF16)
            ys.append(jnp.dot(h, wb[:, cols], preferred_element_type=F32) + b_ref[0, :, cols])
        o_ref[0:rows, :] = _pack_halves(jnp.concatenate(ys[:2], axis=1), jnp.concatenate(ys[2:], axis=1))

    nx = nx_ref[bi]
    _run_weights(copies, (bi == 0, (be_ref[0],)), fresh, bh_ref[bi], compute, o_ref, [(nx >= 0, (nx,))])


def _moe_down(hid, block_expert, n_used, next_expert, block_halves, w2, b2):
    n_slots, f = hid.shape
    e, _, d = w2.shape
    nb = n_slots // MOE_ROWS
    blk = lambda bi, nu: jnp.minimum(bi, nu[0] - 1)
    grid_spec = pltpu.PrefetchScalarGridSpec(
        num_scalar_prefetch=4,
        grid=(nb,),
        in_specs=[pl.BlockSpec((MOE_ROWS, f), lambda bi, be, nu, nx, bh: (blk(bi, nu), 0)),
                  pl.BlockSpec((1, 1, d), lambda bi, be, nu, nx, bh: (be[blk(bi, nu)], 0, 0)),
                  pl.BlockSpec(memory_space=pl.ANY)],
        out_specs=pl.BlockSpec((MOE_ROWS, d // 2), lambda bi, be, nu, nx, bh: (bi, 0)),
        scratch_shapes=[pltpu.VMEM((f, d), F32), pltpu.VMEM((f, d), BF16), pltpu.SemaphoreType.DMA((1,))],
    )
    return pl.pallas_call(
        _moe_down_kernel,
        grid_spec=grid_spec,
        out_shape=jax.ShapeDtypeStruct((n_slots, d // 2), U32),
        compiler_params=_params(("arbitrary",), 48),
        name="moe_down",
    )(block_expert, n_used, next_expert, block_halves, hid, b2.reshape(e, 1, d), w2)


def _combine_kernel(slot_ref, nslot_ref, gate_ref, h_ref, g_ref, y_hbm, o_ref, *scratch, rows):
    bufs, sem = scratch[:COMBINE_PHASES], scratch[COMBINE_PHASES]
    i = pl.program_id(0)
    d2 = bufs[0].shape[2]
    group = SUBLANES

    def start_rows(s_ref, base, buf, sem_i, r0):
        for r in range(r0, r0 + group):
            for kk in range(TOP_K):
                pltpu.make_async_copy(y_hbm.at[pl.ds(s_ref[0, 0, (base + r) * TOP_K + kk], 1)],
                                      buf.at[kk, pl.ds(r, 1)], sem.at[sem_i]).start(priority=kk % 2)

    def wait_all(buf, sem_i):
        for kk in range(TOP_K):
            pltpu.make_async_copy(y_hbm.at[pl.ds(0, rows)], buf.at[kk], sem.at[sem_i]).wait()

    def compute_rows(buf, base, r0):
        tok = slice(base + r0, base + r0 + group)
        gates = gate_ref[tok, :]
        acc_lo, acc_hi = h_ref[tok, 0:d2], h_ref[tok, d2:2 * d2]
        for kk in range(TOP_K):
            lo, hi = _unpack_halves(buf[kk, r0:r0 + group, :])
            gk = gates[:, kk:kk + 1]
            acc_lo = acc_lo + gk * lo
            acc_hi = acc_hi + gk * hi
        ms = (jnp.sum(acc_lo * acc_lo, axis=-1, keepdims=True)
              + jnp.sum(acc_hi * acc_hi, axis=-1, keepdims=True)) * (1.0 / (2 * d2))
        inv = lax.rsqrt(ms + EPS)
        o_ref[tok, 0:d2] = acc_lo * inv * g_ref[:, 0:d2]
        o_ref[tok, d2:2 * d2] = acc_hi * inv * g_ref[:, d2:2 * d2]

    @pl.when(i == 0)
    def _():
        for p in range(COMBINE_AHEAD):
            def body(r, carry, p=p):
                for kk in range(TOP_K):
                    pltpu.make_async_copy(y_hbm.at[pl.ds(slot_ref[0, 0, (p * rows + r) * TOP_K + kk], 1)],
                                          bufs[p].at[kk, pl.ds(r, 1)], sem.at[p]).start(priority=kk % 2)
                return carry
            lax.fori_loop(0, rows, body, 0)

    for p in range(COMBINE_PHASES):
        wait_all(bufs[p], p)
        q = p + COMBINE_AHEAD
        s_ref, qq = (slot_ref, q) if q < COMBINE_PHASES else (nslot_ref, q - COMBINE_PHASES)
        for r0 in range(0, rows, group):
            start_rows(s_ref, qq * rows, bufs[qq], qq, r0)
            compute_rows(bufs[p], p * rows, r0)

    @pl.when(i == pl.num_programs(0) - 1)
    def _():
        for p in range(COMBINE_AHEAD):
            wait_all(bufs[p], p)


def _combine(y_packed, slot, gates, h, g, rows=64):
    t, d = h.shape
    rows = min(rows, t // COMBINE_PHASES)
    block = COMBINE_PHASES * rows
    nsteps = t // block
    slot3 = slot.reshape(nsteps, 1, block * TOP_K)
    return pl.pallas_call(
        functools.partial(_combine_kernel, rows=rows),
        grid=(nsteps,),
        in_specs=[pl.BlockSpec((1, 1, block * TOP_K), lambda i: (i, 0, 0), memory_space=pltpu.SMEM),
                  pl.BlockSpec((1, 1, block * TOP_K), lambda i: (jnp.minimum(i + 1, nsteps - 1), 0, 0),
                               memory_space=pltpu.SMEM),
                  pl.BlockSpec((block, LANES), lambda i: (i, 0)),
                  pl.BlockSpec((block, d), lambda i: (i, 0)),
                  pl.BlockSpec((1, d), lambda i: (0, 0)),
                  pl.BlockSpec(memory_space=pl.ANY)],
        out_specs=pl.BlockSpec((block, d), lambda i: (i, 0)),
        out_shape=jax.ShapeDtypeStruct((t, d), F32),
        scratch_shapes=([pltpu.VMEM((TOP_K, rows, d // 2), U32) for _ in range(COMBINE_PHASES)]
                        + [pltpu.SemaphoreType.DMA((COMBINE_PHASES,))]),
        compiler_params=_params(("arbitrary",), 40),
        name="moe_combine",
    )(slot3, slot3, gates, h, g.reshape(1, d), y_packed)


def _routing(top_idx, n_exp):
    t = top_idx.shape[0]
    experts = jnp.arange(n_exp, dtype=jnp.int32)
    onehot = (top_idx[:, :, None] == experts[None, None, :]).any(axis=1).astype(jnp.int32)
    incl = jnp.cumsum(onehot, axis=0)
    counts = incl[-1]
    padded = (counts + MOE_ROWS - 1) // MOE_ROWS * MOE_ROWS
    pad_ends = jnp.cumsum(padded)
    slot = jnp.take_along_axis(incl - onehot + (pad_ends - padded)[None, :], top_idx, axis=1)
    n_blocks = -(-(t * TOP_K) // MOE_ROWS) + n_exp
    block_rows = jnp.arange(n_blocks, dtype=jnp.int32) * MOE_ROWS
    block_expert = jnp.minimum(jnp.sum(pad_ends[None, :] <= block_rows[:, None], axis=1), n_exp - 1)
    n_used = (pad_ends[-1] // MOE_ROWS).reshape(1)
    rows_in_block = jnp.clip((counts - padded + pad_ends)[block_expert] - block_rows, 0, MOE_ROWS)
    block_halves = jnp.where(block_rows < pad_ends[-1], jnp.where(rows_in_block <= MOE_HALF, 1, 2), 0)
    last_rows = counts - padded + MOE_ROWS
    zero_start = jnp.concatenate([
        jnp.where(jnp.logical_and(counts > 0, last_rows < MOE_ROWS), pad_ends - MOE_HALF, -1),
        jnp.where(jnp.logical_and(counts > 0, last_rows < MOE_HALF), pad_ends - MOE_ROWS, -1)])
    later = jnp.logical_and(experts[None, :] > experts[:, None], padded[None, :] > 0)
    next_of = jnp.min(jnp.where(later, experts[None, :], n_exp), axis=1)
    next_expert = jnp.where(next_of < n_exp, next_of, -1)[block_expert]
    i32 = lambda a: a.astype(jnp.int32)
    return (i32(slot), i32(block_expert), i32(n_used), i32(next_expert), i32(block_halves), i32(zero_start),
            n_blocks * MOE_ROWS)


def kernel(x, attn_norm_g, w_in, w_a2, b_a, gla_norm_g, b_glu, w_dw, b_dw, conv_ln_g, conv_ln_b, w_out,
           ffn_norm_g, w_router, b_router, w1, b1, w2, b2, final_norm_g):
    b, s, d = x.shape
    assert w_in.shape[0] == 1, "single-layer stack"
    t = b * s
    dv = gla_norm_g.shape[1]
    dk = w_a2.shape[2] // GLA_HEADS
    lowrank = w_a2.shape[1]
    dc = w_dw.shape[2]
    n_exp = w_router.shape[2]
    c_qkvr = 2 * GLA_HEADS * (dk + dv)
    assert lowrank <= LANES and n_exp <= LANES
    x2 = x.reshape(t, d)

    n1 = _rmsnorm(x2, attn_norm_g[0], BF16)
    w_main, w_low = _wprep(jnp.swapaxes(w_in[0], 0, 1), c_qkvr, lowrank)
    proj = _matmul_nt(n1, w_main, "inproj").reshape(b, s, -1)
    a_low = _matmul_nt(n1, w_low, "inproj_lowrank").reshape(b, s, LANES)

    wa = jnp.pad(w_a2[0], ((0, LANES - lowrank), (0, 0))).astype(BF16)
    gla_out = _gla(proj, a_low, wa, b_a[0].reshape(1, -1), gla_norm_g[0].reshape(1, dv), dk, dv)
    conv_out = _conv_module(proj, c_qkvr, dc, b_glu[0], w_dw[0], b_dw[0], conv_ln_g[0], conv_ln_b[0])
    h = _matmul([gla_out.reshape(t, -1), conv_out.reshape(t, dc)], w_out[0], d, "outproj", res=x2)

    wr = jnp.pad(w_router[0], ((0, 0), (0, LANES - n_exp))).astype(BF16)
    br = jnp.pad(b_router[0], (0, LANES - n_exp)).reshape(1, LANES)
    n2, idx_pad, gate_pad = _router(h, ffn_norm_g[0], wr, br, n_exp)
    slot, block_expert, n_used, next_expert, block_halves, zero_start, n_slots = _routing(idx_pad[:, :TOP_K], n_exp)
    xs = _dispatch(n2, slot, zero_start, n_used, n_slots)
    hid = _moe_up(xs, block_expert, n_used, next_expert, block_halves, w1[0], b1[0])
    y = _moe_down(hid, block_expert, n_used, next_expert, block_halves, w2[0], b2[0])
    out = _combine(y, slot, gate_pad, h, final_norm_g)
    return out.reshape(b, s, d)
```

```python
import functools

import jax
import jax.numpy as jnp
from jax import lax
from jax.experimental import pallas as pl
from jax.experimental.pallas import tpu as pltpu

GLA_HEADS = 4
GLA_TAU = 16.0
GLA_CHUNK = 64
TOP_K = 4
SWIGLU_ALPHA = 1.702
SWIGLU_LIMIT = 7.0
EPS = 1e-5

LANES = 128
SUBLANES = 8
MOE_ROWS = 512
MOE_HALF = MOE_ROWS // 2
CONV_HALO = 32
COMBINE_PHASES = 4
COMBINE_AHEAD = 2
CAST_ROWS = 128
BULK_DMA_PRIORITY = 1
MIB = 1024 * 1024

F32 = jnp.float32
BF16 = jnp.bfloat16
U32 = jnp.uint32


def _params(semantics, vmem_mib):
    return pltpu.CompilerParams(dimension_semantics=semantics, vmem_limit_bytes=vmem_mib * MIB)


def _pack_halves(lo, hi):
    lo_u = lax.bitcast_convert_type(lo.astype(BF16).astype(F32), U32)
    hi_u = lax.bitcast_convert_type(hi.astype(BF16).astype(F32), U32)
    return (lo_u >> 16) | (hi_u & jnp.uint32(0xFFFF0000))


def _unpack_halves(p):
    lo = lax.bitcast_convert_type(p << 16, F32)
    hi = lax.bitcast_convert_type(p & jnp.uint32(0xFFFF0000), F32)
    return lo, hi


def _rmsnorm_kernel(x_ref, g_ref, o_ref):
    x = x_ref[...]
    y = x * lax.rsqrt(jnp.mean(x * x, axis=-1, keepdims=True) + EPS)
    o_ref[...] = (y * g_ref[...]).astype(o_ref.dtype)


def _rmsnorm(x, g, out_dtype, rows=256):
    t, d = x.shape
    rows = min(rows, t)
    return pl.pallas_call(
        _rmsnorm_kernel,
        grid=(t // rows,),
        in_specs=[pl.BlockSpec((rows, d), lambda i: (i, 0)),
                  pl.BlockSpec((1, d), lambda i: (0, 0))],
        out_specs=pl.BlockSpec((rows, d), lambda i: (i, 0)),
        out_shape=jax.ShapeDtypeStruct((t, d), out_dtype),
        compiler_params=_params(("parallel",), 32),
        name="rmsnorm",
    )(x, g.reshape(1, d))


def _wprep_kernel(a_ref, b_ref, o_ref, low_ref, prev, *, nb_main, lowrank):
    i = pl.program_id(0)
    rows = o_ref.shape[0]

    def straddle(first_ref):
        o_ref[0:rows - lowrank, :] = first_ref[lowrank:rows, :].astype(BF16)
        o_ref[rows - lowrank:rows, :] = b_ref[0:lowrank, :].astype(BF16)
        prev[...] = b_ref[...]

    @pl.when(i < nb_main)
    def _():
        o_ref[...] = a_ref[...].astype(BF16)

    @pl.when(i == nb_main)
    def _():
        low_ref[...] = jnp.zeros_like(low_ref)
        low_ref[0:lowrank, :] = a_ref[0:lowrank, :].astype(BF16)
        straddle(a_ref)

    @pl.when(i > nb_main)
    def _():
        straddle(prev)


def _wprep(wt, c_main, lowrank):
    n_in, k = wt.shape
    rows = 2 * LANES
    n_out = n_in - lowrank
    assert c_main % rows == 0 and n_out % rows == 0 and lowrank % 16 == 0
    last = pl.cdiv(n_in, rows) - 1
    nb_main = c_main // rows
    return pl.pallas_call(
        functools.partial(_wprep_kernel, nb_main=nb_main, lowrank=lowrank),
        grid=(n_out // rows,),
        in_specs=[pl.BlockSpec((rows, k), lambda i: (jnp.minimum(i, nb_main), 0)),
                  pl.BlockSpec((rows, k), lambda i: (jnp.minimum(jnp.maximum(i, nb_main) + 1, last), 0))],
        out_specs=[pl.BlockSpec((rows, k), lambda i: (i, 0)),
                   pl.BlockSpec((LANES, k), lambda i: (0, 0))],
        out_shape=[jax.ShapeDtypeStruct((n_out, k), BF16), jax.ShapeDtypeStruct((LANES, k), BF16)],
        scratch_shapes=[pltpu.VMEM((rows, k), F32)],
        compiler_params=_params(("arbitrary",), 32),
        name="inproj_weight_prep",
    )(wt, wt)


def _mm_nt_kernel(a_ref, wt_ref, o_ref):
    o_ref[...] = lax.dot_general(a_ref[...], wt_ref[...], (((1,), (1,)), ((), ())), preferred_element_type=F32)


def _matmul_nt(a, wt, name, tm=1024, tn=1024):
    m, k = a.shape
    n = wt.shape[0]
    tm, tn = min(tm, m), min(tn, n)
    return pl.pallas_call(
        _mm_nt_kernel,
        grid=(m // tm, n // tn),
        in_specs=[pl.BlockSpec((tm, k), lambda i, j: (i, 0)),
                  pl.BlockSpec((tn, k), lambda i, j: (j, 0))],
        out_specs=pl.BlockSpec((tm, tn), lambda i, j: (i, j)),
        out_shape=jax.ShapeDtypeStruct((m, n), F32),
        compiler_params=_params(("parallel", "arbitrary"), 48),
        name=name,
    )(a, wt)


def _mm_kernel(*refs, n_a, has_res):
    a_refs, w_ref = refs[:n_a], refs[n_a]
    res_ref = refs[n_a + 1] if has_res else None
    o_ref, wb = refs[-2], refs[-1]

    @pl.when(pl.program_id(1) == 0)
    def _():
        _cast_rows(w_ref, wb, CAST_ROWS)

    acc, k0 = None, 0
    for a_ref in a_refs:
        kk = a_ref.shape[1]
        part = jnp.dot(a_ref[...], wb[k0:k0 + kk, :], preferred_element_type=F32)
        acc = part if acc is None else acc + part
        k0 += kk
    if has_res:
        acc = acc + res_ref[...]
    o_ref[...] = acc.astype(o_ref.dtype)


def _matmul(a_list, w, n_out, name, res=None, tm=1024, tn=512):
    m = a_list[0].shape[0]
    k = w.shape[0]
    assert sum(a.shape[1] for a in a_list) == k
    tm, tn = min(tm, m), min(tn, n_out)
    in_specs = [pl.BlockSpec((tm, a.shape[1]), lambda j, i: (i, 0)) for a in a_list]
    in_specs.append(pl.BlockSpec((k, tn), lambda j, i: (0, j)))
    args = list(a_list) + [w]
    if res is not None:
        in_specs.append(pl.BlockSpec((tm, tn), lambda j, i: (i, j)))
        args.append(res)
    return pl.pallas_call(
        functools.partial(_mm_kernel, n_a=len(a_list), has_res=res is not None),
        grid=(n_out // tn, m // tm),
        in_specs=in_specs,
        out_specs=pl.BlockSpec((tm, tn), lambda j, i: (i, j)),
        out_shape=jax.ShapeDtypeStruct((m, n_out), F32),
        scratch_shapes=[pltpu.VMEM((k, tn), BF16)],
        compiler_params=_params(("parallel", "arbitrary"), 52),
        name=name,
    )(*args)


def _gla_kernel(q_ref, k_ref, v_ref, r_ref, al_ref, wa_ref, ba_ref, g_ref, o_ref, st_ref, *, chunk, scale):
    @pl.when(pl.program_id(2) == 0)
    def _():
        st_ref[...] = jnp.zeros_like(st_ref)

    ts = q_ref.shape[1]
    heads, dv, dk = st_ref.shape
    rows = lax.broadcasted_iota(jnp.int32, (ts, ts), 0)
    cols = lax.broadcasted_iota(jnp.int32, (ts, ts), 1)
    assert chunk & (chunk - 1) == 0
    same_chunk = (rows & -chunk) == (cols & -chunk)
    causal = jnp.logical_and(same_chunk, rows >= cols)
    tril = jnp.where(causal, 1.0, 0.0).astype(BF16)
    ones = jnp.where(same_chunk, 1.0, 0.0).astype(BF16)
    nt = (((1,), (1,)), ((), ()))
    tn = (((0,), (0,)), ((), ()))

    z = jnp.dot(al_ref[0].astype(BF16), wa_ref[...], preferred_element_type=F32) + ba_ref[...]
    la = (jnp.minimum(z, 0.0) - jnp.log1p(jnp.exp(-jnp.abs(z)))) * (1.0 / GLA_TAU)
    la_hi = la.astype(BF16)
    la_lo = (la - la_hi.astype(F32)).astype(BF16)
    bcum_all = jnp.dot(tril, la_hi, preferred_element_type=F32) + jnp.dot(tril, la_lo, preferred_element_type=F32)
    btot_all = jnp.dot(ones, la_hi, preferred_element_type=F32) + jnp.dot(ones, la_lo, preferred_element_type=F32)
    for i in range(heads):
        ks, vs = slice(i * dk, (i + 1) * dk), slice(i * dv, (i + 1) * dv)
        q, k, v = q_ref[0, :, ks], k_ref[0, :, ks], v_ref[0, :, vs]
        bcum, btot = bcum_all[:, ks], btot_all[:, ks]
        q_e = (q * jnp.exp(bcum) * scale).astype(BF16)
        k_e = (k * jnp.exp(-bcum)).astype(BF16)
        k_d = (k * jnp.exp(btot - bcum)).astype(BF16)
        decay = jnp.exp(btot)
        vb = v.astype(BF16)
        s = lax.dot_general(q_e, k_e, nt, preferred_element_type=F32)
        s = jnp.where(causal, s, 0.0).astype(BF16)
        o_intra = jnp.dot(s, vb, preferred_element_type=F32)

        st = st_ref[i]
        outs = []
        for c in range(ts // chunk):
            lo, hi = c * chunk, (c + 1) * chunk
            outs.append(o_intra[lo:hi]
                        + lax.dot_general(q_e[lo:hi], st.astype(BF16), nt, preferred_element_type=F32))
            st = st * decay[lo:lo + 1] + lax.dot_general(vb[lo:hi], k_d[lo:hi], tn, preferred_element_type=F32)
        st_ref[i] = st
        o = jnp.concatenate(outs, axis=0)
        o = o * lax.rsqrt(jnp.mean(o * o, axis=-1, keepdims=True) + EPS) * g_ref[...]
        r = r_ref[0, :, vs]
        o_ref[0, :, vs] = (o * (r * jax.nn.sigmoid(r))).astype(o_ref.dtype)


def _gla(proj, a_low, wa, ba, g, dk, dv, ts=256, heads_per_step=4):
    b, s, _ = proj.shape
    hp = heads_per_step
    hg = GLA_HEADS // hp
    ts = min(ts, s)
    wk, wv = hp * dk, hp * dv
    kq, kv = (GLA_HEADS * dk) // wk, (2 * GLA_HEADS * dk) // wv
    return pl.pallas_call(
        functools.partial(_gla_kernel, chunk=GLA_CHUNK, scale=dk ** -0.5),
        grid=(b, hg, s // ts),
        in_specs=[pl.BlockSpec((1, ts, wk), lambda bi, hi, si: (bi, si, hi)),
                  pl.BlockSpec((1, ts, wk), lambda bi, hi, si: (bi, si, kq + hi)),
                  pl.BlockSpec((1, ts, wv), lambda bi, hi, si: (bi, si, kv + hi)),
                  pl.BlockSpec((1, ts, wv), lambda bi, hi, si: (bi, si, kv + hg + hi)),
                  pl.BlockSpec((1, ts, LANES), lambda bi, hi, si: (bi, si, 0)),
                  pl.BlockSpec((LANES, wk), lambda bi, hi, si: (0, hi)),
                  pl.BlockSpec((1, wk), lambda bi, hi, si: (0, hi)),
                  pl.BlockSpec((1, dv), lambda bi, hi, si: (0, 0))],
        out_specs=pl.BlockSpec((1, ts, wv), lambda bi, hi, si: (bi, si, hi)),
        out_shape=jax.ShapeDtypeStruct((b, s, GLA_HEADS * dv), BF16),
        scratch_shapes=[pltpu.VMEM((hp, dv, dk), F32)],
        compiler_params=_params(("parallel", "parallel", "arbitrary"), 40),
        name="gla",
    )(proj, proj, proj, proj, a_low, wa, ba, g)


def _conv_kernel(a_ref, b_ref, bga_ref, bgb_ref, w_ref, bdw_ref, lg_ref, lb_ref, o_ref, ubuf, sbuf, cbuf, *,
                 width, cw, rc):
    ts, dc = a_ref.shape[1], a_ref.shape[2]
    si = pl.program_id(1)

    @pl.when(si == 0)
    def _():
        ubuf[0:CONV_HALO, :] = jnp.zeros((CONV_HALO, dc), F32)

    @pl.when(si > 0)
    def _():
        ubuf[0:CONV_HALO, :] = ubuf[ts:ts + CONV_HALO, :]

    ubuf[CONV_HALO:CONV_HALO + ts, :] = (a_ref[0] + bga_ref[...]) * jax.nn.sigmoid(b_ref[0] + bgb_ref[...])

    span = ts + CONV_HALO - SUBLANES

    def col_body(cb, carry):
        cs = pl.ds(pl.multiple_of(cb * cw, cw), cw)
        for b in range(1, SUBLANES):
            sbuf[b, 0:span, :] = ubuf[b:b + span, cs]
        for rb in range(ts // rc):
            acc = jnp.broadcast_to(bdw_ref[:, cs], (rc, cw))
            for j in range(width):
                off = CONV_HALO - (width - 1) + j
                b = off % SUBLANES
                r0 = off - b + rb * rc
                src = ubuf[r0:r0 + rc, cs] if b == 0 else sbuf[b, r0:r0 + rc, :]
                acc = acc + w_ref[j:j + 1, cs] * src
            cbuf[rb * rc:(rb + 1) * rc, cs] = acc
        return carry

    lax.fori_loop(0, dc // cw, col_body, 0)
    c = cbuf[...]
    mu = jnp.mean(c, axis=-1, keepdims=True)
    cen = c - mu
    var = jnp.mean(cen * cen, axis=-1, keepdims=True)
    un = cen * lax.rsqrt(var + EPS) * lg_ref[...] + lb_ref[...]
    o_ref[0] = (un * jax.nn.sigmoid(un)).astype(o_ref.dtype)


def _conv_module(proj, col0, dc, b_glu, w_dw, b_dw, ln_g, ln_b, ts=256):
    b, s, _ = proj.shape
    ts = min(ts, s)
    width = w_dw.shape[0]
    cw = min(128, dc)
    assert width - 1 <= CONV_HALO <= ts and col0 % dc == 0
    cblk = col0 // dc
    row = lambda v: v.reshape(1, dc)
    vec = pl.BlockSpec((1, dc), lambda bi, si: (0, 0))
    return pl.pallas_call(
        functools.partial(_conv_kernel, width=width, cw=cw, rc=min(128, ts)),
        grid=(b, s // ts),
        in_specs=[pl.BlockSpec((1, ts, dc), lambda bi, si: (bi, si, cblk)),
                  pl.BlockSpec((1, ts, dc), lambda bi, si: (bi, si, cblk + 1)),
                  vec, vec,
                  pl.BlockSpec((width, dc), lambda bi, si: (0, 0)),
                  vec, vec, vec],
        out_specs=pl.BlockSpec((1, ts, dc), lambda bi, si: (bi, si, 0)),
        out_shape=jax.ShapeDtypeStruct((b, s, dc), BF16),
        scratch_shapes=[pltpu.VMEM((CONV_HALO + ts, dc), F32),
                        pltpu.VMEM((SUBLANES, CONV_HALO + ts, cw), F32),
                        pltpu.VMEM((ts, dc), F32)],
        compiler_params=_params(("parallel", "arbitrary"), 32),
        name="conv_module",
    )(proj, proj, row(b_glu[:dc]), row(b_glu[dc:]), w_dw, row(b_dw), row(ln_g), row(ln_b))


def _router_kernel(h_ref, g_ref, wr_ref, br_ref, n_ref, idx_ref, gate_ref, *, n_exp):
    h = h_ref[...]
    d2 = h.shape[1] // 2
    n = h * lax.rsqrt(jnp.mean(h * h, axis=-1, keepdims=True) + EPS) * g_ref[...]
    n_ref[...] = _pack_halves(n[:, :d2], n[:, d2:])
    logits = jnp.dot(n.astype(BF16), wr_ref[...], preferred_element_type=F32) + br_ref[...]
    lane = lax.broadcasted_iota(jnp.int32, logits.shape, 1)
    lane_f = lane.astype(F32)
    neg = jnp.float32(-jnp.inf)
    l = jnp.where(lane < n_exp, logits, neg)
    vals, idxs = [], []
    for _ in range(TOP_K):
        m = jnp.max(l, axis=-1, keepdims=True)
        i = jnp.min(jnp.where(l == m, lane_f, float(LANES)), axis=-1, keepdims=True)
        vals.append(m)
        idxs.append(i)
        l = jnp.where(lane_f == i, neg, l)
    es = [jnp.exp(v - vals[0]) for v in vals]
    tot = es[0]
    for e in es[1:]:
        tot = tot + e
    idx_out = jnp.zeros(logits.shape, F32)
    gate_out = jnp.zeros(logits.shape, F32)
    for kk in range(TOP_K):
        idx_out = jnp.where(lane == kk, idxs[kk], idx_out)
        gate_out = jnp.where(lane == kk, es[kk] / tot, gate_out)
    idx_ref[...] = idx_out.astype(jnp.int32)
    gate_ref[...] = gate_out


def _router(h, g, wr, br, n_exp, rows=256):
    t, d = h.shape
    rows = min(rows, t)
    blk = lambda w: pl.BlockSpec((rows, w), lambda i: (i, 0))
    return pl.pallas_call(
        functools.partial(_router_kernel, n_exp=n_exp),
        grid=(t // rows,),
        in_specs=[blk(d),
                  pl.BlockSpec((1, d), lambda i: (0, 0)),
                  pl.BlockSpec((d, LANES), lambda i: (0, 0)),
                  pl.BlockSpec((1, LANES), lambda i: (0, 0))],
        out_specs=[blk(d // 2), blk(LANES), blk(LANES)],
        out_shape=[jax.ShapeDtypeStruct((t, d // 2), U32),
                   jax.ShapeDtypeStruct((t, LANES), jnp.int32),
                   jax.ShapeDtypeStruct((t, LANES), F32)],
        compiler_params=_params(("parallel",), 32),
        name="ffn_norm_router",
    )(h, g.reshape(1, d), wr, br)


def _dispatch_kernel(zf_ref, zl_ref, nu_ref, slot_ref, n_ref, xs_hbm, zbuf, zsem, sem, *, rows):
    i = pl.program_id(0)
    n_halves = xs_hbm.shape[0] // MOE_HALF
    first_unused = nu_ref[0] * (MOE_ROWS // MOE_HALF)

    def zero_copy(row, sem_i):
        return pltpu.make_async_copy(zbuf, xs_hbm.at[pl.ds(pl.multiple_of(row, MOE_HALF), MOE_HALF)],
                                     zsem.at[sem_i])

    def for_each(z_ref, sem_i, act):
        for z in range(z_ref.shape[0]):
            @pl.when(z_ref[z] >= 0)
            def _():
                act(zero_copy(z_ref[z], sem_i))

    def for_unused(act):
        def body(hi, carry):
            act(zero_copy(hi * MOE_HALF, 1))
            return carry
        lax.fori_loop(first_unused, n_halves, body, 0)

    @pl.when(i == 0)
    def _():
        zbuf[...] = jnp.zeros_like(zbuf)
        for_each(zf_ref, 0, lambda c: c.start())
        for_each(zl_ref, 1, lambda c: c.start())
        for_unused(lambda c: c.start())
        for_each(zf_ref, 0, lambda c: c.wait())

    def row_copy(r, slot):
        return pltpu.make_async_copy(n_ref.at[pl.ds(r, 1)], xs_hbm.at[pl.ds(slot, 1)], sem)

    def start(r, carry):
        for kk in range(TOP_K):
            row_copy(r, slot_ref[0, 0, r * TOP_K + kk]).start(priority=kk % 2)
        return carry

    lax.fori_loop(0, rows, start, 0)
    for kk in range(TOP_K):
        pltpu.make_async_copy(n_ref, xs_hbm.at[pl.ds(0, rows)], sem).wait()

    @pl.when(i == pl.num_programs(0) - 1)
    def _():
        for_each(zl_ref, 1, lambda c: c.wait())
        for_unused(lambda c: c.wait())


def _dispatch(n_packed, slot, zero_first, zero_later, n_used, n_slots, rows=512):
    t, d2 = n_packed.shape
    rows = min(rows, t)
    grid_spec = pltpu.PrefetchScalarGridSpec(
        num_scalar_prefetch=3,
        grid=(t // rows,),
        in_specs=[pl.BlockSpec((1, 1, rows * TOP_K), lambda i, zf, zl, nu: (i, 0, 0), memory_space=pltpu.SMEM),
                  pl.BlockSpec((rows, d2), lambda i, zf, zl, nu: (i, 0))],
        out_specs=pl.BlockSpec(memory_space=pl.ANY),
        scratch_shapes=[pltpu.VMEM((MOE_HALF, d2), U32), pltpu.SemaphoreType.DMA((2,)),
                        pltpu.SemaphoreType.DMA(())],
    )
    return pl.pallas_call(
        functools.partial(_dispatch_kernel, rows=rows),
        grid_spec=grid_spec,
        out_shape=jax.ShapeDtypeStruct((n_slots, d2), U32),
        compiler_params=_params(("arbitrary",), 32),
        name="moe_dispatch",
    )(zero_first, zero_later, n_used, slot.reshape(t // rows, 1, rows * TOP_K), n_packed)


def _block_state(be_ref, nu_ref, bi):
    used = bi < nu_ref[0]
    fresh = jnp.logical_or(bi == 0, be_ref[bi] != be_ref[jnp.maximum(bi - 1, 0)])
    return used, jnp.logical_and(used, fresh)


def _run_weights(copies, first, fresh, cast, then):
    first_cond, first_args = first

    @pl.when(first_cond)
    def _():
        for c in copies(*first_args):
            c.start(priority=BULK_DMA_PRIORITY)

    @pl.when(fresh)
    def _():
        for c in copies(*first_args):
            c.wait()
        cast()
        for cond, args in then:
            @pl.when(cond)
            def _():
                for c in copies(*args):
                    c.start(priority=BULK_DMA_PRIORITY)


def _cast_rows(src, dst, rows):
    def body(i, carry):
        r = pl.ds(pl.multiple_of(i * rows, rows), rows)
        dst[r, :] = src[r, :].astype(BF16)
        return carry
    lax.fori_loop(0, src.shape[0] // rows, body, 0)


def _per_block_halves(halves, compute, o_ref):
    @pl.when(halves == 2)
    def _():
        compute(MOE_ROWS)

    @pl.when(halves == 1)
    def _():
        compute(MOE_HALF)
        o_ref[MOE_HALF:MOE_ROWS, :] = jnp.zeros((MOE_ROWS - MOE_HALF, o_ref.shape[1]), o_ref.dtype)

    @pl.when(halves == 0)
    def _():
        o_ref[...] = jnp.zeros_like(o_ref)


def _moe_up_kernel(be_ref, nu_ref, nx_ref, bh_ref, x_ref, bg_ref, bl_ref, w_hbm, o_ref, sg, sl, wgb, wlb, sem, *,
                   nf):
    fi, bi = pl.program_id(0), pl.program_id(1)
    _, fresh = _block_state(be_ref, nu_ref, bi)
    tf = sg.shape[1]
    f_all = w_hbm.shape[2] // 2

    def copies(e, f):
        c0 = pl.multiple_of(f * tf, tf)
        return (pltpu.make_async_copy(w_hbm.at[e, :, pl.ds(c0, tf)], sg, sem.at[0]),
                pltpu.make_async_copy(w_hbm.at[e, :, pl.ds(f_all + c0, tf)], sl, sem.at[1]))

    def cast():
        _cast_rows(sg, wgb, CAST_ROWS)
        _cast_rows(sl, wlb, CAST_ROWS)

    nx = nx_ref[bi]
    _run_weights(copies, (jnp.logical_and(fi == 0, bi == 0), (be_ref[0], 0)), fresh, cast,
                 [(nx >= 0, (nx, fi)),
                  (jnp.logical_and(nx < 0, fi + 1 < nf), (be_ref[0], fi + 1))])

    def compute(rows):
        d2 = x_ref.shape[1]
        lo, hi = _unpack_halves(x_ref[0:rows, :])
        lo, hi = lo.astype(BF16), hi.astype(BF16)

        def proj(wb, b_ref):
            return (jnp.dot(lo, wb[0:d2, :], preferred_element_type=F32)
                    + jnp.dot(hi, wb[d2:2 * d2, :], preferred_element_type=F32) + b_ref[0])

        xg = jnp.minimum(proj(wgb, bg_ref), SWIGLU_LIMIT)
        xl = jnp.clip(proj(wlb, bl_ref), -SWIGLU_LIMIT, SWIGLU_LIMIT)
        o_ref[0:rows, :] = (xg * jax.nn.sigmoid(SWIGLU_ALPHA * xg) * (xl + 1.0)).astype(o_ref.dtype)

    _per_block_halves(bh_ref[bi], compute, o_ref)


def _moe_up(xs, block_expert, n_used, next_expert, block_halves, w1, b1, tf=512):
    n_slots, d2 = xs.shape
    e, d, f2 = w1.shape
    f = f2 // 2
    tf = min(tf, f)
    nf = f // tf
    nb = n_slots // MOE_ROWS
    blk = lambda bi, nu: jnp.minimum(bi, nu[0] - 1)
    grid_spec = pltpu.PrefetchScalarGridSpec(
        num_scalar_prefetch=4,
        grid=(nf, nb),
        in_specs=[pl.BlockSpec((MOE_ROWS, d2), lambda fi, bi, be, nu, nx, bh: (blk(bi, nu), 0)),
                  pl.BlockSpec((1, 1, tf), lambda fi, bi, be, nu, nx, bh: (be[blk(bi, nu)], 0, fi)),
                  pl.BlockSpec((1, 1, tf), lambda fi, bi, be, nu, nx, bh: (be[blk(bi, nu)], 0, nf + fi)),
                  pl.BlockSpec(memory_space=pl.ANY)],
        out_specs=pl.BlockSpec((MOE_ROWS, tf), lambda fi, bi, be, nu, nx, bh: (bi, fi)),
        scratch_shapes=[pltpu.VMEM((d, tf), F32), pltpu.VMEM((d, tf), F32),
                        pltpu.VMEM((d, tf), BF16), pltpu.VMEM((d, tf), BF16),
                        pltpu.SemaphoreType.DMA((2,))],
    )
    return pl.pallas_call(
        functools.partial(_moe_up_kernel, nf=nf),
        grid_spec=grid_spec,
        out_shape=jax.ShapeDtypeStruct((n_slots, f), BF16),
        compiler_params=_params(("arbitrary", "arbitrary"), 48),
        name="moe_up",
    )(block_expert, n_used, next_expert, block_halves, xs, b1.reshape(e, 1, f2), b1.reshape(e, 1, f2), w1)


def _moe_down_kernel(be_ref, nu_ref, nx_ref, bh_ref, h_ref, b_ref, w_hbm, o_ref, sw, wb, sem):
    bi = pl.program_id(0)
    _, fresh = _block_state(be_ref, nu_ref, bi)

    def copies(e):
        return (pltpu.make_async_copy(w_hbm.at[e], sw, sem.at[0]),)

    def cast():
        wb[...] = sw[...].astype(BF16)

    nx = nx_ref[bi]
    _run_weights(copies, (bi == 0, (be_ref[0],)), fresh, cast, [(nx >= 0, (nx,))])

    def compute(rows):
        d2 = o_ref.shape[1]
        h = h_ref[0:rows, :]
        y_lo = jnp.dot(h, wb[:, 0:d2], preferred_element_type=F32) + b_ref[0, :, 0:d2]
        y_hi = jnp.dot(h, wb[:, d2:2 * d2], preferred_element_type=F32) + b_ref[0, :, d2:2 * d2]
        o_ref[0:rows, :] = _pack_halves(y_lo, y_hi)

    _per_block_halves(bh_ref[bi], compute, o_ref)


def _moe_down(hid, block_expert, n_used, next_expert, block_halves, w2, b2):
    n_slots, f = hid.shape
    e, _, d = w2.shape
    nb = n_slots // MOE_ROWS
    blk = lambda bi, nu: jnp.minimum(bi, nu[0] - 1)
    grid_spec = pltpu.PrefetchScalarGridSpec(
        num_scalar_prefetch=4,
        grid=(nb,),
        in_specs=[pl.BlockSpec((MOE_ROWS, f), lambda bi, be, nu, nx, bh: (blk(bi, nu), 0)),
                  pl.BlockSpec((1, 1, d), lambda bi, be, nu, nx, bh: (be[blk(bi, nu)], 0, 0)),
                  pl.BlockSpec(memory_space=pl.ANY)],
        out_specs=pl.BlockSpec((MOE_ROWS, d // 2), lambda bi, be, nu, nx, bh: (bi, 0)),
        scratch_shapes=[pltpu.VMEM((f, d), F32), pltpu.VMEM((f, d), BF16), pltpu.SemaphoreType.DMA((1,))],
    )
    return pl.pallas_call(
        _moe_down_kernel,
        grid_spec=grid_spec,
        out_shape=jax.ShapeDtypeStruct((n_slots, d // 2), U32),
        compiler_params=_params(("arbitrary",), 48),
        name="moe_down",
    )(block_expert, n_used, next_expert, block_halves, hid, b2.reshape(e, 1, d), w2)


def _combine_kernel(slot_ref, nslot_ref, gate_ref, h_ref, g_ref, y_hbm, o_ref, *scratch, rows):
    bufs, sem = scratch[:COMBINE_PHASES], scratch[COMBINE_PHASES]
    i = pl.program_id(0)
    d2 = bufs[0].shape[2]
    group = SUBLANES

    def start_rows(s_ref, base, buf, sem_i, r0):
        for r in range(r0, r0 + group):
            for kk in range(TOP_K):
                pltpu.make_async_copy(y_hbm.at[pl.ds(s_ref[0, 0, (base + r) * TOP_K + kk], 1)],
                                      buf.at[kk, pl.ds(r, 1)], sem.at[sem_i]).start(priority=kk % 2)

    def wait_all(buf, sem_i):
        for kk in range(TOP_K):
            pltpu.make_async_copy(y_hbm.at[pl.ds(0, rows)], buf.at[kk], sem.at[sem_i]).wait()

    def compute_rows(buf, base, r0):
        tok = slice(base + r0, base + r0 + group)
        gates = gate_ref[tok, :]
        acc_lo, acc_hi = h_ref[tok, 0:d2], h_ref[tok, d2:2 * d2]
        for kk in range(TOP_K):
            lo, hi = _unpack_halves(buf[kk, r0:r0 + group, :])
            gk = gates[:, kk:kk + 1]
            acc_lo = acc_lo + gk * lo
            acc_hi = acc_hi + gk * hi
        ms = (jnp.sum(acc_lo * acc_lo, axis=-1, keepdims=True)
              + jnp.sum(acc_hi * acc_hi, axis=-1, keepdims=True)) * (1.0 / (2 * d2))
        inv = lax.rsqrt(ms + EPS)
        o_ref[tok, 0:d2] = acc_lo * inv * g_ref[:, 0:d2]
        o_ref[tok, d2:2 * d2] = acc_hi * inv * g_ref[:, d2:2 * d2]

    @pl.when(i == 0)
    def _():
        for p in range(COMBINE_AHEAD):
            def body(r, carry, p=p):
                for kk in range(TOP_K):
                    pltpu.make_async_copy(y_hbm.at[pl.ds(slot_ref[0, 0, (p * rows + r) * TOP_K + kk], 1)],
                                          bufs[p].at[kk, pl.ds(r, 1)], sem.at[p]).start(priority=kk % 2)
                return carry
            lax.fori_loop(0, rows, body, 0)

    for p in range(COMBINE_PHASES):
        wait_all(bufs[p], p)
        q = p + COMBINE_AHEAD
        s_ref, qq = (slot_ref, q) if q < COMBINE_PHASES else (nslot_ref, q - COMBINE_PHASES)
        for r0 in range(0, rows, group):
            start_rows(s_ref, qq * rows, bufs[qq], qq, r0)
            compute_rows(bufs[p], p * rows, r0)

    @pl.when(i == pl.num_programs(0) - 1)
    def _():
        for p in range(COMBINE_AHEAD):
            wait_all(bufs[p], p)


def _combine(y_packed, slot, gates, h, g, rows=64):
    t, d = h.shape
    rows = min(rows, t // COMBINE_PHASES)
    block = COMBINE_PHASES * rows
    nsteps = t // block
    slot3 = slot.reshape(nsteps, 1, block * TOP_K)
    return pl.pallas_call(
        functools.partial(_combine_kernel, rows=rows),
        grid=(nsteps,),
        in_specs=[pl.BlockSpec((1, 1, block * TOP_K), lambda i: (i, 0, 0), memory_space=pltpu.SMEM),
                  pl.BlockSpec((1, 1, block * TOP_K), lambda i: (jnp.minimum(i + 1, nsteps - 1), 0, 0),
                               memory_space=pltpu.SMEM),
                  pl.BlockSpec((block, LANES), lambda i: (i, 0)),
                  pl.BlockSpec((block, d), lambda i: (i, 0)),
                  pl.BlockSpec((1, d), lambda i: (0, 0)),
                  pl.BlockSpec(memory_space=pl.ANY)],
        out_specs=pl.BlockSpec((block, d), lambda i: (i, 0)),
        out_shape=jax.ShapeDtypeStruct((t, d), F32),
        scratch_shapes=([pltpu.VMEM((TOP_K, rows, d // 2), U32) for _ in range(COMBINE_PHASES)]
                        + [pltpu.SemaphoreType.DMA((COMBINE_PHASES,))]),
        compiler_params=_params(("arbitrary",), 40),
        name="moe_combine",
    )(slot3, slot3, gates, h, g.reshape(1, d), y_packed)


def _routing(top_idx, n_exp):
    t = top_idx.shape[0]
    experts = jnp.arange(n_exp, dtype=jnp.int32)
    onehot = (top_idx[:, :, None] == experts[None, None, :]).any(axis=1).astype(jnp.int32)
    incl = jnp.cumsum(onehot, axis=0)
    counts = incl[-1]
    padded = (counts + MOE_ROWS - 1) // MOE_ROWS * MOE_ROWS
    pad_ends = jnp.cumsum(padded)
    slot = jnp.take_along_axis(incl - onehot + (pad_ends - padded)[None, :], top_idx, axis=1)
    n_blocks = -(-(t * TOP_K) // MOE_ROWS) + n_exp
    block_rows = jnp.arange(n_blocks, dtype=jnp.int32) * MOE_ROWS
    block_expert = jnp.minimum(jnp.sum(pad_ends[None, :] <= block_rows[:, None], axis=1), n_exp - 1)
    n_used = (pad_ends[-1] // MOE_ROWS).reshape(1)
    rows_in_block = jnp.clip((counts - padded + pad_ends)[block_expert] - block_rows, 0, MOE_ROWS)
    block_halves = jnp.where(block_rows < pad_ends[-1], jnp.where(rows_in_block <= MOE_HALF, 1, 2), 0)
    last_rows = counts - padded + MOE_ROWS
    live = counts > 0
    zero_first = jnp.concatenate([
        jnp.where(live & (last_rows > MOE_HALF) & (last_rows < MOE_ROWS), pad_ends - MOE_HALF, -1),
        jnp.where(live & (last_rows < MOE_HALF), pad_ends - MOE_ROWS, -1)])
    zero_later = jnp.where(live & (last_rows <= MOE_HALF), pad_ends - MOE_HALF, -1)
    later = jnp.logical_and(experts[None, :] > experts[:, None], padded[None, :] > 0)
    next_of = jnp.min(jnp.where(later, experts[None, :], n_exp), axis=1)
    next_expert = jnp.where(next_of < n_exp, next_of, -1)[block_expert]
    i32 = lambda a: a.astype(jnp.int32)
    return (i32(slot), i32(block_expert), i32(n_used), i32(next_expert), i32(block_halves), i32(zero_first),
            i32(zero_later), n_blocks * MOE_ROWS)


def kernel(x, attn_norm_g, w_in, w_a2, b_a, gla_norm_g, b_glu, w_dw, b_dw, conv_ln_g, conv_ln_b, w_out,
           ffn_norm_g, w_router, b_router, w1, b1, w2, b2, final_norm_g):
    b, s, d = x.shape
    assert w_in.shape[0] == 1, "single-layer stack"
    t = b * s
    dv = gla_norm_g.shape[1]
    dk = w_a2.shape[2] // GLA_HEADS
    lowrank = w_a2.shape[1]
    dc = w_dw.shape[2]
    n_exp = w_router.shape[2]
    c_qkvr = 2 * GLA_HEADS * (dk + dv)
    assert lowrank <= LANES and n_exp <= LANES
    x2 = x.reshape(t, d)

    n1 = _rmsnorm(x2, attn_norm_g[0], BF16)
    w_main, w_low = _wprep(jnp.swapaxes(w_in[0], 0, 1), c_qkvr, lowrank)
    proj = _matmul_nt(n1, w_main, "inproj").reshape(b, s, -1)
    a_low = _matmul_nt(n1, w_low, "inproj_lowrank").reshape(b, s, LANES)

    wa = jnp.pad(w_a2[0], ((0, LANES - lowrank), (0, 0))).astype(BF16)
    gla_out = _gla(proj, a_low, wa, b_a[0].reshape(1, -1), gla_norm_g[0].reshape(1, dv), dk, dv)
    conv_out = _conv_module(proj, c_qkvr, dc, b_glu[0], w_dw[0], b_dw[0], conv_ln_g[0], conv_ln_b[0])
    h = _matmul([gla_out.reshape(t, -1), conv_out.reshape(t, dc)], w_out[0], d, "outproj", res=x2)

    wr = jnp.pad(w_router[0], ((0, 0), (0, LANES - n_exp))).astype(BF16)
    br = jnp.pad(b_router[0], (0, LANES - n_exp)).reshape(1, LANES)
    n2, idx_pad, gate_pad = _router(h, ffn_norm_g[0], wr, br, n_exp)
    slot, block_expert, n_used, next_expert, block_halves, zero_first, zero_later, n_slots = _routing(
        idx_pad[:, :TOP_K], n_exp)
    xs = _dispatch(n2, slot, zero_first, zero_later, n_used, n_slots)
    hid = _moe_up(xs, block_expert, n_used, next_expert, block_halves, w1[0], b1[0])
    y = _moe_down(hid, block_expert, n_used, next_expert, block_halves, w2[0], b2[0])
    out = _combine(y, slot, gate_pad, h, final_norm_g)
    return out.reshape(b, s, d)
```

```python
import functools

import jax
import jax.numpy as jnp
from jax import lax
from jax.experimental import pallas as pl
from jax.experimental.pallas import tpu as pltpu

GLA_HEADS = 4
GLA_TAU = 16.0
GLA_CHUNK = 64
TOP_K = 4
SWIGLU_ALPHA = 1.702
SWIGLU_LIMIT = 7.0
EPS = 1e-5

LANES = 128
SUBLANES = 8
MOE_ROWS = 512
MOE_HALF = MOE_ROWS // 2
CONV_HALO = 32
COMBINE_PHASES = 4
COMBINE_AHEAD = 2
CAST_ROWS = 128
BULK_DMA_PRIORITY = 1
MIB = 1024 * 1024

F32 = jnp.float32
BF16 = jnp.bfloat16
U32 = jnp.uint32


def _params(semantics, vmem_mib):
    return pltpu.CompilerParams(dimension_semantics=semantics, vmem_limit_bytes=vmem_mib * MIB)


def _pack_halves(lo, hi):
    lo_u = lax.bitcast_convert_type(lo.astype(BF16).astype(F32), U32)
    hi_u = lax.bitcast_convert_type(hi.astype(BF16).astype(F32), U32)
    return (lo_u >> 16) | (hi_u & jnp.uint32(0xFFFF0000))


def _unpack_halves(p):
    lo = lax.bitcast_convert_type(p << 16, F32)
    hi = lax.bitcast_convert_type(p & jnp.uint32(0xFFFF0000), F32)
    return lo, hi


def _rmsnorm_kernel(x_ref, g_ref, o_ref):
    x = x_ref[...]
    y = x * lax.rsqrt(jnp.mean(x * x, axis=-1, keepdims=True) + EPS)
    o_ref[...] = (y * g_ref[...]).astype(o_ref.dtype)


def _rmsnorm(x, g, out_dtype, rows=256):
    t, d = x.shape
    rows = min(rows, t)
    return pl.pallas_call(
        _rmsnorm_kernel,
        grid=(t // rows,),
        in_specs=[pl.BlockSpec((rows, d), lambda i: (i, 0)),
                  pl.BlockSpec((1, d), lambda i: (0, 0))],
        out_specs=pl.BlockSpec((rows, d), lambda i: (i, 0)),
        out_shape=jax.ShapeDtypeStruct((t, d), out_dtype),
        compiler_params=_params(("parallel",), 32),
        name="rmsnorm",
    )(x, g.reshape(1, d))


def _wprep_kernel(a_ref, b_ref, o_ref, low_ref, prev, *, nb_main, lowrank):
    i = pl.program_id(0)
    rows = o_ref.shape[0]

    def straddle(first_ref):
        o_ref[0:rows - lowrank, :] = first_ref[lowrank:rows, :].astype(BF16)
        o_ref[rows - lowrank:rows, :] = b_ref[0:lowrank, :].astype(BF16)
        prev[...] = b_ref[...]

    @pl.when(i < nb_main)
    def _():
        o_ref[...] = a_ref[...].astype(BF16)

    @pl.when(i == nb_main)
    def _():
        low_ref[...] = jnp.zeros_like(low_ref)
        low_ref[0:lowrank, :] = a_ref[0:lowrank, :].astype(BF16)
        straddle(a_ref)

    @pl.when(i > nb_main)
    def _():
        straddle(prev)


def _wprep(wt, c_main, lowrank):
    n_in, k = wt.shape
    rows = 2 * LANES
    n_out = n_in - lowrank
    assert c_main % rows == 0 and n_out % rows == 0 and lowrank % 16 == 0
    last = pl.cdiv(n_in, rows) - 1
    nb_main = c_main // rows
    return pl.pallas_call(
        functools.partial(_wprep_kernel, nb_main=nb_main, lowrank=lowrank),
        grid=(n_out // rows,),
        in_specs=[pl.BlockSpec((rows, k), lambda i: (jnp.minimum(i, nb_main), 0)),
                  pl.BlockSpec((rows, k), lambda i: (jnp.minimum(jnp.maximum(i, nb_main) + 1, last), 0))],
        out_specs=[pl.BlockSpec((rows, k), lambda i: (i, 0)),
                   pl.BlockSpec((LANES, k), lambda i: (0, 0))],
        out_shape=[jax.ShapeDtypeStruct((n_out, k), BF16), jax.ShapeDtypeStruct((LANES, k), BF16)],
        scratch_shapes=[pltpu.VMEM((rows, k), F32)],
        compiler_params=_params(("arbitrary",), 32),
        name="inproj_weight_prep",
    )(wt, wt)


def _mm_nt_kernel(a_ref, wt_ref, o_ref):
    o_ref[...] = lax.dot_general(a_ref[...], wt_ref[...], (((1,), (1,)), ((), ())), preferred_element_type=F32)


def _matmul_nt(a, wt, name, tm=1024, tn=1024):
    m, k = a.shape
    n = wt.shape[0]
    tm, tn = min(tm, m), min(tn, n)
    return pl.pallas_call(
        _mm_nt_kernel,
        grid=(m // tm, n // tn),
        in_specs=[pl.BlockSpec((tm, k), lambda i, j: (i, 0)),
                  pl.BlockSpec((tn, k), lambda i, j: (j, 0))],
        out_specs=pl.BlockSpec((tm, tn), lambda i, j: (i, j)),
        out_shape=jax.ShapeDtypeStruct((m, n), F32),
        compiler_params=_params(("parallel", "arbitrary"), 48),
        name=name,
    )(a, wt)


def _mm_kernel(*refs, n_a, has_res):
    a_refs, w_ref = refs[:n_a], refs[n_a]
    res_ref = refs[n_a + 1] if has_res else None
    o_ref, wb = refs[-2], refs[-1]

    @pl.when(pl.program_id(1) == 0)
    def _():
        _cast_rows(w_ref, wb, CAST_ROWS)

    acc, k0 = None, 0
    for a_ref in a_refs:
        kk = a_ref.shape[1]
        part = jnp.dot(a_ref[...], wb[k0:k0 + kk, :], preferred_element_type=F32)
        acc = part if acc is None else acc + part
        k0 += kk
    if has_res:
        acc = acc + res_ref[...]
    o_ref[...] = acc.astype(o_ref.dtype)


def _matmul(a_list, w, n_out, name, res=None, tm=1024, tn=512):
    m = a_list[0].shape[0]
    k = w.shape[0]
    assert sum(a.shape[1] for a in a_list) == k
    tm, tn = min(tm, m), min(tn, n_out)
    in_specs = [pl.BlockSpec((tm, a.shape[1]), lambda j, i: (i, 0)) for a in a_list]
    in_specs.append(pl.BlockSpec((k, tn), lambda j, i: (0, j)))
    args = list(a_list) + [w]
    if res is not None:
        in_specs.append(pl.BlockSpec((tm, tn), lambda j, i: (i, j)))
        args.append(res)
    return pl.pallas_call(
        functools.partial(_mm_kernel, n_a=len(a_list), has_res=res is not None),
        grid=(n_out // tn, m // tm),
        in_specs=in_specs,
        out_specs=pl.BlockSpec((tm, tn), lambda j, i: (i, j)),
        out_shape=jax.ShapeDtypeStruct((m, n_out), F32),
        scratch_shapes=[pltpu.VMEM((k, tn), BF16)],
        compiler_params=_params(("parallel", "arbitrary"), 58),
        name=name,
    )(*args)


def _gla_kernel(q_ref, k_ref, v_ref, r_ref, al_ref, wa_ref, ba_ref, g_ref, o_ref, st_ref, *, chunk, scale):
    @pl.when(pl.program_id(2) == 0)
    def _():
        st_ref[...] = jnp.zeros_like(st_ref)

    ts = q_ref.shape[1]
    heads, dv, dk = st_ref.shape
    rows = lax.broadcasted_iota(jnp.int32, (ts, ts), 0)
    cols = lax.broadcasted_iota(jnp.int32, (ts, ts), 1)
    assert chunk & (chunk - 1) == 0
    same_chunk = (rows & -chunk) == (cols & -chunk)
    causal = jnp.logical_and(same_chunk, rows >= cols)
    tril = jnp.where(causal, 1.0, 0.0).astype(BF16)
    ones = jnp.where(same_chunk, 1.0, 0.0).astype(BF16)
    nt = (((1,), (1,)), ((), ()))
    tn = (((0,), (0,)), ((), ()))

    z = jnp.dot(al_ref[0].astype(BF16), wa_ref[...], preferred_element_type=F32) + ba_ref[...]
    la = (jnp.minimum(z, 0.0) - jnp.log1p(jnp.exp(-jnp.abs(z)))) * (1.0 / GLA_TAU)
    la_hi = la.astype(BF16)
    la_lo = (la - la_hi.astype(F32)).astype(BF16)
    bcum_all = jnp.dot(tril, la_hi, preferred_element_type=F32) + jnp.dot(tril, la_lo, preferred_element_type=F32)
    btot_all = jnp.dot(ones, la_hi, preferred_element_type=F32) + jnp.dot(ones, la_lo, preferred_element_type=F32)
    for i in range(heads):
        ks, vs = slice(i * dk, (i + 1) * dk), slice(i * dv, (i + 1) * dv)
        q, k, v = q_ref[0, :, ks], k_ref[0, :, ks], v_ref[0, :, vs]
        bcum, btot = bcum_all[:, ks], btot_all[:, ks]
        q_e = (q * jnp.exp(bcum) * scale).astype(BF16)
        k_e = (k * jnp.exp(-bcum)).astype(BF16)
        k_d = (k * jnp.exp(btot - bcum)).astype(BF16)
        decay = jnp.exp(btot)
        vb = v.astype(BF16)
        s = lax.dot_general(q_e, k_e, nt, preferred_element_type=F32)
        s = jnp.where(causal, s, 0.0).astype(BF16)
        o_intra = jnp.dot(s, vb, preferred_element_type=F32)

        st = st_ref[i]
        outs = []
        for c in range(ts // chunk):
            lo, hi = c * chunk, (c + 1) * chunk
            outs.append(o_intra[lo:hi]
                        + lax.dot_general(q_e[lo:hi], st.astype(BF16), nt, preferred_element_type=F32))
            st = st * decay[lo:lo + 1] + lax.dot_general(vb[lo:hi], k_d[lo:hi], tn, preferred_element_type=F32)
        st_ref[i] = st
        o = jnp.concatenate(outs, axis=0)
        o = o * lax.rsqrt(jnp.mean(o * o, axis=-1, keepdims=True) + EPS) * g_ref[...]
        r = r_ref[0, :, vs]
        o_ref[0, :, vs] = (o * (r * jax.nn.sigmoid(r))).astype(o_ref.dtype)


def _gla(proj, a_low, wa, ba, g, dk, dv, ts=256, heads_per_step=4):
    b, s, _ = proj.shape
    hp = heads_per_step
    hg = GLA_HEADS // hp
    ts = min(ts, s)
    wk, wv = hp * dk, hp * dv
    kq, kv = (GLA_HEADS * dk) // wk, (2 * GLA_HEADS * dk) // wv
    return pl.pallas_call(
        functools.partial(_gla_kernel, chunk=GLA_CHUNK, scale=dk ** -0.5),
        grid=(b, hg, s // ts),
        in_specs=[pl.BlockSpec((1, ts, wk), lambda bi, hi, si: (bi, si, hi)),
                  pl.BlockSpec((1, ts, wk), lambda bi, hi, si: (bi, si, kq + hi)),
                  pl.BlockSpec((1, ts, wv), lambda bi, hi, si: (bi, si, kv + hi)),
                  pl.BlockSpec((1, ts, wv), lambda bi, hi, si: (bi, si, kv + hg + hi)),
                  pl.BlockSpec((1, ts, LANES), lambda bi, hi, si: (bi, si, 0)),
                  pl.BlockSpec((LANES, wk), lambda bi, hi, si: (0, hi)),
                  pl.BlockSpec((1, wk), lambda bi, hi, si: (0, hi)),
                  pl.BlockSpec((1, dv), lambda bi, hi, si: (0, 0))],
        out_specs=pl.BlockSpec((1, ts, wv), lambda bi, hi, si: (bi, si, hi)),
        out_shape=jax.ShapeDtypeStruct((b, s, GLA_HEADS * dv), BF16),
        scratch_shapes=[pltpu.VMEM((hp, dv, dk), F32)],
        compiler_params=_params(("parallel", "parallel", "arbitrary"), 40),
        name="gla",
    )(proj, proj, proj, proj, a_low, wa, ba, g)


def _conv_kernel(a_ref, b_ref, bga_ref, bgb_ref, w_ref, bdw_ref, lg_ref, lb_ref, o_ref, ubuf, sbuf, cbuf, *,
                 width, cw, rc):
    ts, dc = a_ref.shape[1], a_ref.shape[2]
    si = pl.program_id(1)

    @pl.when(si == 0)
    def _():
        ubuf[0:CONV_HALO, :] = jnp.zeros((CONV_HALO, dc), F32)

    @pl.when(si > 0)
    def _():
        ubuf[0:CONV_HALO, :] = ubuf[ts:ts + CONV_HALO, :]

    ubuf[CONV_HALO:CONV_HALO + ts, :] = (a_ref[0] + bga_ref[...]) * jax.nn.sigmoid(b_ref[0] + bgb_ref[...])

    span = ts + CONV_HALO - SUBLANES

    def col_body(cb, carry):
        cs = pl.ds(pl.multiple_of(cb * cw, cw), cw)
        for b in range(1, SUBLANES):
            sbuf[b, 0:span, :] = ubuf[b:b + span, cs]
        for rb in range(ts // rc):
            acc = jnp.broadcast_to(bdw_ref[:, cs], (rc, cw))
            for j in range(width):
                off = CONV_HALO - (width - 1) + j
                b = off % SUBLANES
                r0 = off - b + rb * rc
                src = ubuf[r0:r0 + rc, cs] if b == 0 else sbuf[b, r0:r0 + rc, :]
                acc = acc + w_ref[j:j + 1, cs] * src
            cbuf[rb * rc:(rb + 1) * rc, cs] = acc
        return carry

    lax.fori_loop(0, dc // cw, col_body, 0)
    c = cbuf[...]
    mu = jnp.mean(c, axis=-1, keepdims=True)
    cen = c - mu
    var = jnp.mean(cen * cen, axis=-1, keepdims=True)
    un = cen * lax.rsqrt(var + EPS) * lg_ref[...] + lb_ref[...]
    o_ref[0] = (un * jax.nn.sigmoid(un)).astype(o_ref.dtype)


def _conv_module(proj, col0, dc, b_glu, w_dw, b_dw, ln_g, ln_b, ts=256):
    b, s, _ = proj.shape
    ts = min(ts, s)
    width = w_dw.shape[0]
    cw = min(128, dc)
    assert width - 1 <= CONV_HALO <= ts and col0 % dc == 0
    cblk = col0 // dc
    row = lambda v: v.reshape(1, dc)
    vec = pl.BlockSpec((1, dc), lambda bi, si: (0, 0))
    return pl.pallas_call(
        functools.partial(_conv_kernel, width=width, cw=cw, rc=min(128, ts)),
        grid=(b, s // ts),
        in_specs=[pl.BlockSpec((1, ts, dc), lambda bi, si: (bi, si, cblk)),
                  pl.BlockSpec((1, ts, dc), lambda bi, si: (bi, si, cblk + 1)),
                  vec, vec,
                  pl.BlockSpec((width, dc), lambda bi, si: (0, 0)),
                  vec, vec, vec],
        out_specs=pl.BlockSpec((1, ts, dc), lambda bi, si: (bi, si, 0)),
        out_shape=jax.ShapeDtypeStruct((b, s, dc), BF16),
        scratch_shapes=[pltpu.VMEM((CONV_HALO + ts, dc), F32),
                        pltpu.VMEM((SUBLANES, CONV_HALO + ts, cw), F32),
                        pltpu.VMEM((ts, dc), F32)],
        compiler_params=_params(("parallel", "arbitrary"), 32),
        name="conv_module",
    )(proj, proj, row(b_glu[:dc]), row(b_glu[dc:]), w_dw, row(b_dw), row(ln_g), row(ln_b))


def _router_kernel(h_ref, g_ref, wr_ref, br_ref, n_ref, idx_ref, gate_ref, *, n_exp):
    h = h_ref[...]
    d2 = h.shape[1] // 2
    n = h * lax.rsqrt(jnp.mean(h * h, axis=-1, keepdims=True) + EPS) * g_ref[...]
    n_ref[...] = _pack_halves(n[:, :d2], n[:, d2:])
    logits = jnp.dot(n.astype(BF16), wr_ref[...], preferred_element_type=F32) + br_ref[...]
    lane = lax.broadcasted_iota(jnp.int32, logits.shape, 1)
    lane_f = lane.astype(F32)
    neg = jnp.float32(-jnp.inf)
    l = jnp.where(lane < n_exp, logits, neg)
    vals, idxs = [], []
    for _ in range(TOP_K):
        m = jnp.max(l, axis=-1, keepdims=True)
        i = jnp.min(jnp.where(l == m, lane_f, float(LANES)), axis=-1, keepdims=True)
        vals.append(m)
        idxs.append(i)
        l = jnp.where(lane_f == i, neg, l)
    es = [jnp.exp(v - vals[0]) for v in vals]
    tot = es[0]
    for e in es[1:]:
        tot = tot + e
    idx_out = jnp.zeros(logits.shape, F32)
    gate_out = jnp.zeros(logits.shape, F32)
    for kk in range(TOP_K):
        idx_out = jnp.where(lane == kk, idxs[kk], idx_out)
        gate_out = jnp.where(lane == kk, es[kk] / tot, gate_out)
    idx_ref[...] = idx_out.astype(jnp.int32)
    gate_ref[...] = gate_out


def _router(h, g, wr, br, n_exp, rows=256):
    t, d = h.shape
    rows = min(rows, t)
    blk = lambda w: pl.BlockSpec((rows, w), lambda i: (i, 0))
    return pl.pallas_call(
        functools.partial(_router_kernel, n_exp=n_exp),
        grid=(t // rows,),
        in_specs=[blk(d),
                  pl.BlockSpec((1, d), lambda i: (0, 0)),
                  pl.BlockSpec((d, LANES), lambda i: (0, 0)),
                  pl.BlockSpec((1, LANES), lambda i: (0, 0))],
        out_specs=[blk(d // 2), blk(LANES), blk(LANES)],
        out_shape=[jax.ShapeDtypeStruct((t, d // 2), U32),
                   jax.ShapeDtypeStruct((t, LANES), jnp.int32),
                   jax.ShapeDtypeStruct((t, LANES), F32)],
        compiler_params=_params(("parallel",), 32),
        name="ffn_norm_router",
    )(h, g.reshape(1, d), wr, br)


def _dispatch_kernel(zf_ref, zl_ref, nu_ref, slot_ref, n_ref, xs_hbm, zbuf, zsem, sem, *, rows):
    i = pl.program_id(0)
    n_halves = xs_hbm.shape[0] // MOE_HALF
    first_unused = nu_ref[0] * (MOE_ROWS // MOE_HALF)

    def zero_copy(row, sem_i):
        return pltpu.make_async_copy(zbuf, xs_hbm.at[pl.ds(pl.multiple_of(row, MOE_HALF), MOE_HALF)],
                                     zsem.at[sem_i])

    def for_each(z_ref, sem_i, act):
        for z in range(z_ref.shape[0]):
            @pl.when(z_ref[z] >= 0)
            def _():
                act(zero_copy(z_ref[z], sem_i))

    def for_unused(act):
        def body(hi, carry):
            act(zero_copy(hi * MOE_HALF, 1))
            return carry
        lax.fori_loop(first_unused, n_halves, body, 0)

    @pl.when(i == 0)
    def _():
        zbuf[...] = jnp.zeros_like(zbuf)
        for_each(zf_ref, 0, lambda c: c.start())
        for_each(zl_ref, 1, lambda c: c.start())
        for_unused(lambda c: c.start())
        for_each(zf_ref, 0, lambda c: c.wait())

    def row_copy(r, slot):
        return pltpu.make_async_copy(n_ref.at[pl.ds(r, 1)], xs_hbm.at[pl.ds(slot, 1)], sem)

    def start(r, carry):
        for kk in range(TOP_K):
            row_copy(r, slot_ref[0, 0, r * TOP_K + kk]).start(priority=kk % 2)
        return carry

    lax.fori_loop(0, rows, start, 0)
    for kk in range(TOP_K):
        pltpu.make_async_copy(n_ref, xs_hbm.at[pl.ds(0, rows)], sem).wait()

    @pl.when(i == pl.num_programs(0) - 1)
    def _():
        for_each(zl_ref, 1, lambda c: c.wait())
        for_unused(lambda c: c.wait())


def _dispatch(n_packed, slot, zero_first, zero_later, n_used, n_slots, rows=512):
    t, d2 = n_packed.shape
    rows = min(rows, t)
    grid_spec = pltpu.PrefetchScalarGridSpec(
        num_scalar_prefetch=3,
        grid=(t // rows,),
        in_specs=[pl.BlockSpec((1, 1, rows * TOP_K), lambda i, zf, zl, nu: (i, 0, 0), memory_space=pltpu.SMEM),
                  pl.BlockSpec((rows, d2), lambda i, zf, zl, nu: (i, 0))],
        out_specs=pl.BlockSpec(memory_space=pl.ANY),
        scratch_shapes=[pltpu.VMEM((MOE_HALF, d2), U32), pltpu.SemaphoreType.DMA((2,)),
                        pltpu.SemaphoreType.DMA(())],
    )
    return pl.pallas_call(
        functools.partial(_dispatch_kernel, rows=rows),
        grid_spec=grid_spec,
        out_shape=jax.ShapeDtypeStruct((n_slots, d2), U32),
        compiler_params=_params(("arbitrary",), 32),
        name="moe_dispatch",
    )(zero_first, zero_later, n_used, slot.reshape(t // rows, 1, rows * TOP_K), n_packed)


def _block_state(be_ref, nu_ref, bi):
    used = bi < nu_ref[0]
    fresh = jnp.logical_or(bi == 0, be_ref[bi] != be_ref[jnp.maximum(bi - 1, 0)])
    return used, jnp.logical_and(used, fresh)


def _run_weights(copies, first, fresh, cast, then):
    first_cond, first_args = first

    @pl.when(first_cond)
    def _():
        for c in copies(*first_args):
            c.start(priority=BULK_DMA_PRIORITY)

    @pl.when(fresh)
    def _():
        for c in copies(*first_args):
            c.wait()
        cast()
        for cond, args in then:
            @pl.when(cond)
            def _():
                for c in copies(*args):
                    c.start(priority=BULK_DMA_PRIORITY)


def _cast_rows(src, dst, rows):
    def body(i, carry):
        r = pl.ds(pl.multiple_of(i * rows, rows), rows)
        dst[r, :] = src[r, :].astype(BF16)
        return carry
    lax.fori_loop(0, src.shape[0] // rows, body, 0)


def _per_block_halves(halves, compute, o_ref):
    @pl.when(halves == 2)
    def _():
        compute(MOE_ROWS)

    @pl.when(halves == 1)
    def _():
        compute(MOE_HALF)
        o_ref[MOE_HALF:MOE_ROWS, :] = jnp.zeros((MOE_ROWS - MOE_HALF, o_ref.shape[1]), o_ref.dtype)

    @pl.when(halves == 0)
    def _():
        o_ref[...] = jnp.zeros_like(o_ref)


def _moe_up_kernel(be_ref, nu_ref, nx_ref, bh_ref, x_ref, bg_ref, bl_ref, w_hbm, o_ref, sg, sl, wgb, wlb, sem, *,
                   nf):
    fi, bi = pl.program_id(0), pl.program_id(1)
    _, fresh = _block_state(be_ref, nu_ref, bi)
    tf = sg.shape[1]
    f_all = w_hbm.shape[2] // 2

    def copies(e, f):
        c0 = pl.multiple_of(f * tf, tf)
        return (pltpu.make_async_copy(w_hbm.at[e, :, pl.ds(c0, tf)], sg, sem.at[0]),
                pltpu.make_async_copy(w_hbm.at[e, :, pl.ds(f_all + c0, tf)], sl, sem.at[1]))

    def cast():
        _cast_rows(sg, wgb, CAST_ROWS)
        _cast_rows(sl, wlb, CAST_ROWS)

    nx = nx_ref[bi]
    _run_weights(copies, (jnp.logical_and(fi == 0, bi == 0), (be_ref[0], 0)), fresh, cast,
                 [(nx >= 0, (nx, fi)),
                  (jnp.logical_and(nx < 0, fi + 1 < nf), (be_ref[0], fi + 1))])

    def compute(rows):
        d2 = x_ref.shape[1]
        lo, hi = _unpack_halves(x_ref[0:rows, :])
        lo, hi = lo.astype(BF16), hi.astype(BF16)

        def proj(wb, b_ref):
            return (jnp.dot(lo, wb[0:d2, :], preferred_element_type=F32)
                    + jnp.dot(hi, wb[d2:2 * d2, :], preferred_element_type=F32) + b_ref[0])

        xg = jnp.minimum(proj(wgb, bg_ref), SWIGLU_LIMIT)
        xl = jnp.clip(proj(wlb, bl_ref), -SWIGLU_LIMIT, SWIGLU_LIMIT)
        o_ref[0:rows, :] = (xg * jax.nn.sigmoid(SWIGLU_ALPHA * xg) * (xl + 1.0)).astype(o_ref.dtype)

    _per_block_halves(bh_ref[bi], compute, o_ref)


def _moe_up(xs, block_expert, n_used, next_expert, block_halves, w1, b1, tf=512):
    n_slots, d2 = xs.shape
    e, d, f2 = w1.shape
    f = f2 // 2
    tf = min(tf, f)
    nf = f // tf
    nb = n_slots // MOE_ROWS
    blk = lambda bi, nu: jnp.minimum(bi, nu[0] - 1)
    grid_spec = pltpu.PrefetchScalarGridSpec(
        num_scalar_prefetch=4,
        grid=(nf, nb),
        in_specs=[pl.BlockSpec((MOE_ROWS, d2), lambda fi, bi, be, nu, nx, bh: (blk(bi, nu), 0)),
                  pl.BlockSpec((1, 1, tf), lambda fi, bi, be, nu, nx, bh: (be[blk(bi, nu)], 0, fi)),
                  pl.BlockSpec((1, 1, tf), lambda fi, bi, be, nu, nx, bh: (be[blk(bi, nu)], 0, nf + fi)),
                  pl.BlockSpec(memory_space=pl.ANY)],
        out_specs=pl.BlockSpec((MOE_ROWS, tf), lambda fi, bi, be, nu, nx, bh: (bi, fi)),
        scratch_shapes=[pltpu.VMEM((d, tf), F32), pltpu.VMEM((d, tf), F32),
                        pltpu.VMEM((d, tf), BF16), pltpu.VMEM((d, tf), BF16),
                        pltpu.SemaphoreType.DMA((2,))],
    )
    return pl.pallas_call(
        functools.partial(_moe_up_kernel, nf=nf),
        grid_spec=grid_spec,
        out_shape=jax.ShapeDtypeStruct((n_slots, f), BF16),
        compiler_params=_params(("arbitrary", "arbitrary"), 48),
        name="moe_up",
    )(block_expert, n_used, next_expert, block_halves, xs, b1.reshape(e, 1, f2), b1.reshape(e, 1, f2), w1)


def _moe_down_kernel(be_ref, nu_ref, nx_ref, bh_ref, h_ref, b_ref, w_hbm, o_ref, sw, wb, sem):
    bi = pl.program_id(0)
    _, fresh = _block_state(be_ref, nu_ref, bi)

    def copies(e):
        return (pltpu.make_async_copy(w_hbm.at[e], sw, sem.at[0]),)

    def cast():
        wb[...] = sw[...].astype(BF16)

    nx = nx_ref[bi]
    _run_weights(copies, (bi == 0, (be_ref[0],)), fresh, cast, [(nx >= 0, (nx,))])

    def compute(rows):
        d2 = o_ref.shape[1]
        h = h_ref[0:rows, :]
        y_lo = jnp.dot(h, wb[:, 0:d2], preferred_element_type=F32) + b_ref[0, :, 0:d2]
        y_hi = jnp.dot(h, wb[:, d2:2 * d2], preferred_element_type=F32) + b_ref[0, :, d2:2 * d2]
        o_ref[0:rows, :] = _pack_halves(y_lo, y_hi)

    _per_block_halves(bh_ref[bi], compute, o_ref)


def _moe_down(hid, block_expert, n_used, next_expert, block_halves, w2, b2):
    n_slots, f = hid.shape
    e, _, d = w2.shape
    nb = n_slots // MOE_ROWS
    blk = lambda bi, nu: jnp.minimum(bi, nu[0] - 1)
    grid_spec = pltpu.PrefetchScalarGridSpec(
        num_scalar_prefetch=4,
        grid=(nb,),
        in_specs=[pl.BlockSpec((MOE_ROWS, f), lambda bi, be, nu, nx, bh: (blk(bi, nu), 0)),
                  pl.BlockSpec((1, 1, d), lambda bi, be, nu, nx, bh: (be[blk(bi, nu)], 0, 0)),
                  pl.BlockSpec(memory_space=pl.ANY)],
        out_specs=pl.BlockSpec((MOE_ROWS, d // 2), lambda bi, be, nu, nx, bh: (bi, 0)),
        scratch_shapes=[pltpu.VMEM((f, d), F32), pltpu.VMEM((f, d), BF16), pltpu.SemaphoreType.DMA((1,))],
    )
    return pl.pallas_call(
        _moe_down_kernel,
        grid_spec=grid_spec,
        out_shape=jax.ShapeDtypeStruct((n_slots, d // 2), U32),
        compiler_params=_params(("arbitrary",), 48),
        name="moe_down",
    )(block_expert, n_used, next_expert, block_halves, hid, b2.reshape(e, 1, d), w2)


def _combine_kernel(slot_ref, nslot_ref, gate_ref, h_ref, g_ref, y_hbm, o_ref, *scratch, rows):
    bufs, sem = scratch[:COMBINE_PHASES], scratch[COMBINE_PHASES]
    i = pl.program_id(0)
    d2 = bufs[0].shape[2]
    group = SUBLANES

    def start_rows(s_ref, base, buf, sem_i, r0):
        for r in range(r0, r0 + group):
            for kk in range(TOP_K):
                pltpu.make_async_copy(y_hbm.at[pl.ds(s_ref[0, 0, (base + r) * TOP_K + kk], 1)],
                                      buf.at[kk, pl.ds(r, 1)], sem.at[sem_i]).start(priority=kk % 2)

    def wait_all(buf, sem_i):
        for kk in range(TOP_K):
            pltpu.make_async_copy(y_hbm.at[pl.ds(0, rows)], buf.at[kk], sem.at[sem_i]).wait()

    def compute_rows(buf, base, r0):
        tok = slice(base + r0, base + r0 + group)
        gates = gate_ref[tok, :]
        acc_lo, acc_hi = h_ref[tok, 0:d2], h_ref[tok, d2:2 * d2]
        for kk in range(TOP_K):
            lo, hi = _unpack_halves(buf[kk, r0:r0 + group, :])
            gk = gates[:, kk:kk + 1]
            acc_lo = acc_lo + gk * lo
            acc_hi = acc_hi + gk * hi
        ms = (jnp.sum(acc_lo * acc_lo, axis=-1, keepdims=True)
              + jnp.sum(acc_hi * acc_hi, axis=-1, keepdims=True)) * (1.0 / (2 * d2))
        inv = lax.rsqrt(ms + EPS)
        o_ref[tok, 0:d2] = acc_lo * inv * g_ref[:, 0:d2]
        o_ref[tok, d2:2 * d2] = acc_hi * inv * g_ref[:, d2:2 * d2]

    @pl.when(i == 0)
    def _():
        for p in range(COMBINE_AHEAD):
            def body(r, carry, p=p):
                for kk in range(TOP_K):
                    pltpu.make_async_copy(y_hbm.at[pl.ds(slot_ref[0, 0, (p * rows + r) * TOP_K + kk], 1)],
                                          bufs[p].at[kk, pl.ds(r, 1)], sem.at[p]).start(priority=kk % 2)
                return carry
            lax.fori_loop(0, rows, body, 0)

    for p in range(COMBINE_PHASES):
        wait_all(bufs[p], p)
        q = p + COMBINE_AHEAD
        s_ref, qq = (slot_ref, q) if q < COMBINE_PHASES else (nslot_ref, q - COMBINE_PHASES)
        for r0 in range(0, rows, group):
            start_rows(s_ref, qq * rows, bufs[qq], qq, r0)
            compute_rows(bufs[p], p * rows, r0)

    @pl.when(i == pl.num_programs(0) - 1)
    def _():
        for p in range(COMBINE_AHEAD):
            wait_all(bufs[p], p)


def _combine(y_packed, slot, gates, h, g, rows=64):
    t, d = h.shape
    rows = min(rows, t // COMBINE_PHASES)
    block = COMBINE_PHASES * rows
    nsteps = t // block
    slot3 = slot.reshape(nsteps, 1, block * TOP_K)
    return pl.pallas_call(
        functools.partial(_combine_kernel, rows=rows),
        grid=(nsteps,),
        in_specs=[pl.BlockSpec((1, 1, block * TOP_K), lambda i: (i, 0, 0), memory_space=pltpu.SMEM),
                  pl.BlockSpec((1, 1, block * TOP_K), lambda i: (jnp.minimum(i + 1, nsteps - 1), 0, 0),
                               memory_space=pltpu.SMEM),
                  pl.BlockSpec((block, LANES), lambda i: (i, 0)),
                  pl.BlockSpec((block, d), lambda i: (i, 0)),
                  pl.BlockSpec((1, d), lambda i: (0, 0)),
                  pl.BlockSpec(memory_space=pl.ANY)],
        out_specs=pl.BlockSpec((block, d), lambda i: (i, 0)),
        out_shape=jax.ShapeDtypeStruct((t, d), F32),
        scratch_shapes=([pltpu.VMEM((TOP_K, rows, d // 2), U32) for _ in range(COMBINE_PHASES)]
                        + [pltpu.SemaphoreType.DMA((COMBINE_PHASES,))]),
        compiler_params=_params(("arbitrary",), 40),
        name="moe_combine",
    )(slot3, slot3, gates, h, g.reshape(1, d), y_packed)


def _routing(top_idx, n_exp):
    t = top_idx.shape[0]
    experts = jnp.arange(n_exp, dtype=jnp.int32)
    onehot = (top_idx[:, :, None] == experts[None, None, :]).any(axis=1).astype(jnp.int32)
    incl = jnp.cumsum(onehot, axis=0)
    counts = incl[-1]
    padded = (counts + MOE_ROWS - 1) // MOE_ROWS * MOE_ROWS
    pad_ends = jnp.cumsum(padded)
    slot = jnp.take_along_axis(incl - onehot + (pad_ends - padded)[None, :], top_idx, axis=1)
    n_blocks = -(-(t * TOP_K) // MOE_ROWS) + n_exp
    block_rows = jnp.arange(n_blocks, dtype=jnp.int32) * MOE_ROWS
    block_expert = jnp.minimum(jnp.sum(pad_ends[None, :] <= block_rows[:, None], axis=1), n_exp - 1)
    n_used = (pad_ends[-1] // MOE_ROWS).reshape(1)
    rows_in_block = jnp.clip((counts - padded + pad_ends)[block_expert] - block_rows, 0, MOE_ROWS)
    block_halves = jnp.where(block_rows < pad_ends[-1], jnp.where(rows_in_block <= MOE_HALF, 1, 2), 0)
    last_rows = counts - padded + MOE_ROWS
    live = counts > 0
    zero_first = jnp.concatenate([
        jnp.where(live & (last_rows > MOE_HALF) & (last_rows < MOE_ROWS), pad_ends - MOE_HALF, -1),
        jnp.where(live & (last_rows < MOE_HALF), pad_ends - MOE_ROWS, -1)])
    zero_later = jnp.where(live & (last_rows <= MOE_HALF), pad_ends - MOE_HALF, -1)
    later = jnp.logical_and(experts[None, :] > experts[:, None], padded[None, :] > 0)
    next_of = jnp.min(jnp.where(later, experts[None, :], n_exp), axis=1)
    next_expert = jnp.where(next_of < n_exp, next_of, -1)[block_expert]
    i32 = lambda a: a.astype(jnp.int32)
    return (i32(slot), i32(block_expert), i32(n_used), i32(next_expert), i32(block_halves), i32(zero_first),
            i32(zero_later), n_blocks * MOE_ROWS)


def kernel(x, attn_norm_g, w_in, w_a2, b_a, gla_norm_g, b_glu, w_dw, b_dw, conv_ln_g, conv_ln_b, w_out,
           ffn_norm_g, w_router, b_router, w1, b1, w2, b2, final_norm_g):
    b, s, d = x.shape
    assert w_in.shape[0] == 1, "single-layer stack"
    t = b * s
    dv = gla_norm_g.shape[1]
    dk = w_a2.shape[2] // GLA_HEADS
    lowrank = w_a2.shape[1]
    dc = w_dw.shape[2]
    n_exp = w_router.shape[2]
    c_qkvr = 2 * GLA_HEADS * (dk + dv)
    assert lowrank <= LANES and n_exp <= LANES
    x2 = x.reshape(t, d)

    n1 = _rmsnorm(x2, attn_norm_g[0], BF16)
    w_main, w_low = _wprep(jnp.swapaxes(w_in[0], 0, 1), c_qkvr, lowrank)
    proj = _matmul_nt(n1, w_main, "inproj").reshape(b, s, -1)
    a_low = _matmul_nt(n1, w_low, "inproj_lowrank").reshape(b, s, LANES)

    wa = jnp.pad(w_a2[0], ((0, LANES - lowrank), (0, 0))).astype(BF16)
    gla_out = _gla(proj, a_low, wa, b_a[0].reshape(1, -1), gla_norm_g[0].reshape(1, dv), dk, dv)
    conv_out = _conv_module(proj, c_qkvr, dc, b_glu[0], w_dw[0], b_dw[0], conv_ln_g[0], conv_ln_b[0])
    h = _matmul([gla_out.reshape(t, -1), conv_out.reshape(t, dc)], w_out[0], d, "outproj", res=x2, tm=512, tn=1024)

    wr = jnp.pad(w_router[0], ((0, 0), (0, LANES - n_exp))).astype(BF16)
    br = jnp.pad(b_router[0], (0, LANES - n_exp)).reshape(1, LANES)
    n2, idx_pad, gate_pad = _router(h, ffn_norm_g[0], wr, br, n_exp)
    slot, block_expert, n_used, next_expert, block_halves, zero_first, zero_later, n_slots = _routing(
        idx_pad[:, :TOP_K], n_exp)
    xs = _dispatch(n2, slot, zero_first, zero_later, n_used, n_slots)
    hid = _moe_up(xs, block_expert, n_used, next_expert, block_halves, w1[0], b1[0])
    y = _moe_down(hid, block_expert, n_used, next_expert, block_halves, w2[0], b2[0])
    out = _combine(y, slot, gate_pad, h, final_norm_g)
    return out.reshape(b, s, d)
```

```python
import functools

import jax
import jax.numpy as jnp
from jax import lax
from jax.experimental import pallas as pl
from jax.experimental.pallas import tpu as pltpu

GLA_HEADS = 4
GLA_TAU = 16.0
GLA_CHUNK = 64
TOP_K = 4
SWIGLU_ALPHA = 1.702
SWIGLU_LIMIT = 7.0
EPS = 1e-5

LANES = 128
SUBLANES = 8
MOE_ROWS = 512
MOE_HALF = MOE_ROWS // 2
CONV_HALO = 32
COMBINE_PHASES = 4
COMBINE_AHEAD = 2
CAST_ROWS = 128
BULK_DMA_PRIORITY = 1
MIB = 1024 * 1024

F32 = jnp.float32
BF16 = jnp.bfloat16
U32 = jnp.uint32


def _params(semantics, vmem_mib):
    return pltpu.CompilerParams(dimension_semantics=semantics, vmem_limit_bytes=vmem_mib * MIB)


def _pack_halves(lo, hi):
    lo_u = lax.bitcast_convert_type(lo.astype(BF16).astype(F32), U32)
    hi_u = lax.bitcast_convert_type(hi.astype(BF16).astype(F32), U32)
    return (lo_u >> 16) | (hi_u & jnp.uint32(0xFFFF0000))


def _unpack_halves(p):
    lo = lax.bitcast_convert_type(p << 16, F32)
    hi = lax.bitcast_convert_type(p & jnp.uint32(0xFFFF0000), F32)
    return lo, hi


def _rmsnorm_kernel(x_ref, g_ref, o_ref):
    x = x_ref[...]
    y = x * lax.rsqrt(jnp.mean(x * x, axis=-1, keepdims=True) + EPS)
    o_ref[...] = (y * g_ref[...]).astype(o_ref.dtype)


def _rmsnorm_lowrank_kernel(x_ref, g_ref, wl_ref, o_ref, al_ref):
    x = x_ref[...]
    y = x * lax.rsqrt(jnp.mean(x * x, axis=-1, keepdims=True) + EPS)
    n = (y * g_ref[...]).astype(o_ref.dtype)
    o_ref[...] = n
    al_ref[...] = lax.dot_general(n, wl_ref[...], (((1,), (1,)), ((), ())), preferred_element_type=F32)


def _rmsnorm_lowrank(x, g, w_low, rows=256):
    t, d = x.shape
    rows = min(rows, t)
    return pl.pallas_call(
        _rmsnorm_lowrank_kernel,
        grid=(t // rows,),
        in_specs=[pl.BlockSpec((rows, d), lambda i: (i, 0)),
                  pl.BlockSpec((1, d), lambda i: (0, 0)),
                  pl.BlockSpec((LANES, d), lambda i: (0, 0))],
        out_specs=[pl.BlockSpec((rows, d), lambda i: (i, 0)),
                   pl.BlockSpec((rows, LANES), lambda i: (i, 0))],
        out_shape=[jax.ShapeDtypeStruct((t, d), BF16), jax.ShapeDtypeStruct((t, LANES), F32)],
        compiler_params=_params(("parallel",), 32),
        name="rmsnorm_lowrank",
    )(x, g.reshape(1, d), w_low)


def _rmsnorm(x, g, out_dtype, rows=256):
    t, d = x.shape
    rows = min(rows, t)
    return pl.pallas_call(
        _rmsnorm_kernel,
        grid=(t // rows,),
        in_specs=[pl.BlockSpec((rows, d), lambda i: (i, 0)),
                  pl.BlockSpec((1, d), lambda i: (0, 0))],
        out_specs=pl.BlockSpec((rows, d), lambda i: (i, 0)),
        out_shape=jax.ShapeDtypeStruct((t, d), out_dtype),
        compiler_params=_params(("parallel",), 32),
        name="rmsnorm",
    )(x, g.reshape(1, d))


def _wprep_kernel(a_ref, b_ref, o_ref, low_ref, prev, *, nb_main, lowrank):
    i = pl.program_id(0)
    rows = o_ref.shape[0]

    def straddle(first_ref):
        o_ref[0:rows - lowrank, :] = first_ref[lowrank:rows, :].astype(BF16)
        o_ref[rows - lowrank:rows, :] = b_ref[0:lowrank, :].astype(BF16)
        prev[...] = b_ref[...]

    @pl.when(i < nb_main)
    def _():
        o_ref[...] = a_ref[...].astype(BF16)

    @pl.when(i == nb_main)
    def _():
        low_ref[...] = jnp.zeros_like(low_ref)
        low_ref[0:lowrank, :] = a_ref[0:lowrank, :].astype(BF16)
        straddle(a_ref)

    @pl.when(i > nb_main)
    def _():
        straddle(prev)


def _wprep(wt, c_main, lowrank):
    n_in, k = wt.shape
    rows = 2 * LANES
    n_out = n_in - lowrank
    assert c_main % rows == 0 and n_out % rows == 0 and lowrank % 16 == 0
    last = pl.cdiv(n_in, rows) - 1
    nb_main = c_main // rows
    return pl.pallas_call(
        functools.partial(_wprep_kernel, nb_main=nb_main, lowrank=lowrank),
        grid=(n_out // rows,),
        in_specs=[pl.BlockSpec((rows, k), lambda i: (jnp.minimum(i, nb_main), 0)),
                  pl.BlockSpec((rows, k), lambda i: (jnp.minimum(jnp.maximum(i, nb_main) + 1, last), 0))],
        out_specs=[pl.BlockSpec((rows, k), lambda i: (i, 0)),
                   pl.BlockSpec((LANES, k), lambda i: (0, 0))],
        out_shape=[jax.ShapeDtypeStruct((n_out, k), BF16), jax.ShapeDtypeStruct((LANES, k), BF16)],
        scratch_shapes=[pltpu.VMEM((rows, k), F32)],
        compiler_params=_params(("arbitrary",), 32),
        name="inproj_weight_prep",
    )(wt, wt)


def _mm_nt_kernel(a_ref, wt_ref, o_ref):
    o_ref[...] = lax.dot_general(a_ref[...], wt_ref[...], (((1,), (1,)), ((), ())), preferred_element_type=F32)


def _matmul_nt(a, wt, name, tm=1024, tn=1024):
    m, k = a.shape
    n = wt.shape[0]
    tm, tn = min(tm, m), min(tn, n)
    return pl.pallas_call(
        _mm_nt_kernel,
        grid=(m // tm, n // tn),
        in_specs=[pl.BlockSpec((tm, k), lambda i, j: (i, 0)),
                  pl.BlockSpec((tn, k), lambda i, j: (j, 0))],
        out_specs=pl.BlockSpec((tm, tn), lambda i, j: (i, j)),
        out_shape=jax.ShapeDtypeStruct((m, n), F32),
        compiler_params=_params(("parallel", "arbitrary"), 48),
        name=name,
    )(a, wt)


def _mm_kernel(*refs, n_a, has_res):
    a_refs, w_ref = refs[:n_a], refs[n_a]
    res_ref = refs[n_a + 1] if has_res else None
    o_ref, wb = refs[-2], refs[-1]

    @pl.when(pl.program_id(1) == 0)
    def _():
        _cast_rows(w_ref, wb, CAST_ROWS)

    acc, k0 = None, 0
    for a_ref in a_refs:
        kk = a_ref.shape[1]
        part = jnp.dot(a_ref[...], wb[k0:k0 + kk, :], preferred_element_type=F32)
        acc = part if acc is None else acc + part
        k0 += kk
    if has_res:
        acc = acc + res_ref[...]
    o_ref[...] = acc.astype(o_ref.dtype)


def _matmul(a_list, w, n_out, name, res=None, tm=1024, tn=512):
    m = a_list[0].shape[0]
    k = w.shape[0]
    assert sum(a.shape[1] for a in a_list) == k
    tm, tn = min(tm, m), min(tn, n_out)
    in_specs = [pl.BlockSpec((tm, a.shape[1]), lambda j, i: (i, 0)) for a in a_list]
    in_specs.append(pl.BlockSpec((k, tn), lambda j, i: (0, j)))
    args = list(a_list) + [w]
    if res is not None:
        in_specs.append(pl.BlockSpec((tm, tn), lambda j, i: (i, j)))
        args.append(res)
    return pl.pallas_call(
        functools.partial(_mm_kernel, n_a=len(a_list), has_res=res is not None),
        grid=(n_out // tn, m // tm),
        in_specs=in_specs,
        out_specs=pl.BlockSpec((tm, tn), lambda j, i: (i, j)),
        out_shape=jax.ShapeDtypeStruct((m, n_out), F32),
        scratch_shapes=[pltpu.VMEM((k, tn), BF16)],
        compiler_params=_params(("parallel", "arbitrary"), 58),
        name=name,
    )(*args)


def _gla_kernel(q_ref, k_ref, v_ref, r_ref, al_ref, wa_ref, ba_ref, g_ref, o_ref, st_ref, *, chunk, scale):
    @pl.when(pl.program_id(2) == 0)
    def _():
        st_ref[...] = jnp.zeros_like(st_ref)

    ts = q_ref.shape[1]
    heads, dv, dk = st_ref.shape
    rows = lax.broadcasted_iota(jnp.int32, (ts, ts), 0)
    cols = lax.broadcasted_iota(jnp.int32, (ts, ts), 1)
    assert chunk & (chunk - 1) == 0
    same_chunk = (rows & -chunk) == (cols & -chunk)
    causal = jnp.logical_and(same_chunk, rows >= cols)
    tril = jnp.where(causal, 1.0, 0.0).astype(BF16)
    ones = jnp.where(same_chunk, 1.0, 0.0).astype(BF16)
    nt = (((1,), (1,)), ((), ()))
    tn = (((0,), (0,)), ((), ()))

    z = jnp.dot(al_ref[0].astype(BF16), wa_ref[...], preferred_element_type=F32) + ba_ref[...]
    la = (jnp.minimum(z, 0.0) - jnp.log1p(jnp.exp(-jnp.abs(z)))) * (1.0 / GLA_TAU)
    la_hi = la.astype(BF16)
    la_lo = (la - la_hi.astype(F32)).astype(BF16)
    bcum_all = jnp.dot(tril, la_hi, preferred_element_type=F32) + jnp.dot(tril, la_lo, preferred_element_type=F32)
    btot_all = jnp.dot(ones, la_hi, preferred_element_type=F32) + jnp.dot(ones, la_lo, preferred_element_type=F32)
    for i in range(heads):
        ks, vs = slice(i * dk, (i + 1) * dk), slice(i * dv, (i + 1) * dv)
        q, k, v = q_ref[0, :, ks], k_ref[0, :, ks], v_ref[0, :, vs]
        bcum, btot = bcum_all[:, ks], btot_all[:, ks]
        q_e = (q * jnp.exp(bcum) * scale).astype(BF16)
        k_e = (k * jnp.exp(-bcum)).astype(BF16)
        k_d = (k * jnp.exp(btot - bcum)).astype(BF16)
        decay = jnp.exp(btot)
        vb = v.astype(BF16)
        s = lax.dot_general(q_e, k_e, nt, preferred_element_type=F32)
        s = jnp.where(causal, s, 0.0).astype(BF16)
        o_intra = jnp.dot(s, vb, preferred_element_type=F32)

        st = st_ref[i]
        outs = []
        for c in range(ts // chunk):
            lo, hi = c * chunk, (c + 1) * chunk
            outs.append(o_intra[lo:hi]
                        + lax.dot_general(q_e[lo:hi], st.astype(BF16), nt, preferred_element_type=F32))
            st = st * decay[lo:lo + 1] + lax.dot_general(vb[lo:hi], k_d[lo:hi], tn, preferred_element_type=F32)
        st_ref[i] = st
        o = jnp.concatenate(outs, axis=0)
        o = o * lax.rsqrt(jnp.mean(o * o, axis=-1, keepdims=True) + EPS) * g_ref[...]
        r = r_ref[0, :, vs]
        o_ref[0, :, vs] = (o * (r * jax.nn.sigmoid(r))).astype(o_ref.dtype)


def _gla(proj, a_low, wa, ba, g, dk, dv, ts=256, heads_per_step=4):
    b, s, _ = proj.shape
    hp = heads_per_step
    hg = GLA_HEADS // hp
    ts = min(ts, s)
    wk, wv = hp * dk, hp * dv
    kq, kv = (GLA_HEADS * dk) // wk, (2 * GLA_HEADS * dk) // wv
    return pl.pallas_call(
        functools.partial(_gla_kernel, chunk=GLA_CHUNK, scale=dk ** -0.5),
        grid=(b, hg, s // ts),
        in_specs=[pl.BlockSpec((1, ts, wk), lambda bi, hi, si: (bi, si, hi)),
                  pl.BlockSpec((1, ts, wk), lambda bi, hi, si: (bi, si, kq + hi)),
                  pl.BlockSpec((1, ts, wv), lambda bi, hi, si: (bi, si, kv + hi)),
                  pl.BlockSpec((1, ts, wv), lambda bi, hi, si: (bi, si, kv + hg + hi)),
                  pl.BlockSpec((1, ts, LANES), lambda bi, hi, si: (bi, si, 0)),
                  pl.BlockSpec((LANES, wk), lambda bi, hi, si: (0, hi)),
                  pl.BlockSpec((1, wk), lambda bi, hi, si: (0, hi)),
                  pl.BlockSpec((1, dv), lambda bi, hi, si: (0, 0))],
        out_specs=pl.BlockSpec((1, ts, wv), lambda bi, hi, si: (bi, si, hi)),
        out_shape=jax.ShapeDtypeStruct((b, s, GLA_HEADS * dv), BF16),
        scratch_shapes=[pltpu.VMEM((hp, dv, dk), F32)],
        compiler_params=_params(("parallel", "parallel", "arbitrary"), 40),
        name="gla",
    )(proj, proj, proj, proj, a_low, wa, ba, g)


def _conv_kernel(a_ref, b_ref, bga_ref, bgb_ref, w_ref, bdw_ref, lg_ref, lb_ref, o_ref, ubuf, sbuf, cbuf, *,
                 width, cw, rc):
    ts, dc = a_ref.shape[1], a_ref.shape[2]
    si = pl.program_id(1)

    @pl.when(si == 0)
    def _():
        ubuf[0:CONV_HALO, :] = jnp.zeros((CONV_HALO, dc), F32)

    @pl.when(si > 0)
    def _():
        ubuf[0:CONV_HALO, :] = ubuf[ts:ts + CONV_HALO, :]

    ubuf[CONV_HALO:CONV_HALO + ts, :] = (a_ref[0] + bga_ref[...]) * jax.nn.sigmoid(b_ref[0] + bgb_ref[...])

    span = ts + CONV_HALO - SUBLANES

    def col_body(cb, carry):
        cs = pl.ds(pl.multiple_of(cb * cw, cw), cw)
        for b in range(1, SUBLANES):
            sbuf[b, 0:span, :] = ubuf[b:b + span, cs]
        for rb in range(ts // rc):
            acc = jnp.broadcast_to(bdw_ref[:, cs], (rc, cw))
            for j in range(width):
                off = CONV_HALO - (width - 1) + j
                b = off % SUBLANES
                r0 = off - b + rb * rc
                src = ubuf[r0:r0 + rc, cs] if b == 0 else sbuf[b, r0:r0 + rc, :]
                acc = acc + w_ref[j:j + 1, cs] * src
            cbuf[rb * rc:(rb + 1) * rc, cs] = acc
        return carry

    lax.fori_loop(0, dc // cw, col_body, 0)
    c = cbuf[...]
    mu = jnp.mean(c, axis=-1, keepdims=True)
    cen = c - mu
    var = jnp.mean(cen * cen, axis=-1, keepdims=True)
    un = cen * lax.rsqrt(var + EPS) * lg_ref[...] + lb_ref[...]
    o_ref[0] = (un * jax.nn.sigmoid(un)).astype(o_ref.dtype)


def _conv_module(proj, col0, dc, b_glu, w_dw, b_dw, ln_g, ln_b, ts=256):
    b, s, _ = proj.shape
    ts = min(ts, s)
    width = w_dw.shape[0]
    cw = min(128, dc)
    assert width - 1 <= CONV_HALO <= ts and col0 % dc == 0
    cblk = col0 // dc
    row = lambda v: v.reshape(1, dc)
    vec = pl.BlockSpec((1, dc), lambda bi, si: (0, 0))
    return pl.pallas_call(
        functools.partial(_conv_kernel, width=width, cw=cw, rc=min(128, ts)),
        grid=(b, s // ts),
        in_specs=[pl.BlockSpec((1, ts, dc), lambda bi, si: (bi, si, cblk)),
                  pl.BlockSpec((1, ts, dc), lambda bi, si: (bi, si, cblk + 1)),
                  vec, vec,
                  pl.BlockSpec((width, dc), lambda bi, si: (0, 0)),
                  vec, vec, vec],
        out_specs=pl.BlockSpec((1, ts, dc), lambda bi, si: (bi, si, 0)),
        out_shape=jax.ShapeDtypeStruct((b, s, dc), BF16),
        scratch_shapes=[pltpu.VMEM((CONV_HALO + ts, dc), F32),
                        pltpu.VMEM((SUBLANES, CONV_HALO + ts, cw), F32),
                        pltpu.VMEM((ts, dc), F32)],
        compiler_params=_params(("parallel", "arbitrary"), 32),
        name="conv_module",
    )(proj, proj, row(b_glu[:dc]), row(b_glu[dc:]), w_dw, row(b_dw), row(ln_g), row(ln_b))


def _router_kernel(h_ref, g_ref, wr_ref, br_ref, n_ref, idx_ref, gate_ref, *, n_exp):
    h = h_ref[...]
    d2 = h.shape[1] // 2
    n = h * lax.rsqrt(jnp.mean(h * h, axis=-1, keepdims=True) + EPS) * g_ref[...]
    n_ref[...] = _pack_halves(n[:, :d2], n[:, d2:])
    logits = jnp.dot(n.astype(BF16), wr_ref[...], preferred_element_type=F32) + br_ref[...]
    lane = lax.broadcasted_iota(jnp.int32, logits.shape, 1)
    lane_f = lane.astype(F32)
    neg = jnp.float32(-jnp.inf)
    l = jnp.where(lane < n_exp, logits, neg)
    vals, idxs = [], []
    for _ in range(TOP_K):
        m = jnp.max(l, axis=-1, keepdims=True)
        i = jnp.min(jnp.where(l == m, lane_f, float(LANES)), axis=-1, keepdims=True)
        vals.append(m)
        idxs.append(i)
        l = jnp.where(lane_f == i, neg, l)
    es = [jnp.exp(v - vals[0]) for v in vals]
    tot = es[0]
    for e in es[1:]:
        tot = tot + e
    idx_out = jnp.zeros(logits.shape, F32)
    gate_out = jnp.zeros(logits.shape, F32)
    for kk in range(TOP_K):
        idx_out = jnp.where(lane == kk, idxs[kk], idx_out)
        gate_out = jnp.where(lane == kk, es[kk] / tot, gate_out)
    idx_ref[...] = idx_out.astype(jnp.int32)
    gate_ref[...] = gate_out


def _router(h, g, wr, br, n_exp, rows=256):
    t, d = h.shape
    rows = min(rows, t)
    blk = lambda w: pl.BlockSpec((rows, w), lambda i: (i, 0))
    return pl.pallas_call(
        functools.partial(_router_kernel, n_exp=n_exp),
        grid=(t // rows,),
        in_specs=[blk(d),
                  pl.BlockSpec((1, d), lambda i: (0, 0)),
                  pl.BlockSpec((d, LANES), lambda i: (0, 0)),
                  pl.BlockSpec((1, LANES), lambda i: (0, 0))],
        out_specs=[blk(d // 2), blk(LANES), blk(LANES)],
        out_shape=[jax.ShapeDtypeStruct((t, d // 2), U32),
                   jax.ShapeDtypeStruct((t, LANES), jnp.int32),
                   jax.ShapeDtypeStruct((t, LANES), F32)],
        compiler_params=_params(("parallel",), 32),
        name="ffn_norm_router",
    )(h, g.reshape(1, d), wr, br)


def _dispatch_kernel(zf_ref, zl_ref, nu_ref, slot_ref, n_ref, xs_hbm, zbuf, zsem, sem, *, rows):
    i = pl.program_id(0)
    n_halves = xs_hbm.shape[0] // MOE_HALF
    first_unused = nu_ref[0] * (MOE_ROWS // MOE_HALF)

    def zero_copy(row, sem_i):
        return pltpu.make_async_copy(zbuf, xs_hbm.at[pl.ds(pl.multiple_of(row, MOE_HALF), MOE_HALF)],
                                     zsem.at[sem_i])

    def for_each(z_ref, sem_i, act):
        for z in range(z_ref.shape[0]):
            @pl.when(z_ref[z] >= 0)
            def _():
                act(zero_copy(z_ref[z], sem_i))

    def for_unused(act):
        def body(hi, carry):
            act(zero_copy(hi * MOE_HALF, 1))
            return carry
        lax.fori_loop(first_unused, n_halves, body, 0)

    @pl.when(i == 0)
    def _():
        zbuf[...] = jnp.zeros_like(zbuf)
        for_each(zf_ref, 0, lambda c: c.start())
        for_each(zl_ref, 1, lambda c: c.start())
        for_unused(lambda c: c.start())
        for_each(zf_ref, 0, lambda c: c.wait())

    def row_copy(r, slot):
        return pltpu.make_async_copy(n_ref.at[pl.ds(r, 1)], xs_hbm.at[pl.ds(slot, 1)], sem)

    def start(r, carry):
        for kk in range(TOP_K):
            row_copy(r, slot_ref[0, 0, r * TOP_K + kk]).start(priority=kk % 2)
        return carry

    lax.fori_loop(0, rows, start, 0)
    for kk in range(TOP_K):
        pltpu.make_async_copy(n_ref, xs_hbm.at[pl.ds(0, rows)], sem).wait()

    @pl.when(i == pl.num_programs(0) - 1)
    def _():
        for_each(zl_ref, 1, lambda c: c.wait())
        for_unused(lambda c: c.wait())


def _dispatch(n_packed, slot, zero_first, zero_later, n_used, n_slots, rows=512):
    t, d2 = n_packed.shape
    rows = min(rows, t)
    grid_spec = pltpu.PrefetchScalarGridSpec(
        num_scalar_prefetch=3,
        grid=(t // rows,),
        in_specs=[pl.BlockSpec((1, 1, rows * TOP_K), lambda i, zf, zl, nu: (i, 0, 0), memory_space=pltpu.SMEM),
                  pl.BlockSpec((rows, d2), lambda i, zf, zl, nu: (i, 0))],
        out_specs=pl.BlockSpec(memory_space=pl.ANY),
        scratch_shapes=[pltpu.VMEM((MOE_HALF, d2), U32), pltpu.SemaphoreType.DMA((2,)),
                        pltpu.SemaphoreType.DMA(())],
    )
    return pl.pallas_call(
        functools.partial(_dispatch_kernel, rows=rows),
        grid_spec=grid_spec,
        out_shape=jax.ShapeDtypeStruct((n_slots, d2), U32),
        compiler_params=_params(("arbitrary",), 32),
        name="moe_dispatch",
    )(zero_first, zero_later, n_used, slot.reshape(t // rows, 1, rows * TOP_K), n_packed)


def _block_state(be_ref, nu_ref, bi):
    used = bi < nu_ref[0]
    fresh = jnp.logical_or(bi == 0, be_ref[bi] != be_ref[jnp.maximum(bi - 1, 0)])
    return used, jnp.logical_and(used, fresh)


def _run_weights(copies, first, fresh, cast, then):
    first_cond, first_args = first

    @pl.when(first_cond)
    def _():
        for c in copies(*first_args):
            c.start(priority=BULK_DMA_PRIORITY)

    @pl.when(fresh)
    def _():
        for c in copies(*first_args):
            c.wait()
        cast()
        for cond, args in then:
            @pl.when(cond)
            def _():
                for c in copies(*args):
                    c.start(priority=BULK_DMA_PRIORITY)


def _cast_rows(src, dst, rows):
    def body(i, carry):
        r = pl.ds(pl.multiple_of(i * rows, rows), rows)
        dst[r, :] = src[r, :].astype(BF16)
        return carry
    lax.fori_loop(0, src.shape[0] // rows, body, 0)


def _per_block_halves(halves, compute, o_ref):
    @pl.when(halves == 2)
    def _():
        compute(MOE_ROWS)

    @pl.when(halves == 1)
    def _():
        compute(MOE_HALF)
        o_ref[MOE_HALF:MOE_ROWS, :] = jnp.zeros((MOE_ROWS - MOE_HALF, o_ref.shape[1]), o_ref.dtype)

    @pl.when(halves == 0)
    def _():
        o_ref[...] = jnp.zeros_like(o_ref)


def _moe_up_kernel(be_ref, nu_ref, nx_ref, bh_ref, x_ref, bg_ref, bl_ref, w_hbm, o_ref, sg, sl, wgb, wlb, sem, *,
                   nf):
    fi, bi = pl.program_id(0), pl.program_id(1)
    _, fresh = _block_state(be_ref, nu_ref, bi)
    tf = sg.shape[1]
    f_all = w_hbm.shape[2] // 2

    def copies(e, f):
        c0 = pl.multiple_of(f * tf, tf)
        return (pltpu.make_async_copy(w_hbm.at[e, :, pl.ds(c0, tf)], sg, sem.at[0]),
                pltpu.make_async_copy(w_hbm.at[e, :, pl.ds(f_all + c0, tf)], sl, sem.at[1]))

    def cast():
        _cast_rows(sg, wgb, CAST_ROWS)
        _cast_rows(sl, wlb, CAST_ROWS)

    nx = nx_ref[bi]
    _run_weights(copies, (jnp.logical_and(fi == 0, bi == 0), (be_ref[0], 0)), fresh, cast,
                 [(nx >= 0, (nx, fi)),
                  (jnp.logical_and(nx < 0, fi + 1 < nf), (be_ref[0], fi + 1))])

    def compute(rows):
        d2 = x_ref.shape[1]
        lo, hi = _unpack_halves(x_ref[0:rows, :])
        lo, hi = lo.astype(BF16), hi.astype(BF16)

        def proj(wb, b_ref):
            return (jnp.dot(lo, wb[0:d2, :], preferred_element_type=F32)
                    + jnp.dot(hi, wb[d2:2 * d2, :], preferred_element_type=F32) + b_ref[0])

        xg = jnp.minimum(proj(wgb, bg_ref), SWIGLU_LIMIT)
        xl = jnp.clip(proj(wlb, bl_ref), -SWIGLU_LIMIT, SWIGLU_LIMIT)
        o_ref[0:rows, :] = (xg * jax.nn.sigmoid(SWIGLU_ALPHA * xg) * (xl + 1.0)).astype(o_ref.dtype)

    _per_block_halves(bh_ref[bi], compute, o_ref)


def _moe_up(xs, block_expert, n_used, next_expert, block_halves, w1, b1, tf=512):
    n_slots, d2 = xs.shape
    e, d, f2 = w1.shape
    f = f2 // 2
    tf = min(tf, f)
    nf = f // tf
    nb = n_slots // MOE_ROWS
    blk = lambda bi, nu: jnp.minimum(bi, nu[0] - 1)
    grid_spec = pltpu.PrefetchScalarGridSpec(
        num_scalar_prefetch=4,
        grid=(nf, nb),
        in_specs=[pl.BlockSpec((MOE_ROWS, d2), lambda fi, bi, be, nu, nx, bh: (blk(bi, nu), 0)),
                  pl.BlockSpec((1, 1, tf), lambda fi, bi, be, nu, nx, bh: (be[blk(bi, nu)], 0, fi)),
                  pl.BlockSpec((1, 1, tf), lambda fi, bi, be, nu, nx, bh: (be[blk(bi, nu)], 0, nf + fi)),
                  pl.BlockSpec(memory_space=pl.ANY)],
        out_specs=pl.BlockSpec((MOE_ROWS, tf), lambda fi, bi, be, nu, nx, bh: (bi, fi)),
        scratch_shapes=[pltpu.VMEM((d, tf), F32), pltpu.VMEM((d, tf), F32),
                        pltpu.VMEM((d, tf), BF16), pltpu.VMEM((d, tf), BF16),
                        pltpu.SemaphoreType.DMA((2,))],
    )
    return pl.pallas_call(
        functools.partial(_moe_up_kernel, nf=nf),
        grid_spec=grid_spec,
        out_shape=jax.ShapeDtypeStruct((n_slots, f), BF16),
        compiler_params=_params(("arbitrary", "arbitrary"), 48),
        name="moe_up",
    )(block_expert, n_used, next_expert, block_halves, xs, b1.reshape(e, 1, f2), b1.reshape(e, 1, f2), w1)


def _moe_down_kernel(be_ref, nu_ref, nx_ref, bh_ref, h_ref, b_ref, w_hbm, o_ref, sw, wb, sem):
    bi = pl.program_id(0)
    _, fresh = _block_state(be_ref, nu_ref, bi)

    def copies(e):
        return (pltpu.make_async_copy(w_hbm.at[e], sw, sem.at[0]),)

    def cast():
        wb[...] = sw[...].astype(BF16)

    nx = nx_ref[bi]
    _run_weights(copies, (bi == 0, (be_ref[0],)), fresh, cast, [(nx >= 0, (nx,))])

    def compute(rows):
        d2 = o_ref.shape[1]
        h = h_ref[0:rows, :]
        y_lo = jnp.dot(h, wb[:, 0:d2], preferred_element_type=F32) + b_ref[0, :, 0:d2]
        y_hi = jnp.dot(h, wb[:, d2:2 * d2], preferred_element_type=F32) + b_ref[0, :, d2:2 * d2]
        o_ref[0:rows, :] = _pack_halves(y_lo, y_hi)

    _per_block_halves(bh_ref[bi], compute, o_ref)


def _moe_down(hid, block_expert, n_used, next_expert, block_halves, w2, b2):
    n_slots, f = hid.shape
    e, _, d = w2.shape
    nb = n_slots // MOE_ROWS
    blk = lambda bi, nu: jnp.minimum(bi, nu[0] - 1)
    grid_spec = pltpu.PrefetchScalarGridSpec(
        num_scalar_prefetch=4,
        grid=(nb,),
        in_specs=[pl.BlockSpec((MOE_ROWS, f), lambda bi, be, nu, nx, bh: (blk(bi, nu), 0)),
                  pl.BlockSpec((1, 1, d), lambda bi, be, nu, nx, bh: (be[blk(bi, nu)], 0, 0)),
                  pl.BlockSpec(memory_space=pl.ANY)],
        out_specs=pl.BlockSpec((MOE_ROWS, d // 2), lambda bi, be, nu, nx, bh: (bi, 0)),
        scratch_shapes=[pltpu.VMEM((f, d), F32), pltpu.VMEM((f, d), BF16), pltpu.SemaphoreType.DMA((1,))],
    )
    return pl.pallas_call(
        _moe_down_kernel,
        grid_spec=grid_spec,
        out_shape=jax.ShapeDtypeStruct((n_slots, d // 2), U32),
        compiler_params=_params(("arbitrary",), 48),
        name="moe_down",
    )(block_expert, n_used, next_expert, block_halves, hid, b2.reshape(e, 1, d), w2)


def _combine_kernel(slot_ref, nslot_ref, gate_ref, h_ref, g_ref, y_hbm, o_ref, *scratch, rows):
    bufs, sem = scratch[:COMBINE_PHASES], scratch[COMBINE_PHASES]
    i = pl.program_id(0)
    d2 = bufs[0].shape[2]
    group = SUBLANES

    def start_rows(s_ref, base, buf, sem_i, r0):
        for r in range(r0, r0 + group):
            for kk in range(TOP_K):
                pltpu.make_async_copy(y_hbm.at[pl.ds(s_ref[0, 0, (base + r) * TOP_K + kk], 1)],
                                      buf.at[kk, pl.ds(r, 1)], sem.at[sem_i]).start(priority=kk % 2)

    def wait_all(buf, sem_i):
        for kk in range(TOP_K):
            pltpu.make_async_copy(y_hbm.at[pl.ds(0, rows)], buf.at[kk], sem.at[sem_i]).wait()

    def compute_rows(buf, base, r0):
        tok = slice(base + r0, base + r0 + group)
        gates = gate_ref[tok, :]
        acc_lo, acc_hi = h_ref[tok, 0:d2], h_ref[tok, d2:2 * d2]
        for kk in range(TOP_K):
            lo, hi = _unpack_halves(buf[kk, r0:r0 + group, :])
            gk = gates[:, kk:kk + 1]
            acc_lo = acc_lo + gk * lo
            acc_hi = acc_hi + gk * hi
        ms = (jnp.sum(acc_lo * acc_lo, axis=-1, keepdims=True)
              + jnp.sum(acc_hi * acc_hi, axis=-1, keepdims=True)) * (1.0 / (2 * d2))
        inv = lax.rsqrt(ms + EPS)
        o_ref[tok, 0:d2] = acc_lo * inv * g_ref[:, 0:d2]
        o_ref[tok, d2:2 * d2] = acc_hi * inv * g_ref[:, d2:2 * d2]

    @pl.when(i == 0)
    def _():
        for p in range(COMBINE_AHEAD):
            def body(r, carry, p=p):
                for kk in range(TOP_K):
                    pltpu.make_async_copy(y_hbm.at[pl.ds(slot_ref[0, 0, (p * rows + r) * TOP_K + kk], 1)],
                                          bufs[p].at[kk, pl.ds(r, 1)], sem.at[p]).start(priority=kk % 2)
                return carry
            lax.fori_loop(0, rows, body, 0)

    for p in range(COMBINE_PHASES):
        wait_all(bufs[p], p)
        q = p + COMBINE_AHEAD
        s_ref, qq = (slot_ref, q) if q < COMBINE_PHASES else (nslot_ref, q - COMBINE_PHASES)
        for r0 in range(0, rows, group):
            start_rows(s_ref, qq * rows, bufs[qq], qq, r0)
            compute_rows(bufs[p], p * rows, r0)

    @pl.when(i == pl.num_programs(0) - 1)
    def _():
        for p in range(COMBINE_AHEAD):
            wait_all(bufs[p], p)


def _combine(y_packed, slot, gates, h, g, rows=64):
    t, d = h.shape
    rows = min(rows, t // COMBINE_PHASES)
    block = COMBINE_PHASES * rows
    nsteps = t // block
    slot3 = slot.reshape(nsteps, 1, block * TOP_K)
    return pl.pallas_call(
        functools.partial(_combine_kernel, rows=rows),
        grid=(nsteps,),
        in_specs=[pl.BlockSpec((1, 1, block * TOP_K), lambda i: (i, 0, 0), memory_space=pltpu.SMEM),
                  pl.BlockSpec((1, 1, block * TOP_K), lambda i: (jnp.minimum(i + 1, nsteps - 1), 0, 0),
                               memory_space=pltpu.SMEM),
                  pl.BlockSpec((block, LANES), lambda i: (i, 0)),
                  pl.BlockSpec((block, d), lambda i: (i, 0)),
                  pl.BlockSpec((1, d), lambda i: (0, 0)),
                  pl.BlockSpec(memory_space=pl.ANY)],
        out_specs=pl.BlockSpec((block, d), lambda i: (i, 0)),
        out_shape=jax.ShapeDtypeStruct((t, d), F32),
        scratch_shapes=([pltpu.VMEM((TOP_K, rows, d // 2), U32) for _ in range(COMBINE_PHASES)]
                        + [pltpu.SemaphoreType.DMA((COMBINE_PHASES,))]),
        compiler_params=_params(("arbitrary",), 40),
        name="moe_combine",
    )(slot3, slot3, gates, h, g.reshape(1, d), y_packed)


def _routing(top_idx, n_exp):
    t = top_idx.shape[0]
    experts = jnp.arange(n_exp, dtype=jnp.int32)
    onehot = (top_idx[:, :, None] == experts[None, None, :]).any(axis=1).astype(jnp.int32)
    incl = jnp.cumsum(onehot, axis=0)
    counts = incl[-1]
    padded = (counts + MOE_ROWS - 1) // MOE_ROWS * MOE_ROWS
    pad_ends = jnp.cumsum(padded)
    slot = jnp.take_along_axis(incl - onehot + (pad_ends - padded)[None, :], top_idx, axis=1)
    n_blocks = -(-(t * TOP_K) // MOE_ROWS) + n_exp
    block_rows = jnp.arange(n_blocks, dtype=jnp.int32) * MOE_ROWS
    block_expert = jnp.minimum(jnp.sum(pad_ends[None, :] <= block_rows[:, None], axis=1), n_exp - 1)
    n_used = (pad_ends[-1] // MOE_ROWS).reshape(1)
    rows_in_block = jnp.clip((counts - padded + pad_ends)[block_expert] - block_rows, 0, MOE_ROWS)
    block_halves = jnp.where(block_rows < pad_ends[-1], jnp.where(rows_in_block <= MOE_HALF, 1, 2), 0)
    last_rows = counts - padded + MOE_ROWS
    live = counts > 0
    zero_first = jnp.concatenate([
        jnp.where(live & (last_rows > MOE_HALF) & (last_rows < MOE_ROWS), pad_ends - MOE_HALF, -1),
        jnp.where(live & (last_rows < MOE_HALF), pad_ends - MOE_ROWS, -1)])
    zero_later = jnp.where(live & (last_rows <= MOE_HALF), pad_ends - MOE_HALF, -1)
    later = jnp.logical_and(experts[None, :] > experts[:, None], padded[None, :] > 0)
    next_of = jnp.min(jnp.where(later, experts[None, :], n_exp), axis=1)
    next_expert = jnp.where(next_of < n_exp, next_of, -1)[block_expert]
    i32 = lambda a: a.astype(jnp.int32)
    return (i32(slot), i32(block_expert), i32(n_used), i32(next_expert), i32(block_halves), i32(zero_first),
            i32(zero_later), n_blocks * MOE_ROWS)


def kernel(x, attn_norm_g, w_in, w_a2, b_a, gla_norm_g, b_glu, w_dw, b_dw, conv_ln_g, conv_ln_b, w_out,
           ffn_norm_g, w_router, b_router, w1, b1, w2, b2, final_norm_g):
    b, s, d = x.shape
    assert w_in.shape[0] == 1, "single-layer stack"
    t = b * s
    dv = gla_norm_g.shape[1]
    dk = w_a2.shape[2] // GLA_HEADS
    lowrank = w_a2.shape[1]
    dc = w_dw.shape[2]
    n_exp = w_router.shape[2]
    c_qkvr = 2 * GLA_HEADS * (dk + dv)
    assert lowrank <= LANES and n_exp <= LANES
    x2 = x.reshape(t, d)

    w_main, w_low = _wprep(jnp.swapaxes(w_in[0], 0, 1), c_qkvr, lowrank)
    n1, a_low = _rmsnorm_lowrank(x2, attn_norm_g[0], w_low)
    proj = _matmul_nt(n1, w_main, "inproj").reshape(b, s, -1)
    a_low = a_low.reshape(b, s, LANES)

    wa = jnp.pad(w_a2[0], ((0, LANES - lowrank), (0, 0))).astype(BF16)
    gla_out = _gla(proj, a_low, wa, b_a[0].reshape(1, -1), gla_norm_g[0].reshape(1, dv), dk, dv)
    conv_out = _conv_module(proj, c_qkvr, dc, b_glu[0], w_dw[0], b_dw[0], conv_ln_g[0], conv_ln_b[0])
    h = _matmul([gla_out.reshape(t, -1), conv_out.reshape(t, dc)], w_out[0], d, "outproj", res=x2, tm=512, tn=1024)

    wr = jnp.pad(w_router[0], ((0, 0), (0, LANES - n_exp))).astype(BF16)
    br = jnp.pad(b_router[0], (0, LANES - n_exp)).reshape(1, LANES)
    n2, idx_pad, gate_pad = _router(h, ffn_norm_g[0], wr, br, n_exp)
    slot, block_expert, n_used, next_expert, block_halves, zero_first, zero_later, n_slots = _routing(
        idx_pad[:, :TOP_K], n_exp)
    xs = _dispatch(n2, slot, zero_first, zero_later, n_used, n_slots)
    hid = _moe_up(xs, block_expert, n_used, next_expert, block_halves, w1[0], b1[0])
    y = _moe_down(hid, block_expert, n_used, next_expert, block_halves, w2[0], b2[0])
    out = _combine(y, slot, gate_pad, h, final_norm_g)
    return out.reshape(b, s, d)
```
